```python
import jax
import jax.numpy as jnp
from jax import lax
import numpy as np

D_MODEL = 1024
BATCH = 8
SEQ = 2048
DEPTH = 2

GRID_W = 64
CTX_LEN = 256
N_BRANCH = 4
BRANCH_W = D_MODEL // 2
HEAD_DIM = 64
A_HEADS = BRANCH_W // HEAD_DIM
A_KV = 2
W_HEADS = BRANCH_W // HEAD_DIM
W_KV = 2
WINDOW = 128
Q_BLOCK = 128
B_GROUPS = 4
B_GROUP_W = BRANCH_W // B_GROUPS
CHUNK = 128
M_HEADS = 4
M_HD = BRANCH_W // M_HEADS
M_CHUNK = 128
CONV_W = 3
ROPE_THETA = 10000.0
EPS = 1e-6

IN_LAYOUT = (
    ('a_q', A_HEADS * HEAD_DIM), ('a_k', A_KV * HEAD_DIM), ('a_v', A_KV * HEAD_DIM), ('a_gate', BRANCH_W),
    ('w_q', W_HEADS * HEAD_DIM), ('w_k', W_KV * HEAD_DIM), ('w_v', W_KV * HEAD_DIM), ('w_gate', BRANCH_W),
    ('g_u', BRANCH_W), ('g_v', BRANCH_W), ('g_gate', BRANCH_W),
    ('m_qkv', 3 * BRANCH_W), ('m_i', 2 * M_HEADS), ('m_f', 2 * M_HEADS), ('m_o', BRANCH_W), ('m_gate', BRANCH_W),
    ('merge', N_BRANCH * D_MODEL),
)
D_IN = sum(size for _, size in IN_LAYOUT)

kernel_name = 'hybrid_prefix_dit_block'


def rms_norm(x, g):
    xf = x.astype(jnp.float32)
    y = xf * lax.rsqrt(jnp.mean(xf * xf, axis=-1, keepdims=True) + EPS)
    return (y * g.astype(jnp.float32)).astype(x.dtype)


def layer_norm(x, g, b):
    xf = x.astype(jnp.float32)
    mu = jnp.mean(xf, axis=-1, keepdims=True)
    xc = xf - mu
    y = xc * lax.rsqrt(jnp.mean(xc * xc, axis=-1, keepdims=True) + EPS)
    return (y * g.astype(jnp.float32) + b.astype(jnp.float32)).astype(x.dtype)


def split_cols(p):
    out = {}
    start = 0
    for name, size in IN_LAYOUT:
        out[name] = p[..., start:start + size]
        start += size
    return out


def split_heads(t, h):
    return t.reshape(t.shape[0], t.shape[1], h, HEAD_DIM)


def axial_rope(rows):
    row = jnp.repeat(jnp.arange(rows), GRID_W).astype(jnp.float32)
    col = jnp.tile(jnp.arange(GRID_W), rows).astype(jnp.float32)
    n_freq = HEAD_DIM // 4
    inv = ROPE_THETA ** (-jnp.arange(n_freq, dtype=jnp.float32) / n_freq)
    ang = jnp.concatenate([row[:, None] * inv, col[:, None] * inv], axis=-1)
    return jnp.cos(ang), jnp.sin(ang)


def apply_rope(x, cos, sin):
    x1, x2 = jnp.split(x.astype(jnp.float32), 2, axis=-1)
    c = cos[None, :, None, :]
    s = sin[None, :, None, :]
    return jnp.concatenate([x1 * c - x2 * s, x1 * s + x2 * c], axis=-1).astype(x.dtype)


def attend_dense(q, k, v):
    s = jnp.einsum('bqhgd,bkhd->bhgqk', q, k).astype(jnp.float32) * HEAD_DIM ** -0.5
    p = jax.nn.softmax(s, axis=-1).astype(v.dtype)
    return jnp.einsum('bhgqk,bkhd->bqhgd', p, v)


def global_attention(q_c, k_c, v_c, q_x, k_x, v_x, q_norm, k_norm, cos, sin, need_ctx):
    B, n, _ = q_x.shape
    G = A_HEADS // A_KV
    qx = apply_rope(rms_norm(split_heads(q_x, A_HEADS), q_norm), cos, sin)
    kx = apply_rope(rms_norm(split_heads(k_x, A_KV), k_norm), cos, sin)
    kc = rms_norm(split_heads(k_c, A_KV), k_norm)
    vc = split_heads(v_c, A_KV)
    k_all = jnp.concatenate([kc, kx], axis=1)
    v_all = jnp.concatenate([vc, split_heads(v_x, A_KV)], axis=1)
    qb = qx.reshape(B, n // Q_BLOCK, Q_BLOCK, A_KV, G, HEAD_DIM).swapaxes(0, 1)
    yx = lax.map(lambda blk: attend_dense(blk, k_all, v_all), qb)
    yx = yx.swapaxes(0, 1).reshape(B, n, A_HEADS * HEAD_DIM)
    yc = None
    if need_ctx:
        qc = rms_norm(split_heads(q_c, A_HEADS), q_norm).reshape(B, -1, A_KV, G, HEAD_DIM)
        yc = attend_dense(qc, kc, vc).reshape(B, -1, A_HEADS * HEAD_DIM)
    return yc, yx


def window_attention(q_c, k_c, v_c, q_x, k_x, v_x, sink, cos, sin, need_ctx):
    B, n, _ = q_x.shape
    G = W_HEADS // W_KV
    nb = n // Q_BLOCK
    scale = HEAD_DIM ** -0.5
    qx = apply_rope(split_heads(q_x, W_HEADS), cos, sin).reshape(B, nb, Q_BLOCK, W_KV, G, HEAD_DIM)
    kx = apply_rope(split_heads(k_x, W_KV), cos, sin)
    vx = split_heads(v_x, W_KV)
    kc = split_heads(k_c, W_KV)
    vc = split_heads(v_c, W_KV)
    n_ctx = kc.shape[1]
    sink_l = sink.astype(jnp.float32).reshape(W_KV, G)[None, :, :, None, None]

    def band(t):
        tp = jnp.pad(t, ((0, 0), (Q_BLOCK, Q_BLOCK), (0, 0), (0, 0)))
        tp = tp.reshape(B, nb + 2, Q_BLOCK, W_KV, HEAD_DIM)
        return jnp.concatenate([tp[:, :-2], tp[:, 1:-1], tp[:, 2:]], axis=2)

    blk = jnp.arange(nb)[:, None, None]
    q_pos = blk * Q_BLOCK + jnp.arange(Q_BLOCK)[None, :, None]
    k_pos = (blk - 1) * Q_BLOCK + jnp.arange(3 * Q_BLOCK)[None, None, :]
    valid = (jnp.abs(k_pos - q_pos) <= WINDOW) & (k_pos >= 0) & (k_pos < n)

    def sink_softmax(s):
        sk = jnp.broadcast_to(sink_l, s.shape[:-1] + (1,))
        p = jax.nn.softmax(jnp.concatenate([s, sk], axis=-1), axis=-1)
        return p[..., :-1]

    def block(args):
        qb, kb, vb, ok = args
        s_ctx = jnp.einsum('bqhgd,bkhd->bhgqk', qb, kc).astype(jnp.float32) * scale
        s_loc = jnp.einsum('bqhgd,bkhd->bhgqk', qb, kb).astype(jnp.float32) * scale
        s_loc = jnp.where(ok, s_loc, -jnp.inf)
        p = sink_softmax(jnp.concatenate([s_ctx, s_loc], axis=-1)).astype(vb.dtype)
        return (jnp.einsum('bhgqk,bkhd->bqhgd', p[..., :n_ctx], vc)
                + jnp.einsum('bhgqk,bkhd->bqhgd', p[..., n_ctx:], vb))

    yx = lax.map(block, (qx.swapaxes(0, 1), band(kx).swapaxes(0, 1), band(vx).swapaxes(0, 1), valid))
    yx = yx.swapaxes(0, 1).reshape(B, n, W_HEADS * HEAD_DIM)
    yc = None
    if need_ctx:
        qc = split_heads(q_c, W_HEADS).reshape(B, n_ctx, W_KV, G, HEAD_DIM)
        s = jnp.einsum('bqhgd,bkhd->bhgqk', qc, kc).astype(jnp.float32) * scale
        p = sink_softmax(s).astype(vc.dtype)
        yc = jnp.einsum('bhgqk,bkhd->bqhgd', p, vc).reshape(B, n_ctx, W_HEADS * HEAD_DIM)
    return yc, yx


def chunk_spatial_gating(u, v, ln_g, ln_b, w_s, b_s):
    B, n, _ = u.shape
    u = jax.nn.gelu(u)
    v = layer_norm(jax.nn.gelu(v), ln_g, ln_b)
    vc = v.reshape(B, n // CHUNK, CHUNK, B_GROUPS, B_GROUP_W)
    mixed = jnp.einsum('gts,bnsgc->bntgc', w_s, vc) + b_s.T[:, :, None]
    return u * mixed.reshape(B, n, BRANCH_W)


def centred_dwconv(x, w):
    pad = CONV_W // 2
    return lax.conv_general_dilated(x, w[:, None, :], window_strides=(1,), padding=[(pad, pad)],
                                    dimension_numbers=('NWC', 'WIO', 'NWC'),
                                    feature_group_count=x.shape[-1])


def mlstm_scan(q, k, v, log_i, log_f, state):
    B, n, H, d = q.shape
    nc = n // M_CHUNK
    causal = jnp.tril(jnp.ones((M_CHUNK, M_CHUNK), dtype=bool))

    def to_chunks(t):
        return t.reshape((B, nc, M_CHUNK) + t.shape[2:]).swapaxes(0, 1)

    def step(carry, xs):
        C, nv, m = carry
        qc, kc, vc, ic, fc = xs
        b = jnp.cumsum(fc, axis=1).transpose(0, 2, 1)
        ic = ic.transpose(0, 2, 1)
        Dm = b[:, :, :, None] - b[:, :, None, :] + ic[:, :, None, :]
        Dm = jnp.where(causal, Dm, -jnp.inf)
        m_inter = b + m[:, :, None]
        m_t = jnp.maximum(m_inter, jnp.max(Dm, axis=-1))
        w_intra = jnp.exp(Dm - m_t[..., None])
        w_inter = jnp.exp(m_inter - m_t)
        s = jnp.einsum('bthd,bshd->bhts', qc, kc) * w_intra
        num = (jnp.einsum('bhts,bshe->bthe', s, vc)
               + w_inter.transpose(0, 2, 1)[..., None] * jnp.einsum('bhed,bthd->bthe', C, qc))
        den = jnp.sum(s, axis=-1) + w_inter * jnp.einsum('bhd,bthd->bht', nv, qc)
        den = jnp.maximum(jnp.abs(den), jnp.exp(-m_t))
        h = num / den.transpose(0, 2, 1)[..., None]
        bL = b[:, :, -1]
        g = bL[:, :, None] - b + ic
        m_new = jnp.maximum(bL + m, jnp.max(g, axis=-1))
        a = jnp.exp(bL + m - m_new)
        wk = jnp.exp(g - m_new[..., None])
        C_new = a[..., None, None] * C + jnp.einsum('bhs,bshe,bshd->bhed', wk, vc, kc)
        n_new = a[..., None] * nv + jnp.einsum('bhs,bshd->bhd', wk, kc)
        return (C_new, n_new, m_new), h

    carry, hs = lax.scan(step, state, (to_chunks(q), to_chunks(k), to_chunks(v), to_chunks(log_i), to_chunks(log_f)))
    return hs.swapaxes(0, 1).reshape(B, n, H, d), carry


def mlstm_mixer(pc, px, conv_w, b_i, b_f, m_norm, need_ctx):
    def prep(p):
        B, n, _ = p['m_qkv'].shape
        z = jax.nn.silu(centred_dwconv(p['m_qkv'], conv_w)).astype(jnp.float32)
        z = z.reshape(B, n, 3, M_HEADS, M_HD)
        q = z[:, :, 0] * M_HD ** -0.5
        log_i = p['m_i'].astype(jnp.float32).reshape(B, n, 2, M_HEADS) + b_i.astype(jnp.float32)
        log_f = jax.nn.log_sigmoid(p['m_f'].astype(jnp.float32).reshape(B, n, 2, M_HEADS) + b_f.astype(jnp.float32))
        return q, z[:, :, 1], z[:, :, 2], log_i, log_f

    def run(seq, direction, state, reverse):
        q, k, v, li, lf = seq
        xs = (q, k, v, li[:, :, direction], lf[:, :, direction])
        if reverse:
            xs = tuple(jnp.flip(t, axis=1) for t in xs)
        h, st = mlstm_scan(xs[0], xs[1], xs[2], xs[3], xs[4], state)
        return (jnp.flip(h, axis=1) if reverse else h), st

    sc = prep(pc)
    sx = prep(px)
    B = px['m_qkv'].shape[0]
    zero = (jnp.zeros((B, M_HEADS, M_HD, M_HD), jnp.float32),
            jnp.zeros((B, M_HEADS, M_HD), jnp.float32),
            jnp.zeros((B, M_HEADS), jnp.float32))
    hc_f, st_f = run(sc, 0, zero, False)
    hx_f, _ = run(sx, 0, st_f, False)
    hc_b, st_b = run(sc, 1, zero, True)
    hx_b, _ = run(sx, 1, st_b, True)

    def finish(h, o):
        hn = rms_norm(h, m_norm.reshape(M_HEADS, M_HD)).reshape(o.shape)
        return (jax.nn.sigmoid(o.astype(jnp.float32)) * hn).astype(o.dtype)

    y_x = finish(hx_f + hx_b, px['m_o'])
    y_c = finish(hc_f + hc_b, pc['m_o']) if need_ctx else None
    return y_c, y_x


def merge_branches(ys, p, w_br, w_o):
    gates = (p['a_gate'], p['w_gate'], p['g_gate'], p['m_gate'])
    y = jnp.stack([yk * jax.nn.silu(gk) for yk, gk in zip(ys, gates)], axis=2)
    proj = jnp.einsum('bnkw,kwd->bnkd', y, w_br)
    m = p['merge']
    g = jax.nn.sigmoid(m.reshape(m.shape[0], m.shape[1], N_BRANCH, D_MODEL))
    return jnp.sum(g * proj, axis=2) @ w_o


def mixer_sublayer(pc, px, cos, sin, q_norm, k_norm, sink, ln_g, ln_b, w_s, b_s,
                   conv_w, b_i, b_f, m_norm, w_br, w_o, need_ctx):
    ya_c, ya_x = global_attention(pc['a_q'], pc['a_k'], pc['a_v'], px['a_q'], px['a_k'], px['a_v'],
                                  q_norm, k_norm, cos, sin, need_ctx)
    yw_c, yw_x = window_attention(pc['w_q'], pc['w_k'], pc['w_v'], px['w_q'], px['w_k'], px['w_v'],
                                  sink, cos, sin, need_ctx)
    yg_x = chunk_spatial_gating(px['g_u'], px['g_v'], ln_g, ln_b, w_s, b_s)
    ym_c, ym_x = mlstm_mixer(pc, px, conv_w, b_i, b_f, m_norm, need_ctx)
    y_x = merge_branches((ya_x, yw_x, yg_x, ym_x), px, w_br, w_o)
    y_c = None
    if need_ctx:
        yg_c = chunk_spatial_gating(pc['g_u'], pc['g_v'], ln_g, ln_b, w_s, b_s)
        y_c = merge_branches((ya_c, yw_c, yg_c, ym_c), pc, w_br, w_o)
    return y_c, y_x


def setup_inputs(seed: int = 0) -> dict:
    key = jax.random.key(seed)
    ks = jax.random.split(key, 24)
    f32 = jnp.float32

    def nrm(k, shape, s):
        return jax.random.normal(k, shape, f32) * s

    return {
        'x': nrm(ks[0], (BATCH, SEQ, D_MODEL), 1.0),
        'c': nrm(ks[1], (BATCH, D_MODEL), 1.0),
        'ctx': nrm(ks[2], (BATCH, CTX_LEN, D_MODEL), 1.0),
        'c_ctx': nrm(ks[3], (D_MODEL,), 1.0),
        'w_mod': nrm(ks[4], (DEPTH, D_MODEL, 3 * D_MODEL), 0.2 * D_MODEL ** -0.5),
        'b_mod': nrm(ks[5], (DEPTH, 3 * D_MODEL), 0.02),
        'g_pre': 1.0 + nrm(ks[6], (DEPTH, D_MODEL), 0.02),
        'g_post': 1.0 + nrm(ks[7], (DEPTH, D_MODEL), 0.02),
        'w_in': nrm(ks[8], (DEPTH, D_MODEL, D_IN), D_MODEL ** -0.5),
        'a_q_norm': 1.0 + nrm(ks[9], (DEPTH, HEAD_DIM), 0.02),
        'a_k_norm': 1.0 + nrm(ks[10], (DEPTH, HEAD_DIM), 0.02),
        'w_sink': nrm(ks[11], (DEPTH, W_HEADS), 0.5),
        'sg_ln_g': 1.0 + nrm(ks[12], (DEPTH, BRANCH_W), 0.02),
        'sg_ln_b': nrm(ks[13], (DEPTH, BRANCH_W), 0.02),
        'sg_w': nrm(ks[14], (DEPTH, B_GROUPS, CHUNK, CHUNK), CHUNK ** -0.5),
        'sg_b': 1.0 + nrm(ks[15], (DEPTH, B_GROUPS, CHUNK), 0.1),
        'm_conv': nrm(ks[16], (DEPTH, CONV_W, 3 * BRANCH_W), CONV_W ** -0.5),
        'm_b_i': nrm(ks[17], (DEPTH, 2, M_HEADS), 0.1),
        'm_b_f': jnp.linspace(3.0, 6.0, M_HEADS, dtype=f32) + nrm(ks[18], (DEPTH, 2, M_HEADS), 0.1),
        'm_norm': 1.0 + nrm(ks[19], (DEPTH, BRANCH_W), 0.02),
        'w_branch': nrm(ks[20], (DEPTH, N_BRANCH, BRANCH_W, D_MODEL), BRANCH_W ** -0.5),
        'w_out': nrm(ks[21], (DEPTH, D_MODEL, D_MODEL), D_MODEL ** -0.5),
    }


def reference(x, c, ctx, c_ctx, w_mod, b_mod, g_pre, g_post, w_in, a_q_norm, a_k_norm, w_sink,
              sg_ln_g, sg_ln_b, sg_w, sg_b, m_conv, m_b_i, m_b_f, m_norm, w_branch, w_out):
    rows = x.shape[1] // GRID_W
    cos, sin = axial_rope(rows)
    h_ctx = ctx
    for l in range(DEPTH):
        need_ctx = l < DEPTH - 1
        mod_x = jax.nn.silu(c) @ w_mod[l] + b_mod[l]
        mod_c = jax.nn.silu(c_ctx) @ w_mod[l] + b_mod[l]
        shift_x, scale_x, gate_x = jnp.split(mod_x[:, None, :], 3, axis=-1)
        shift_c, scale_c, gate_c = jnp.split(mod_c, 3, axis=-1)
        in_x = rms_norm(x, g_pre[l]) * (1.0 + scale_x) + shift_x
        in_c = rms_norm(h_ctx, g_pre[l]) * (1.0 + scale_c) + shift_c
        y_c, y_x = mixer_sublayer(split_cols(in_c @ w_in[l]), split_cols(in_x @ w_in[l]), cos, sin,
                                  a_q_norm[l], a_k_norm[l], w_sink[l], sg_ln_g[l], sg_ln_b[l], sg_w[l], sg_b[l],
                                  m_conv[l], m_b_i[l], m_b_f[l], m_norm[l], w_branch[l], w_out[l], need_ctx)
        x = x + gate_x * rms_norm(y_x, g_post[l])
        if need_ctx:
            h_ctx = h_ctx + gate_c * rms_norm(y_c, g_post[l])
    return x
```

```python
import functools

import jax
import jax.numpy as jnp
from jax import lax
from jax.experimental import pallas as pl
from jax.experimental.pallas import tpu as pltpu

F32 = jnp.float32
BF16 = jnp.bfloat16

D_MODEL = 1024
GRID_W = 64
CTX_LEN = 256
N_BRANCH = 4
BRANCH_W = 512
HEAD_DIM = 64
N_HEADS = 8
N_KV = 2
GROUP = N_HEADS // N_KV
WINDOW = 128
CHUNK = 128
B_GROUPS = 4
M_HEADS = 4
M_HD = 128
ROPE_THETA = 10000.0
EPS = 1e-6

LANES = 128
VMEM_LIMIT = 48 * 1024 * 1024

COL_MERGE = 0
COL_AQ = 4096
COL_MQKV = 4608
COL_AK = 6144
COL_AV = 6272
COL_WK = 6400
COL_WV = 6528
COL_WQ = 6656
COL_GU = 7168
COL_GV = 7680
COL_AGATE = 8192
COL_WGATE = 8704
COL_GGATE = 9216
COL_MGATE = 9728
COL_MO = 10240
N_MAIN = 10752

_REF_LAYOUT = (
    ('a_q', 512), ('a_k', 128), ('a_v', 128), ('a_gate', 512),
    ('w_q', 512), ('w_k', 128), ('w_v', 128), ('w_gate', 512),
    ('g_u', 512), ('g_v', 512), ('g_gate', 512),
    ('m_qkv', 1536), ('m_i', 8), ('m_f', 8), ('m_o', 512), ('m_gate', 512),
    ('merge', 4096),
)
_MAIN_ORDER = ('merge', 'a_q', 'm_qkv', 'a_k', 'a_v', 'w_k', 'w_v', 'w_q', 'g_u', 'g_v',
               'a_gate', 'w_gate', 'g_gate', 'm_gate', 'm_o')


def _ref_slices():
    out, start = {}, 0
    for name, size in _REF_LAYOUT:
        out[name] = (start, size)
        start += size
    return out


def _params(*sem):
    return pltpu.CompilerParams(dimension_semantics=sem, vmem_limit_bytes=VMEM_LIMIT)


def _sigmoid(x):
    return 1.0 / (1.0 + jnp.exp(-x))


def _silu(x):
    return x * _sigmoid(x)


def _gelu_tanh(x):
    c = 0.7978845608028654
    return 0.5 * x * (1.0 + jnp.tanh(c * (x + 0.044715 * (x * x * x))))


def _log_sigmoid(x):
    return -(jnp.maximum(-x, 0.0) + jnp.log1p(jnp.exp(-jnp.abs(x))))


def _mod_kernel(c_ref, w_ref, b_ref, o_ref):
    a = _silu(c_ref[...]).astype(BF16)
    o_ref[...] = jnp.dot(a, w_ref[...], preferred_element_type=F32) + b_ref[...]


def _modulation(cc, w_mod_bf, b_mod):
    rows = cc.shape[0]
    n = w_mod_bf.shape[1]
    tn = 1024
    return pl.pallas_call(
        _mod_kernel,
        grid=(n // tn,),
        in_specs=[pl.BlockSpec((rows, D_MODEL), lambda j: (0, 0)),
                  pl.BlockSpec((D_MODEL, tn), lambda j: (0, j)),
                  pl.BlockSpec((1, tn), lambda j: (0, j))],
        out_specs=pl.BlockSpec((rows, tn), lambda j: (0, j)),
        out_shape=jax.ShapeDtypeStruct((rows, n), F32),
        compiler_params=_params("parallel"),
        name="modulation",
    )(cc, w_mod_bf, b_mod)


def _inproj_kernel(h_ref, mod_ref, w_ref, o_ref, xn_ref, *, tm):
    t = pl.program_id(1)
    j = pl.program_id(2)

    @pl.when(j == 0)
    def _():
        x = h_ref[0]
        ms = jnp.mean(x * x, axis=-1, keepdims=True)
        xh = x * lax.rsqrt(ms + EPS)
        row = t * tm + lax.broadcasted_iota(jnp.int32, (tm, 1), 0)
        is_ctx = row < CTX_LEN
        mult = jnp.where(is_ctx, mod_ref[0, 0:1, :], mod_ref[0, 2:3, :])
        shift = jnp.where(is_ctx, mod_ref[0, 1:2, :], mod_ref[0, 3:4, :])
        xn_ref[...] = (xh * mult + shift).astype(BF16)

    o_ref[0] = jnp.dot(xn_ref[...], w_ref[...], preferred_element_type=F32).astype(o_ref.dtype)


def _in_projection(h, mod4, w_bf, out_dtype, tm, tn):
    B, T, _ = h.shape
    n = w_bf.shape[1]
    return pl.pallas_call(
        functools.partial(_inproj_kernel, tm=tm),
        grid=(B, T // tm, n // tn),
        in_specs=[pl.BlockSpec((1, tm, D_MODEL), lambda b, t, j: (b, t, 0)),
                  pl.BlockSpec((1, 4, D_MODEL), lambda b, t, j: (b, 0, 0)),
                  pl.BlockSpec((D_MODEL, tn), lambda b, t, j: (0, j))],
        out_specs=pl.BlockSpec((1, tm, tn), lambda b, t, j: (b, t, j)),
        out_shape=jax.ShapeDtypeStruct((B, T, n), out_dtype),
        scratch_shapes=[pltpu.VMEM((tm, D_MODEL), BF16)],
        compiler_params=_params("parallel", "parallel", "arbitrary"),
        name="in_projection",
    )(h, mod4, w_bf)


def _rope_slab(x, cos, sin):
    lane = lax.broadcasted_iota(jnp.int32, x.shape, 1)
    first_half = (lane & (HEAD_DIM - 1)) < (HEAD_DIM // 2)
    partner = jnp.where(first_half,
                        pltpu.roll(x, LANES - HEAD_DIM // 2, axis=1),
                        pltpu.roll(x, HEAD_DIM // 2, axis=1))
    return x * cos + partner * sin


def _head_norm_slab(x, gain, gmat):
    ss = jnp.dot((x * x).astype(BF16), gmat, preferred_element_type=F32)
    return x * lax.rsqrt(ss * (1.0 / HEAD_DIM) + EPS) * gain


def _stack_heads(slabs):
    parts = []
    for s in slabs:
        parts.append(s[:, :HEAD_DIM])
        parts.append(s[:, HEAD_DIM:])
    return jnp.concatenate(parts, axis=0)


def _unstack_heads(o, n):
    return jnp.concatenate([o[h * n:(h + 1) * n, :] for h in range(GROUP)], axis=1)


def _prep_kv(k_ref, v_ref, cosk_ref, sink_ref, kn_ref, gmat_ref, kp_ref, vp_ref, nk, use_norm):
    k = k_ref[0, 0:nk, :].astype(F32)
    if use_norm:
        k = _head_norm_slab(k, kn_ref[...], gmat_ref[...])
    k = _rope_slab(k, cosk_ref[0:nk, :], sink_ref[0:nk, :]).astype(BF16)
    v = v_ref[0, 0:nk, :]
    for g in range(N_KV):
        kp_ref[g, 0:nk, :] = k[:, g * HEAD_DIM:(g + 1) * HEAD_DIM]
        vp_ref[g, 0:nk, :] = v[:, g * HEAD_DIM:(g + 1) * HEAD_DIM]


def _prep_q(q_ref, cosq_ref, sinq_ref, qn_ref, gmat_ref, use_norm):
    cos = cosq_ref[...]
    sin = sinq_ref[...]
    slabs = []
    for s in range(N_HEADS // 2):
        x = q_ref[0, :, s * LANES:(s + 1) * LANES].astype(F32)
        if use_norm:
            x = _head_norm_slab(x, qn_ref[...], gmat_ref[...])
        x = _rope_slab(x, cos, sin) * (HEAD_DIM ** -0.5)
        slabs.append(x.astype(BF16))
    per = GROUP // 2
    return [_stack_heads(slabs[g * per:(g + 1) * per]) for g in range(N_KV)]


def _qk(q, k):
    return lax.dot_general(q, k, (((1,), (1,)), ((), ())), preferred_element_type=F32)


def _gattn_kernel(q_ref, k_ref, v_ref, cosq_ref, sinq_ref, cosk_ref, sink_ref, qn_ref, kn_ref, gmat_ref,
                  o_ref, kp_ref, vp_ref, s_ref, *, tq, nk, ck):
    i = pl.program_id(1)

    @pl.when(i == 0)
    def _():
        _prep_kv(k_ref, v_ref, cosk_ref, sink_ref, kn_ref, gmat_ref, kp_ref, vp_ref, nk, True)

    qs = _prep_q(q_ref, cosq_ref, sinq_ref, qn_ref, gmat_ref, True)
    n_chunks = nk // ck
    rows = GROUP * tq
    outs = []
    for g in range(N_KV):
        q = qs[g]

        def score_body(c, m):
            off = pl.multiple_of(c * ck, ck)
            s = _qk(q, kp_ref[g, pl.ds(off, ck), :])
            s_ref[c] = s
            return jnp.maximum(m, jnp.max(s, axis=-1, keepdims=True))

        m = lax.fori_loop(0, n_chunks, score_body, jnp.full((rows, 1), -jnp.inf, F32))

        def pv_body(c, carry):
            l, acc = carry
            off = pl.multiple_of(c * ck, ck)
            p = jnp.exp(s_ref[c] - m)
            l = l + jnp.sum(p, axis=-1, keepdims=True)
            acc = acc + jnp.dot(p.astype(BF16), vp_ref[g, pl.ds(off, ck), :], preferred_element_type=F32)
            return l, acc

        l, acc = lax.fori_loop(0, n_chunks, pv_body,
                               (jnp.zeros((rows, 1), F32), jnp.zeros((rows, HEAD_DIM), F32)))
        outs.append(_unstack_heads(acc / l, tq))
    o_ref[0] = jnp.concatenate(outs, axis=1).astype(o_ref.dtype)


def _global_attention(p_main, rope, qn, kn, gmat, *, q_tile0, n_q_tiles, nk):
    B, T, _ = p_main.shape
    cos, sin = rope
    tq = CHUNK
    ck = 256
    kernel = functools.partial(_gattn_kernel, tq=tq, nk=nk, ck=ck)
    qcol = COL_AQ // BRANCH_W
    return pl.pallas_call(
        kernel,
        grid=(B, n_q_tiles),
        in_specs=[pl.BlockSpec((1, tq, BRANCH_W), lambda b, i: (b, i + q_tile0, qcol)),
                  pl.BlockSpec((1, T, LANES), lambda b, i: (b, 0, COL_AK // LANES)),
                  pl.BlockSpec((1, T, LANES), lambda b, i: (b, 0, COL_AV // LANES)),
                  pl.BlockSpec((tq, LANES), lambda b, i: (i + q_tile0, 0)),
                  pl.BlockSpec((tq, LANES), lambda b, i: (i + q_tile0, 0)),
                  pl.BlockSpec((T, LANES), lambda b, i: (0, 0)),
                  pl.BlockSpec((T, LANES), lambda b, i: (0, 0)),
                  pl.BlockSpec((1, LANES), lambda b, i: (0, 0)),
                  pl.BlockSpec((1, LANES), lambda b, i: (0, 0)),
                  pl.BlockSpec((LANES, LANES), lambda b, i: (0, 0))],
        out_specs=pl.BlockSpec((1, tq, BRANCH_W), lambda b, i: (b, i, 0)),
        out_shape=jax.ShapeDtypeStruct((B, n_q_tiles * tq, BRANCH_W), BF16),
        scratch_shapes=[pltpu.VMEM((N_KV, nk, HEAD_DIM), BF16),
                        pltpu.VMEM((N_KV, nk, HEAD_DIM), BF16),
                        pltpu.VMEM((nk // ck, GROUP * tq, ck), F32)],
        compiler_params=_params("parallel", "arbitrary"),
        name="global_attention",
    )(p_main, p_main, p_main, cos, sin, cos, sin, qn, kn, gmat)


def _wattn_kernel(wsink_ref, q_ref, k_ref, v_ref, cosq_ref, sinq_ref, cosk_ref, sink_ref,
                  o_ref, kp_ref, vp_ref, *, tq, nk, banded, n_lat_blocks):
    i = pl.program_id(1)

    @pl.when(i == 0)
    def _():
        _prep_kv(k_ref, v_ref, cosk_ref, sink_ref, None, None, kp_ref, vp_ref, nk, False)

    qs = _prep_q(q_ref, cosq_ref, sinq_ref, None, None, False)
    rows = GROUP * tq
    n_ctx_chunks = CTX_LEN // CHUNK
    row_id = lax.broadcasted_iota(jnp.int32, (rows, 1), 0)

    if banded:
        prev_off = pl.multiple_of((i + n_ctx_chunks - 1) * CHUNK, CHUNK)
        own_off = pl.multiple_of((i + n_ctx_chunks) * CHUNK, CHUNK)
        nxt = jnp.minimum(i + n_ctx_chunks + 1, n_ctx_chunks + n_lat_blocks - 1)
        next_off = pl.multiple_of(nxt * CHUNK, CHUNK)
        qq = lax.broadcasted_iota(jnp.int32, (rows, CHUNK), 0) & (tq - 1)
        kk = lax.broadcasted_iota(jnp.int32, (rows, CHUNK), 1)
        ok_prev = jnp.logical_and(kk >= qq, i > 0)
        ok_next = jnp.logical_and(kk <= qq, i < n_lat_blocks - 1)

    outs = []
    for g in range(N_KV):
        q = qs[g]
        sink_col = jnp.zeros((rows, 1), F32)
        for h in range(GROUP):
            in_head = jnp.logical_and(row_id >= h * tq, row_id < (h + 1) * tq)
            sink_col = jnp.where(in_head, wsink_ref[g * GROUP + h], sink_col)
        s_ctx = _qk(q, kp_ref[g, 0:CTX_LEN, :])
        m = jnp.maximum(jnp.max(s_ctx, axis=-1, keepdims=True), sink_col)
        if banded:
            s_prev = jnp.where(ok_prev, _qk(q, kp_ref[g, pl.ds(prev_off, CHUNK), :]), -jnp.inf)
            s_own = _qk(q, kp_ref[g, pl.ds(own_off, CHUNK), :])
            s_next = jnp.where(ok_next, _qk(q, kp_ref[g, pl.ds(next_off, CHUNK), :]), -jnp.inf)
            for s in (s_prev, s_own, s_next):
                m = jnp.maximum(m, jnp.max(s, axis=-1, keepdims=True))
        p_ctx = jnp.exp(s_ctx - m)
        l = jnp.exp(sink_col - m) + jnp.sum(p_ctx, axis=-1, keepdims=True)
        acc = jnp.dot(p_ctx.astype(BF16), vp_ref[g, 0:CTX_LEN, :], preferred_element_type=F32)
        if banded:
            for s, off in ((s_prev, prev_off), (s_own, own_off), (s_next, next_off)):
                p = jnp.exp(s - m)
                l = l + jnp.sum(p, axis=-1, keepdims=True)
                acc = acc + jnp.dot(p.astype(BF16), vp_ref[g, pl.ds(off, CHUNK), :],
                                    preferred_element_type=F32)
        outs.append(_unstack_heads(acc / l, tq))
    o_ref[0] = jnp.concatenate(outs, axis=1).astype(o_ref.dtype)


def _window_attention(p_main, rope, w_sink, *, q_tile0, n_q_tiles, nk, banded):
    B, T, _ = p_main.shape
    cos, sin = rope
    tq = CHUNK
    n_lat_blocks = (T - CTX_LEN) // CHUNK
    kernel = functools.partial(_wattn_kernel, tq=tq, nk=nk, banded=banded, n_lat_blocks=n_lat_blocks)
    qcol = COL_WQ // BRANCH_W
    return pl.pallas_call(
        kernel,
        grid=(B, n_q_tiles),
        in_specs=[pl.BlockSpec(memory_space=pltpu.SMEM),
                  pl.BlockSpec((1, tq, BRANCH_W), lambda b, i: (b, i + q_tile0, qcol)),
                  pl.BlockSpec((1, T, LANES), lambda b, i: (b, 0, COL_WK // LANES)),
                  pl.BlockSpec((1, T, LANES), lambda b, i: (b, 0, COL_WV // LANES)),
                  pl.BlockSpec((tq, LANES), lambda b, i: (i + q_tile0, 0)),
                  pl.BlockSpec((tq, LANES), lambda b, i: (i + q_tile0, 0)),
                  pl.BlockSpec((T, LANES), lambda b, i: (0, 0)),
                  pl.BlockSpec((T, LANES), lambda b, i: (0, 0))],
        out_specs=pl.BlockSpec((1, tq, BRANCH_W), lambda b, i: (b, i, 0)),
        out_shape=jax.ShapeDtypeStruct((B, n_q_tiles * tq, BRANCH_W), BF16),
        scratch_shapes=[pltpu.VMEM((N_KV, nk, HEAD_DIM), BF16),
                        pltpu.VMEM((N_KV, nk, HEAD_DIM), BF16)],
        compiler_params=_params("parallel", "arbitrary"),
        name="window_attention",
    )(w_sink, p_main, p_main, p_main, cos, sin, cos, sin)


def _sgate_kernel(u_ref, v_ref, lng_ref, lnb_ref, ws_ref, bs_ref, o_ref, *, tm):
    u = _gelu_tanh(u_ref[0].astype(F32))
    v = _gelu_tanh(v_ref[0].astype(F32))
    mu = jnp.mean(v, axis=-1, keepdims=True)
    vc = v - mu
    var = jnp.mean(vc * vc, axis=-1, keepdims=True)
    vn = (vc * lax.rsqrt(var + EPS) * lng_ref[...] + lnb_ref[...]).astype(BF16)
    gw = BRANCH_W // B_GROUPS
    chunks = []
    for c in range(tm // CHUNK):
        cols = []
        for g in range(B_GROUPS):
            mixed = jnp.dot(ws_ref[g], vn[c * CHUNK:(c + 1) * CHUNK, g * gw:(g + 1) * gw],
                            preferred_element_type=F32)
            cols.append(mixed + bs_ref[:, g:g + 1])
        chunks.append(jnp.concatenate(cols, axis=1))
    o_ref[0] = (u * jnp.concatenate(chunks, axis=0)).astype(o_ref.dtype)


def _spatial_gating(p_main, ln_g, ln_b, ws_bf, bs_t, *, tile0, n_tiles, tm):
    B, T, _ = p_main.shape
    return pl.pallas_call(
        functools.partial(_sgate_kernel, tm=tm),
        grid=(B, n_tiles),
        in_specs=[pl.BlockSpec((1, tm, BRANCH_W), lambda b, t: (b, t + tile0, COL_GU // BRANCH_W)),
                  pl.BlockSpec((1, tm, BRANCH_W), lambda b, t: (b, t + tile0, COL_GV // BRANCH_W)),
                  pl.BlockSpec((1, BRANCH_W), lambda b, t: (0, 0)),
                  pl.BlockSpec((1, BRANCH_W), lambda b, t: (0, 0)),
                  pl.BlockSpec((B_GROUPS, CHUNK, CHUNK), lambda b, t: (0, 0, 0)),
                  pl.BlockSpec((CHUNK, B_GROUPS), lambda b, t: (0, 0))],
        out_specs=pl.BlockSpec((1, tm, BRANCH_W), lambda b, t: (b, t, 0)),
        out_shape=jax.ShapeDtypeStruct((B, n_tiles * tm, BRANCH_W), BF16),
        compiler_params=_params("parallel", "parallel"),
        name="spatial_gating",
    )(p_main, p_main, ln_g, ln_b, ws_bf, bs_t)


def _conv_kernel(x_ref, w_ref, o_ref, *, T):
    j = pl.program_id(1)
    x = x_ref[0].astype(F32)
    row = lax.broadcasted_iota(jnp.int32, (T, 1), 0)
    prev = pltpu.roll(x, 1, axis=0)
    nxt = pltpu.roll(x, T - 1, axis=0)
    prev = jnp.where(jnp.logical_or(row == 0, row == CTX_LEN), 0.0, prev)
    nxt = jnp.where(jnp.logical_or(row == CTX_LEN - 1, row == T - 1), 0.0, nxt)
    y = prev * w_ref[0:1, :] + x * w_ref[1:2, :] + nxt * w_ref[2:3, :]
    z = _silu(y)
    n_q_slabs = BRANCH_W // LANES
    z = z * jnp.where(j < n_q_slabs, M_HD ** -0.5, 1.0)
    o_ref[0] = z.astype(o_ref.dtype)


def _mlstm_conv(p_main, conv_w):
    B, T, _ = p_main.shape
    n_slabs = 3 * BRANCH_W // LANES
    return pl.pallas_call(
        functools.partial(_conv_kernel, T=T),
        grid=(B, n_slabs),
        in_specs=[pl.BlockSpec((1, T, LANES), lambda b, j: (b, 0, COL_MQKV // LANES + j)),
                  pl.BlockSpec((3, LANES), lambda b, j: (0, j))],
        out_specs=pl.BlockSpec((1, T, LANES), lambda b, j: (b, 0, j)),
        out_shape=jax.ShapeDtypeStruct((B, T, 3 * BRANCH_W), BF16),
        compiler_params=_params("parallel", "parallel"),
        name="mlstm_conv",
    )(p_main, conv_w)


def _mlstm_chain(d, q, k, v, gates, gates_t, bi_ref, bf_ref, st_ref, m_ref, hd_idx):
    L = CHUNK
    h = hd_idx
    col = d * M_HEADS + h
    b_i = bi_ref[col]
    b_f = bf_ref[col]
    i_col = gates[:, col:col + 1] + b_i
    f_col = _log_sigmoid(gates[:, 8 + col:9 + col] + b_f)
    i_row = gates_t[col:col + 1, :] + b_i
    f_row = _log_sigmoid(gates_t[8 + col:9 + col, :] + b_f)

    t_idx = lax.broadcasted_iota(jnp.int32, (L, L), 0)
    s_idx = lax.broadcasted_iota(jnp.int32, (L, L), 1)
    mask = (s_idx <= t_idx) if d == 0 else (s_idx >= t_idx)
    mask_t = (t_idx <= s_idx) if d == 0 else (t_idx >= s_idx)
    b_col = jnp.sum(jnp.where(mask, f_row, 0.0), axis=1, keepdims=True)
    b_row = jnp.sum(jnp.where(mask_t, f_col, 0.0), axis=0, keepdims=True)
    b_last = jnp.sum(f_row, axis=1, keepdims=True)
    u_row = i_row - b_row
    u_col = i_col - b_col

    m_prev = m_ref[d, h][0:1, 0:1]
    st_prev = st_ref[d, h]

    dm = jnp.where(mask, b_col + u_row, -jnp.inf)
    m_inter = b_col + m_prev
    m_t = jnp.maximum(m_inter, jnp.max(dm, axis=1, keepdims=True))
    w_intra = jnp.exp(dm - m_t)
    w_inter = jnp.exp(m_inter - m_t)
    s = _qk(q, k) * w_intra
    inter = jnp.dot(q, st_prev.astype(BF16), preferred_element_type=F32)
    num = jnp.dot(s.astype(BF16), v, preferred_element_type=F32) + w_inter * inter[:, :M_HD]
    den = jnp.sum(s, axis=1, keepdims=True) + w_inter * inter[:, M_HD:M_HD + 1]
    den = jnp.maximum(jnp.abs(den), jnp.exp(-m_t))
    h_out = num / den

    g_row = b_last + u_row
    g_col = b_last + u_col
    m_new = jnp.maximum(b_last + m_prev, jnp.max(g_row, axis=1, keepdims=True))
    a = jnp.exp(b_last + m_prev - m_new)
    wk_col = jnp.exp(g_col - m_new)
    lane = lax.broadcasted_iota(jnp.int32, (L, M_HD), 1)
    v_ext = jnp.concatenate([v.astype(F32), jnp.where(lane == 0, 1.0, 0.0)], axis=1)
    x = (v_ext * wk_col).astype(BF16)
    k_t = k.astype(F32).T.astype(BF16)
    upd = jnp.dot(k_t, x, preferred_element_type=F32)
    st_ref[d, h] = a * st_prev + upd
    m_ref[d, h] = jnp.broadcast_to(m_new, (8, LANES))
    return h_out


def _mlstm_kernel(bi_ref, bf_ref, zf_ref, zb_ref, gf_ref, gb_ref, of_ref, ob_ref, st_ref, m_ref):
    @pl.when(pl.program_id(1) == 0)
    def _():
        st_ref[...] = jnp.zeros_like(st_ref)
        m_ref[...] = jnp.zeros_like(m_ref)

    for d, (z_ref, g_ref, o_ref) in enumerate(((zf_ref, gf_ref, of_ref), (zb_ref, gb_ref, ob_ref))):
        gates = g_ref[0]
        gates_t = gates.T
        outs = []
        for h in range(M_HEADS):
            q = z_ref[0, :, h * M_HD:(h + 1) * M_HD]
            k = z_ref[0, :, BRANCH_W + h * M_HD:BRANCH_W + (h + 1) * M_HD]
            v = z_ref[0, :, 2 * BRANCH_W + h * M_HD:2 * BRANCH_W + (h + 1) * M_HD]
            outs.append(_mlstm_chain(d, q, k, v, gates, gates_t, bi_ref, bf_ref, st_ref, m_ref, h))
        o_ref[0, 0] = jnp.concatenate(outs, axis=1).astype(o_ref.dtype)


def _mlstm_scan(z, gates, b_i, b_f):
    B, T, _ = z.shape
    n_chunks = T // CHUNK
    n_ctx = CTX_LEN // CHUNK

    def fwd_chunk(j):
        return j

    def bwd_chunk(j):
        return jnp.where(j < n_ctx, n_ctx - 1 - j, n_chunks + n_ctx - 1 - j)

    zw = 3 * BRANCH_W
    out = pl.pallas_call(
        _mlstm_kernel,
        grid=(B, n_chunks),
        in_specs=[pl.BlockSpec(memory_space=pltpu.SMEM),
                  pl.BlockSpec(memory_space=pltpu.SMEM),
                  pl.BlockSpec((1, CHUNK, zw), lambda b, j: (b, fwd_chunk(j), 0)),
                  pl.BlockSpec((1, CHUNK, zw), lambda b, j: (b, bwd_chunk(j), 0)),
                  pl.BlockSpec((1, CHUNK, LANES), lambda b, j: (b, fwd_chunk(j), 0)),
                  pl.BlockSpec((1, CHUNK, LANES), lambda b, j: (b, bwd_chunk(j), 0))],
        out_specs=[pl.BlockSpec((1, 1, CHUNK, BRANCH_W), lambda b, j: (0, b, fwd_chunk(j), 0)),
                   pl.BlockSpec((1, 1, CHUNK, BRANCH_W), lambda b, j: (0, b, bwd_chunk(j), 0))],
        out_shape=[jax.ShapeDtypeStruct((1, B, T, BRANCH_W), F32),
                   jax.ShapeDtypeStruct((1, B, T, BRANCH_W), F32)],
        scratch_shapes=[pltpu.VMEM((2, M_HEADS, M_HD, 2 * M_HD), F32),
                        pltpu.VMEM((2, M_HEADS, 8, LANES), F32)],
        compiler_params=_params("parallel", "arbitrary"),
        name="mlstm_scan",
    )(b_i, b_f, z, z, gates, gates)
    return out[0][0], out[1][0]


def _merge_kernel(ya_ref, yw_ref, yg_ref, hf_ref, hb_ref, ga_ref, gw_ref, gg_ref, gm_ref, mo_ref, mg_ref,
                  h_ref, gate_ref, wbr_ref, wo_ref, gpost_ref, mnorm_ref, o_ref):
    hm = hf_ref[0] + hb_ref[0]
    parts = []
    for hd in range(M_HEADS):
        x = hm[:, hd * M_HD:(hd + 1) * M_HD]
        ms = jnp.mean(x * x, axis=-1, keepdims=True)
        parts.append(x * lax.rsqrt(ms + EPS) * mnorm_ref[:, hd * M_HD:(hd + 1) * M_HD])
    ym = _sigmoid(mo_ref[0].astype(F32)) * jnp.concatenate(parts, axis=1)
    ys = (ya_ref[0].astype(F32), yw_ref[0].astype(F32), yg_ref[0].astype(F32), ym)
    gates = (ga_ref, gw_ref, gg_ref, gm_ref)
    acc = None
    for k in range(N_BRANCH):
        yk = (ys[k] * _silu(gates[k][0].astype(F32))).astype(BF16)
        proj = jnp.dot(yk, wbr_ref[k], preferred_element_type=F32)
        g = _sigmoid(mg_ref[0, :, k * D_MODEL:(k + 1) * D_MODEL].astype(F32))
        acc = g * proj if acc is None else acc + g * proj
    y = jnp.dot(acc.astype(BF16), wo_ref[...], preferred_element_type=F32)
    ms = jnp.mean(y * y, axis=-1, keepdims=True)
    yn = y * lax.rsqrt(ms + EPS) * gpost_ref[...]
    o_ref[0] = h_ref[0] + gate_ref[0, 0] * yn


def _merge(p_main, ya, yw, yg, hf, hb, h, gate2, wbr_bf, wo_bf, g_post, m_norm, *, tile0, n_tiles, tm):
    B, T, _ = h.shape
    n_ctx_tiles = CTX_LEN // tm
    t_out = n_tiles * tm

    def tok(w, col):
        return pl.BlockSpec((1, tm, w), lambda b, t: (b, t + tile0, col))

    def branch():
        return pl.BlockSpec((1, tm, BRANCH_W), lambda b, t: (b, t, 0))

    def const(shape):
        nd = len(shape)
        return pl.BlockSpec(shape, lambda b, t: (0,) * nd)

    bw = BRANCH_W
    return pl.pallas_call(
        _merge_kernel,
        grid=(B, n_tiles),
        in_specs=[branch(), branch(), branch(), tok(bw, 0), tok(bw, 0),
                  tok(bw, COL_AGATE // bw), tok(bw, COL_WGATE // bw), tok(bw, COL_GGATE // bw),
                  tok(bw, COL_MGATE // bw), tok(bw, COL_MO // bw),
                  tok(N_BRANCH * D_MODEL, COL_MERGE // (N_BRANCH * D_MODEL)),
                  tok(D_MODEL, 0),
                  pl.BlockSpec((1, 1, 1, D_MODEL),
                               lambda b, t: (b, jnp.where(t + tile0 < n_ctx_tiles, 0, 1), 0, 0)),
                  const((N_BRANCH, bw, D_MODEL)), const((D_MODEL, D_MODEL)),
                  const((1, D_MODEL)), const((1, bw))],
        out_specs=pl.BlockSpec((1, tm, D_MODEL), lambda b, t: (b, t, 0)),
        out_shape=jax.ShapeDtypeStruct((B, t_out, D_MODEL), F32),
        compiler_params=_params("parallel", "parallel"),
        name="merge",
    )(ya, yw, yg, hf, hb, p_main, p_main, p_main, p_main, p_main, p_main, h, gate2,
      wbr_bf, wo_bf, g_post, m_norm)


def _rope_tables(n_lat):
    rows = n_lat // GRID_W
    row = jnp.repeat(jnp.arange(rows), GRID_W).astype(F32)
    col = jnp.tile(jnp.arange(GRID_W), rows).astype(F32)
    n_freq = HEAD_DIM // 4
    inv = ROPE_THETA ** (-jnp.arange(n_freq, dtype=F32) / n_freq)
    ang = jnp.concatenate([row[:, None] * inv, col[:, None] * inv], axis=-1)
    cos, sin = jnp.cos(ang), jnp.sin(ang)
    cos_h = jnp.concatenate([cos, cos], axis=-1)
    sin_h = jnp.concatenate([-sin, sin], axis=-1)
    cos_t = jnp.concatenate([jnp.ones((CTX_LEN, HEAD_DIM), F32), cos_h], axis=0)
    sin_t = jnp.concatenate([jnp.zeros((CTX_LEN, HEAD_DIM), F32), sin_h], axis=0)
    return jnp.tile(cos_t, (1, LANES // HEAD_DIM)), jnp.tile(sin_t, (1, LANES // HEAD_DIM))


def _reorder_w_in(w):
    sl = _ref_slices()
    main = jnp.concatenate([w[:, sl[n][0]:sl[n][0] + sl[n][1]] for n in _MAIN_ORDER], axis=1)
    gate_cols = jnp.concatenate([w[:, sl['m_i'][0]:sl['m_i'][0] + 8], w[:, sl['m_f'][0]:sl['m_f'][0] + 8]],
                                axis=1)
    gate_cols = jnp.pad(gate_cols, ((0, 0), (0, LANES - 16)))
    return main.astype(BF16), gate_cols.astype(BF16)


def kernel(x, c, ctx, c_ctx, w_mod, b_mod, g_pre, g_post, w_in, a_q_norm, a_k_norm, w_sink, sg_ln_g, sg_ln_b,
           sg_w, sg_b, m_conv, m_b_i, m_b_f, m_norm, w_branch, w_out):
    B, n_lat, D = x.shape
    depth = w_mod.shape[0]
    T = CTX_LEN + n_lat
    n_ctx_chunks = CTX_LEN // CHUNK
    n_lat_chunks = n_lat // CHUNK
    rope = _rope_tables(n_lat)
    gmat = (jnp.arange(LANES)[:, None] // HEAD_DIM == jnp.arange(LANES)[None, :] // HEAD_DIM).astype(BF16)

    h = jnp.concatenate([ctx, x], axis=1)
    mod_rows = 16
    cc = jnp.zeros((mod_rows, D), F32).at[:B].set(c).at[B].set(c_ctx)
    tm_tok = 256

    for l in range(depth):
        need_ctx = l < depth - 1
        mod = _modulation(cc, w_mod[l].astype(BF16), b_mod[l][None, :])
        shift, scale, gate = mod[:, :D], mod[:, D:2 * D], mod[:, 2 * D:]
        mult = g_pre[l][None, :] * (1.0 + scale)
        mod4 = jnp.stack([jnp.broadcast_to(mult[B], (B, D)), jnp.broadcast_to(shift[B], (B, D)),
                          mult[:B], shift[:B]], axis=1)
        gate2 = jnp.stack([jnp.broadcast_to(gate[B], (B, D)), gate[:B]], axis=1)[:, :, None, :]

        w_main, w_gates = _reorder_w_in(w_in[l])
        p_main = _in_projection(h, mod4, w_main, BF16, tm=1152, tn=512)
        p_gates = _in_projection(h, mod4, w_gates, F32, tm=1152, tn=LANES)

        qn = jnp.tile(a_q_norm[l], LANES // HEAD_DIM)[None, :]
        kn = jnp.tile(a_k_norm[l], LANES // HEAD_DIM)[None, :]
        ya = _global_attention(p_main, rope, qn, kn, gmat,
                               q_tile0=n_ctx_chunks, n_q_tiles=n_lat_chunks, nk=T)
        yw = _window_attention(p_main, rope, w_sink[l], q_tile0=n_ctx_chunks, n_q_tiles=n_lat_chunks,
                               nk=T, banded=True)
        if need_ctx:
            ya_c = _global_attention(p_main, rope, qn, kn, gmat, q_tile0=0, n_q_tiles=n_ctx_chunks, nk=CTX_LEN)
            yw_c = _window_attention(p_main, rope, w_sink[l], q_tile0=0, n_q_tiles=n_ctx_chunks,
                                     nk=CTX_LEN, banded=False)
            ya = jnp.concatenate([ya_c, ya], axis=1)
            yw = jnp.concatenate([yw_c, yw], axis=1)

        tile0 = 0 if need_ctx else CTX_LEN // tm_tok
        n_tiles = T // tm_tok - tile0
        yg = _spatial_gating(p_main, sg_ln_g[l][None, :], sg_ln_b[l][None, :], sg_w[l].astype(BF16),
                             sg_b[l].T, tile0=tile0, n_tiles=n_tiles, tm=tm_tok)

        z = _mlstm_conv(p_main, m_conv[l])
        hf, hb = _mlstm_scan(z, p_gates, m_b_i[l].reshape(-1), m_b_f[l].reshape(-1))

        out = _merge(p_main, ya, yw, yg, hf, hb, h, gate2, w_branch[l].astype(BF16), w_out[l].astype(BF16),
                     g_post[l][None, :], m_norm[l][None, :], tile0=tile0, n_tiles=n_tiles, tm=tm_tok)
        if need_ctx:
            h = out
        else:
            return out
    return h[:, CTX_LEN:]
```

```python
import functools

import jax
import jax.numpy as jnp
from jax import lax
from jax.experimental import pallas as pl
from jax.experimental.pallas import tpu as pltpu

F32 = jnp.float32
BF16 = jnp.bfloat16

D_MODEL = 1024
GRID_W = 64
CTX_LEN = 256
N_BRANCH = 4
BRANCH_W = 512
HEAD_DIM = 64
N_HEADS = 8
N_KV = 2
GROUP = N_HEADS // N_KV
WINDOW = 128
CHUNK = 128
B_GROUPS = 4
M_HEADS = 4
M_HD = 128
ROPE_THETA = 10000.0
EPS = 1e-6

LANES = 128
VMEM_LIMIT = 48 * 1024 * 1024

COL_MERGE = 0
COL_AQ = 4096
COL_MQKV = 4608
COL_AK = 6144
COL_AV = 6272
COL_WK = 6400
COL_WV = 6528
COL_WQ = 6656
COL_GU = 7168
COL_GV = 7680
COL_AGATE = 8192
COL_WGATE = 8704
COL_GGATE = 9216
COL_MGATE = 9728
COL_MO = 10240
N_MAIN = 10752

_REF_LAYOUT = (
    ('a_q', 512), ('a_k', 128), ('a_v', 128), ('a_gate', 512),
    ('w_q', 512), ('w_k', 128), ('w_v', 128), ('w_gate', 512),
    ('g_u', 512), ('g_v', 512), ('g_gate', 512),
    ('m_qkv', 1536), ('m_i', 8), ('m_f', 8), ('m_o', 512), ('m_gate', 512),
    ('merge', 4096),
)
_MAIN_ORDER = ('merge', 'a_q', 'm_qkv', 'a_k', 'a_v', 'w_k', 'w_v', 'w_q', 'g_u', 'g_v',
               'a_gate', 'w_gate', 'g_gate', 'm_gate', 'm_o')


def _ref_slices():
    out, start = {}, 0
    for name, size in _REF_LAYOUT:
        out[name] = (start, size)
        start += size
    return out


def _params(*sem):
    return pltpu.CompilerParams(dimension_semantics=sem, vmem_limit_bytes=VMEM_LIMIT)


def _sigmoid(x):
    return 1.0 / (1.0 + jnp.exp(-x))


def _silu(x):
    return x * _sigmoid(x)


def _gelu_tanh(x):
    c = 0.7978845608028654
    return 0.5 * x * (1.0 + jnp.tanh(c * (x + 0.044715 * (x * x * x))))


def _log_sigmoid(x):
    return -(jnp.maximum(-x, 0.0) + jnp.log1p(jnp.exp(-jnp.abs(x))))


def _mod_kernel(c_ref, w_ref, b_ref, o_ref):
    a = _silu(c_ref[...]).astype(BF16)
    o_ref[...] = jnp.dot(a, w_ref[...], preferred_element_type=F32) + b_ref[...]


def _modulation(cc, w_mod_bf, b_mod):
    rows = cc.shape[0]
    n = w_mod_bf.shape[1]
    tn = 1024
    return pl.pallas_call(
        _mod_kernel,
        grid=(n // tn,),
        in_specs=[pl.BlockSpec((rows, D_MODEL), lambda j: (0, 0)),
                  pl.BlockSpec((D_MODEL, tn), lambda j: (0, j)),
                  pl.BlockSpec((1, tn), lambda j: (0, j))],
        out_specs=pl.BlockSpec((rows, tn), lambda j: (0, j)),
        out_shape=jax.ShapeDtypeStruct((rows, n), F32),
        compiler_params=_params("parallel"),
        name="modulation",
    )(cc, w_mod_bf, b_mod)


def _inproj_kernel(h_ref, mod_ref, w_ref, o_ref, xn_ref, *, tm):
    t = pl.program_id(1)
    j = pl.program_id(2)

    @pl.when(j == 0)
    def _():
        rb = 256

        def norm_rows(r, carry):
            off = pl.multiple_of(r * rb, rb)
            x = h_ref[0, pl.ds(off, rb), :]
            ms = jnp.mean(x * x, axis=-1, keepdims=True)
            xh = x * lax.rsqrt(ms + EPS)
            row = t * tm + off + lax.broadcasted_iota(jnp.int32, (rb, 1), 0)
            is_ctx = row < CTX_LEN
            mult = jnp.where(is_ctx, mod_ref[0, 0:1, :], mod_ref[0, 2:3, :])
            shift = jnp.where(is_ctx, mod_ref[0, 1:2, :], mod_ref[0, 3:4, :])
            xn_ref[pl.ds(off, rb), :] = (xh * mult + shift).astype(BF16)
            return carry

        lax.fori_loop(0, tm // rb, norm_rows, 0)

    o_ref[0] = jnp.dot(xn_ref[...], w_ref[...], preferred_element_type=F32).astype(o_ref.dtype)


def _in_projection(h, mod4, w_bf, out_dtype, tm, tn):
    B, T, _ = h.shape
    n = w_bf.shape[1]
    return pl.pallas_call(
        functools.partial(_inproj_kernel, tm=tm),
        grid=(B, T // tm, n // tn),
        in_specs=[pl.BlockSpec((1, tm, D_MODEL), lambda b, t, j: (b, t, 0)),
                  pl.BlockSpec((1, 4, D_MODEL), lambda b, t, j: (b, 0, 0)),
                  pl.BlockSpec((D_MODEL, tn), lambda b, t, j: (0, j))],
        out_specs=pl.BlockSpec((1, tm, tn), lambda b, t, j: (b, t, j)),
        out_shape=jax.ShapeDtypeStruct((B, T, n), out_dtype),
        scratch_shapes=[pltpu.VMEM((tm, D_MODEL), BF16)],
        compiler_params=_params("parallel", "parallel", "arbitrary"),
        name="in_projection",
    )(h, mod4, w_bf)


def _rope_slab(x, cos, sin):
    lane = lax.broadcasted_iota(jnp.int32, x.shape, 1)
    first_half = (lane & (HEAD_DIM - 1)) < (HEAD_DIM // 2)
    partner = jnp.where(first_half,
                        pltpu.roll(x, LANES - HEAD_DIM // 2, axis=1),
                        pltpu.roll(x, HEAD_DIM // 2, axis=1))
    return x * cos + partner * sin


def _head_norm_slab(x, gain, gmat):
    ss = jnp.dot((x * x).astype(BF16), gmat, preferred_element_type=F32)
    return x * lax.rsqrt(ss * (1.0 / HEAD_DIM) + EPS) * gain


def _stack_heads(slabs):
    parts = []
    for s in slabs:
        parts.append(s[:, :HEAD_DIM])
        parts.append(s[:, HEAD_DIM:])
    return jnp.concatenate(parts, axis=0)


def _unstack_heads(o, n):
    return jnp.concatenate([o[h * n:(h + 1) * n, :] for h in range(GROUP)], axis=1)


def _prep_kv(k_ref, v_ref, cosk_ref, sink_ref, kn_ref, gmat_ref, kp_ref, vp_ref, nk, use_norm):
    k = k_ref[0, 0:nk, :].astype(F32)
    if use_norm:
        k = _head_norm_slab(k, kn_ref[...], gmat_ref[...])
    k = _rope_slab(k, cosk_ref[0:nk, :], sink_ref[0:nk, :]).astype(BF16)
    v = v_ref[0, 0:nk, :]
    for g in range(N_KV):
        kp_ref[g, 0:nk, :] = k[:, g * HEAD_DIM:(g + 1) * HEAD_DIM]
        vp_ref[g, 0:nk, :] = v[:, g * HEAD_DIM:(g + 1) * HEAD_DIM]


def _prep_q(q_ref, cosq_ref, sinq_ref, qn_ref, gmat_ref, use_norm):
    cos = cosq_ref[...]
    sin = sinq_ref[...]
    slabs = []
    for s in range(N_HEADS // 2):
        x = q_ref[0, :, s * LANES:(s + 1) * LANES].astype(F32)
        if use_norm:
            x = _head_norm_slab(x, qn_ref[...], gmat_ref[...])
        x = _rope_slab(x, cos, sin) * (HEAD_DIM ** -0.5)
        slabs.append(x.astype(BF16))
    per = GROUP // 2
    return [_stack_heads(slabs[g * per:(g + 1) * per]) for g in range(N_KV)]


def _qk(q, k):
    return lax.dot_general(q, k, (((1,), (1,)), ((), ())), preferred_element_type=F32)


def _gattn_kernel(q_ref, k_ref, v_ref, cosq_ref, sinq_ref, cosk_ref, sink_ref, qn_ref, kn_ref, gmat_ref,
                  o_ref, kt_ref, vp_ref, s_ref, *, tq, nk, ck):
    i = pl.program_id(1)

    @pl.when(i == 0)
    def _():
        k = k_ref[0, 0:nk, :].astype(F32)
        k = _head_norm_slab(k, kn_ref[...], gmat_ref[...])
        k = _rope_slab(k, cosk_ref[0:nk, :], sink_ref[0:nk, :])
        kt_ref[...] = k.T.astype(BF16)
        v = v_ref[0, 0:nk, :]
        for g in range(N_KV):
            vp_ref[g] = v[:, g * HEAD_DIM:(g + 1) * HEAD_DIM]

    n_chunks = nk // ck
    cos = cosq_ref[...]
    sin = sinq_ref[...]
    outs = []
    for slab in range(N_HEADS // 2):
        x = q_ref[0, :, slab * LANES:(slab + 1) * LANES].astype(F32)
        x = _head_norm_slab(x, qn_ref[...], gmat_ref[...])
        x = (_rope_slab(x, cos, sin) * (HEAD_DIM ** -0.5)).astype(BF16)
        for hh in range(2):
            h = 2 * slab + hh
            g = h // GROUP
            q = x[:, hh * HEAD_DIM:(hh + 1) * HEAD_DIM]
            m_lane = None
            for c in range(n_chunks):
                sc = jnp.dot(q, kt_ref[g * HEAD_DIM:(g + 1) * HEAD_DIM, c * ck:(c + 1) * ck],
                             preferred_element_type=F32)
                s_ref[h, c] = sc
                for j in range(ck // LANES):
                    part = sc[:, j * LANES:(j + 1) * LANES]
                    m_lane = part if m_lane is None else jnp.maximum(m_lane, part)
            m = jnp.max(m_lane, axis=-1, keepdims=True)
            l_lane = None
            acc = None
            for c in range(n_chunks):
                p = jnp.exp(s_ref[h, c] - m)
                for j in range(ck // LANES):
                    part = p[:, j * LANES:(j + 1) * LANES]
                    l_lane = part if l_lane is None else l_lane + part
                pv = jnp.dot(p.astype(BF16), vp_ref[g, c * ck:(c + 1) * ck, :], preferred_element_type=F32)
                acc = pv if acc is None else acc + pv
            l = jnp.sum(l_lane, axis=-1, keepdims=True)
            outs.append(acc / l)
    o_ref[0] = jnp.concatenate(outs, axis=1).astype(o_ref.dtype)


def _global_attention(p_main, rope, qn, kn, gmat, *, q_tile0, n_q_tiles, nk):
    B, T, _ = p_main.shape
    cos, sin = rope
    tq = CHUNK
    ck = 256
    kernel = functools.partial(_gattn_kernel, tq=tq, nk=nk, ck=ck)
    qcol = COL_AQ // BRANCH_W
    return pl.pallas_call(
        kernel,
        grid=(B, n_q_tiles),
        in_specs=[pl.BlockSpec((1, tq, BRANCH_W), lambda b, i: (b, i + q_tile0, qcol)),
                  pl.BlockSpec((1, T, LANES), lambda b, i: (b, 0, COL_AK // LANES)),
                  pl.BlockSpec((1, T, LANES), lambda b, i: (b, 0, COL_AV // LANES)),
                  pl.BlockSpec((tq, LANES), lambda b, i: (i + q_tile0, 0)),
                  pl.BlockSpec((tq, LANES), lambda b, i: (i + q_tile0, 0)),
                  pl.BlockSpec((T, LANES), lambda b, i: (0, 0)),
                  pl.BlockSpec((T, LANES), lambda b, i: (0, 0)),
                  pl.BlockSpec((1, LANES), lambda b, i: (0, 0)),
                  pl.BlockSpec((1, LANES), lambda b, i: (0, 0)),
                  pl.BlockSpec((LANES, LANES), lambda b, i: (0, 0))],
        out_specs=pl.BlockSpec((1, tq, BRANCH_W), lambda b, i: (b, i, 0)),
        out_shape=jax.ShapeDtypeStruct((B, n_q_tiles * tq, BRANCH_W), BF16),
        scratch_shapes=[pltpu.VMEM((LANES, nk), BF16),
                        pltpu.VMEM((N_KV, nk, HEAD_DIM), BF16),
                        pltpu.VMEM((N_HEADS, nk // ck, tq, ck), F32)],
        compiler_params=_params("parallel", "arbitrary"),
        name="global_attention",
    )(p_main, p_main, p_main, cos, sin, cos, sin, qn, kn, gmat)


def _wattn_kernel(wsink_ref, q_ref, k_ref, v_ref, cosq_ref, sinq_ref, cosk_ref, sink_ref,
                  o_ref, kp_ref, vp_ref, *, tq, nk, banded, n_lat_blocks):
    i = pl.program_id(1)

    @pl.when(i == 0)
    def _():
        _prep_kv(k_ref, v_ref, cosk_ref, sink_ref, None, None, kp_ref, vp_ref, nk, False)

    qs = _prep_q(q_ref, cosq_ref, sinq_ref, None, None, False)
    rows = GROUP * tq
    n_ctx_chunks = CTX_LEN // CHUNK
    row_id = lax.broadcasted_iota(jnp.int32, (rows, 1), 0)

    if banded:
        prev_off = pl.multiple_of((i + n_ctx_chunks - 1) * CHUNK, CHUNK)
        own_off = pl.multiple_of((i + n_ctx_chunks) * CHUNK, CHUNK)
        nxt = jnp.minimum(i + n_ctx_chunks + 1, n_ctx_chunks + n_lat_blocks - 1)
        next_off = pl.multiple_of(nxt * CHUNK, CHUNK)
        qq = lax.broadcasted_iota(jnp.int32, (rows, CHUNK), 0) & (tq - 1)
        kk = lax.broadcasted_iota(jnp.int32, (rows, CHUNK), 1)
        ok_prev = jnp.logical_and(kk >= qq, i > 0)
        ok_next = jnp.logical_and(kk <= qq, i < n_lat_blocks - 1)

    outs = []
    for g in range(N_KV):
        q = qs[g]
        sink_col = jnp.zeros((rows, 1), F32)
        for h in range(GROUP):
            in_head = jnp.logical_and(row_id >= h * tq, row_id < (h + 1) * tq)
            sink_col = jnp.where(in_head, wsink_ref[g * GROUP + h], sink_col)
        s_ctx = _qk(q, kp_ref[g, 0:CTX_LEN, :])
        m = jnp.maximum(jnp.max(s_ctx, axis=-1, keepdims=True), sink_col)
        if banded:
            s_prev = jnp.where(ok_prev, _qk(q, kp_ref[g, pl.ds(prev_off, CHUNK), :]), -jnp.inf)
            s_own = _qk(q, kp_ref[g, pl.ds(own_off, CHUNK), :])
            s_next = jnp.where(ok_next, _qk(q, kp_ref[g, pl.ds(next_off, CHUNK), :]), -jnp.inf)
            for s in (s_prev, s_own, s_next):
                m = jnp.maximum(m, jnp.max(s, axis=-1, keepdims=True))
        p_ctx = jnp.exp(s_ctx - m)
        l = jnp.exp(sink_col - m) + jnp.sum(p_ctx, axis=-1, keepdims=True)
        acc = jnp.dot(p_ctx.astype(BF16), vp_ref[g, 0:CTX_LEN, :], preferred_element_type=F32)
        if banded:
            for s, off in ((s_prev, prev_off), (s_own, own_off), (s_next, next_off)):
                p = jnp.exp(s - m)
                l = l + jnp.sum(p, axis=-1, keepdims=True)
                acc = acc + jnp.dot(p.astype(BF16), vp_ref[g, pl.ds(off, CHUNK), :],
                                    preferred_element_type=F32)
        outs.append(_unstack_heads(acc / l, tq))
    o_ref[0] = jnp.concatenate(outs, axis=1).astype(o_ref.dtype)


def _window_attention(p_main, rope, w_sink, *, q_tile0, n_q_tiles, nk, banded):
    B, T, _ = p_main.shape
    cos, sin = rope
    tq = CHUNK
    n_lat_blocks = (T - CTX_LEN) // CHUNK
    kernel = functools.partial(_wattn_kernel, tq=tq, nk=nk, banded=banded, n_lat_blocks=n_lat_blocks)
    qcol = COL_WQ // BRANCH_W
    return pl.pallas_call(
        kernel,
        grid=(B, n_q_tiles),
        in_specs=[pl.BlockSpec(memory_space=pltpu.SMEM),
                  pl.BlockSpec((1, tq, BRANCH_W), lambda b, i: (b, i + q_tile0, qcol)),
                  pl.BlockSpec((1, T, LANES), lambda b, i: (b, 0, COL_WK // LANES)),
                  pl.BlockSpec((1, T, LANES), lambda b, i: (b, 0, COL_WV // LANES)),
                  pl.BlockSpec((tq, LANES), lambda b, i: (i + q_tile0, 0)),
                  pl.BlockSpec((tq, LANES), lambda b, i: (i + q_tile0, 0)),
                  pl.BlockSpec((T, LANES), lambda b, i: (0, 0)),
                  pl.BlockSpec((T, LANES), lambda b, i: (0, 0))],
        out_specs=pl.BlockSpec((1, tq, BRANCH_W), lambda b, i: (b, i, 0)),
        out_shape=jax.ShapeDtypeStruct((B, n_q_tiles * tq, BRANCH_W), BF16),
        scratch_shapes=[pltpu.VMEM((N_KV, nk, HEAD_DIM), BF16),
                        pltpu.VMEM((N_KV, nk, HEAD_DIM), BF16)],
        compiler_params=_params("parallel", "arbitrary"),
        name="window_attention",
    )(w_sink, p_main, p_main, p_main, cos, sin, cos, sin)


def _sgate_kernel(u_ref, v_ref, lng_ref, lnb_ref, ws_ref, bs_ref, o_ref, *, tm):
    u = _gelu_tanh(u_ref[0].astype(F32))
    v = _gelu_tanh(v_ref[0].astype(F32))
    mu = jnp.mean(v, axis=-1, keepdims=True)
    vc = v - mu
    var = jnp.mean(vc * vc, axis=-1, keepdims=True)
    vn = (vc * lax.rsqrt(var + EPS) * lng_ref[...] + lnb_ref[...]).astype(BF16)
    gw = BRANCH_W // B_GROUPS
    chunks = []
    for c in range(tm // CHUNK):
        cols = []
        for g in range(B_GROUPS):
            mixed = jnp.dot(ws_ref[g], vn[c * CHUNK:(c + 1) * CHUNK, g * gw:(g + 1) * gw],
                            preferred_element_type=F32)
            cols.append(mixed + bs_ref[:, g:g + 1])
        chunks.append(jnp.concatenate(cols, axis=1))
    o_ref[0] = (u * jnp.concatenate(chunks, axis=0)).astype(o_ref.dtype)


def _spatial_gating(p_main, ln_g, ln_b, ws_bf, bs_t, *, tile0, n_tiles, tm):
    B, T, _ = p_main.shape
    return pl.pallas_call(
        functools.partial(_sgate_kernel, tm=tm),
        grid=(B, n_tiles),
        in_specs=[pl.BlockSpec((1, tm, BRANCH_W), lambda b, t: (b, t + tile0, COL_GU // BRANCH_W)),
                  pl.BlockSpec((1, tm, BRANCH_W), lambda b, t: (b, t + tile0, COL_GV // BRANCH_W)),
                  pl.BlockSpec((1, BRANCH_W), lambda b, t: (0, 0)),
                  pl.BlockSpec((1, BRANCH_W), lambda b, t: (0, 0)),
                  pl.BlockSpec((B_GROUPS, CHUNK, CHUNK), lambda b, t: (0, 0, 0)),
                  pl.BlockSpec((CHUNK, B_GROUPS), lambda b, t: (0, 0))],
        out_specs=pl.BlockSpec((1, tm, BRANCH_W), lambda b, t: (b, t, 0)),
        out_shape=jax.ShapeDtypeStruct((B, n_tiles * tm, BRANCH_W), BF16),
        compiler_params=_params("parallel", "parallel"),
        name="spatial_gating",
    )(p_main, p_main, ln_g, ln_b, ws_bf, bs_t)


def _conv_kernel(x_ref, w_ref, o_ref, *, T):
    j = pl.program_id(1)
    x = x_ref[0].astype(F32)
    row = lax.broadcasted_iota(jnp.int32, (T, 1), 0)
    prev = pltpu.roll(x, 1, axis=0)
    nxt = pltpu.roll(x, T - 1, axis=0)
    prev = jnp.where(jnp.logical_or(row == 0, row == CTX_LEN), 0.0, prev)
    nxt = jnp.where(jnp.logical_or(row == CTX_LEN - 1, row == T - 1), 0.0, nxt)
    y = prev * w_ref[0:1, :] + x * w_ref[1:2, :] + nxt * w_ref[2:3, :]
    z = _silu(y)
    n_q_slabs = BRANCH_W // LANES
    z = z * jnp.where(j < n_q_slabs, M_HD ** -0.5, 1.0)
    o_ref[0] = z.astype(o_ref.dtype)


def _mlstm_conv(p_main, conv_w):
    B, T, _ = p_main.shape
    n_slabs = 3 * BRANCH_W // LANES
    return pl.pallas_call(
        functools.partial(_conv_kernel, T=T),
        grid=(B, n_slabs),
        in_specs=[pl.BlockSpec((1, T, LANES), lambda b, j: (b, 0, COL_MQKV // LANES + j)),
                  pl.BlockSpec((3, LANES), lambda b, j: (0, j))],
        out_specs=pl.BlockSpec((1, T, LANES), lambda b, j: (b, 0, j)),
        out_shape=jax.ShapeDtypeStruct((B, T, 3 * BRANCH_W), BF16),
        compiler_params=_params("parallel", "parallel"),
        name="mlstm_conv",
    )(p_main, conv_w)


def _mlstm_chain(d, q, k, v, gates, gates_t, bi_ref, bf_ref, st_ref, m_ref, hd_idx):
    L = CHUNK
    h = hd_idx
    col = d * M_HEADS + h
    b_i = bi_ref[col]
    b_f = bf_ref[col]
    i_col = gates[:, col:col + 1] + b_i
    f_col = _log_sigmoid(gates[:, 8 + col:9 + col] + b_f)
    i_row = gates_t[col:col + 1, :] + b_i
    f_row = _log_sigmoid(gates_t[8 + col:9 + col, :] + b_f)

    t_idx = lax.broadcasted_iota(jnp.int32, (L, L), 0)
    s_idx = lax.broadcasted_iota(jnp.int32, (L, L), 1)
    mask = (s_idx <= t_idx) if d == 0 else (s_idx >= t_idx)
    mask_t = (t_idx <= s_idx) if d == 0 else (t_idx >= s_idx)
    b_col = jnp.sum(jnp.where(mask, f_row, 0.0), axis=1, keepdims=True)
    b_row = jnp.sum(jnp.where(mask_t, f_col, 0.0), axis=0, keepdims=True)
    b_last = jnp.sum(f_row, axis=1, keepdims=True)
    u_row = i_row - b_row
    u_col = i_col - b_col

    m_prev = m_ref[d, h][0:1, 0:1]
    st_prev = st_ref[d, h]

    dm = jnp.where(mask, b_col + u_row, -jnp.inf)
    m_inter = b_col + m_prev
    m_t = jnp.maximum(m_inter, jnp.max(dm, axis=1, keepdims=True))
    w_intra = jnp.exp(dm - m_t)
    w_inter = jnp.exp(m_inter - m_t)
    s = _qk(q, k) * w_intra
    inter = jnp.dot(q, st_prev.astype(BF16), preferred_element_type=F32)
    num = jnp.dot(s.astype(BF16), v, preferred_element_type=F32) + w_inter * inter[:, :M_HD]
    den = jnp.sum(s, axis=1, keepdims=True) + w_inter * inter[:, M_HD:M_HD + 1]
    den = jnp.maximum(jnp.abs(den), jnp.exp(-m_t))
    h_out = num / den

    g_row = b_last + u_row
    g_col = b_last + u_col
    m_new = jnp.maximum(b_last + m_prev, jnp.max(g_row, axis=1, keepdims=True))
    a = jnp.exp(b_last + m_prev - m_new)
    wk_col = jnp.exp(g_col - m_new)
    lane = lax.broadcasted_iota(jnp.int32, (L, M_HD), 1)
    v_ext = jnp.concatenate([v.astype(F32), jnp.where(lane == 0, 1.0, 0.0)], axis=1)
    x = (v_ext * wk_col).astype(BF16)
    k_t = k.astype(F32).T.astype(BF16)
    upd = jnp.dot(k_t, x, preferred_element_type=F32)
    st_ref[d, h] = a * st_prev + upd
    m_ref[d, h] = jnp.broadcast_to(m_new, (8, LANES))
    return h_out


def _mlstm_kernel(bi_ref, bf_ref, zf_ref, zb_ref, gf_ref, gb_ref, of_ref, ob_ref, st_ref, m_ref):
    @pl.when(pl.program_id(1) == 0)
    def _():
        st_ref[...] = jnp.zeros_like(st_ref)
        m_ref[...] = jnp.zeros_like(m_ref)

    for d, (z_ref, g_ref, o_ref) in enumerate(((zf_ref, gf_ref, of_ref), (zb_ref, gb_ref, ob_ref))):
        gates = g_ref[0]
        gates_t = gates.T
        outs = []
        for h in range(M_HEADS):
            q = z_ref[0, :, h * M_HD:(h + 1) * M_HD]
            k = z_ref[0, :, BRANCH_W + h * M_HD:BRANCH_W + (h + 1) * M_HD]
            v = z_ref[0, :, 2 * BRANCH_W + h * M_HD:2 * BRANCH_W + (h + 1) * M_HD]
            outs.append(_mlstm_chain(d, q, k, v, gates, gates_t, bi_ref, bf_ref, st_ref, m_ref, h))
        o_ref[0, 0] = jnp.concatenate(outs, axis=1).astype(o_ref.dtype)


def _mlstm_scan(z, gates, b_i, b_f):
    B, T, _ = z.shape
    n_chunks = T // CHUNK
    n_ctx = CTX_LEN // CHUNK

    def fwd_chunk(j):
        return j

    def bwd_chunk(j):
        return jnp.where(j < n_ctx, n_ctx - 1 - j, n_chunks + n_ctx - 1 - j)

    zw = 3 * BRANCH_W
    out = pl.pallas_call(
        _mlstm_kernel,
        grid=(B, n_chunks),
        in_specs=[pl.BlockSpec(memory_space=pltpu.SMEM),
                  pl.BlockSpec(memory_space=pltpu.SMEM),
                  pl.BlockSpec((1, CHUNK, zw), lambda b, j: (b, fwd_chunk(j), 0)),
                  pl.BlockSpec((1, CHUNK, zw), lambda b, j: (b, bwd_chunk(j), 0)),
                  pl.BlockSpec((1, CHUNK, LANES), lambda b, j: (b, fwd_chunk(j), 0)),
                  pl.BlockSpec((1, CHUNK, LANES), lambda b, j: (b, bwd_chunk(j), 0))],
        out_specs=[pl.BlockSpec((1, 1, CHUNK, BRANCH_W), lambda b, j: (0, b, fwd_chunk(j), 0)),
                   pl.BlockSpec((1, 1, CHUNK, BRANCH_W), lambda b, j: (0, b, bwd_chunk(j), 0))],
        out_shape=[jax.ShapeDtypeStruct((1, B, T, BRANCH_W), F32),
                   jax.ShapeDtypeStruct((1, B, T, BRANCH_W), F32)],
        scratch_shapes=[pltpu.VMEM((2, M_HEADS, M_HD, 2 * M_HD), F32),
                        pltpu.VMEM((2, M_HEADS, 8, LANES), F32)],
        compiler_params=_params("parallel", "arbitrary"),
        name="mlstm_scan",
    )(b_i, b_f, z, z, gates, gates)
    return out[0][0], out[1][0]


def _merge_kernel(ya_ref, yw_ref, yg_ref, hf_ref, hb_ref, ga_ref, gw_ref, gg_ref, gm_ref, mo_ref, mg_ref,
                  h_ref, gate_ref, wbr_ref, wo_ref, gpost_ref, mnorm_ref, o_ref):
    hm = hf_ref[0] + hb_ref[0]
    parts = []
    for hd in range(M_HEADS):
        x = hm[:, hd * M_HD:(hd + 1) * M_HD]
        ms = jnp.mean(x * x, axis=-1, keepdims=True)
        parts.append(x * lax.rsqrt(ms + EPS) * mnorm_ref[:, hd * M_HD:(hd + 1) * M_HD])
    ym = _sigmoid(mo_ref[0].astype(F32)) * jnp.concatenate(parts, axis=1)
    ys = (ya_ref[0].astype(F32), yw_ref[0].astype(F32), yg_ref[0].astype(F32), ym)
    gates = (ga_ref, gw_ref, gg_ref, gm_ref)
    acc = None
    for k in range(N_BRANCH):
        yk = (ys[k] * _silu(gates[k][0].astype(F32))).astype(BF16)
        proj = jnp.dot(yk, wbr_ref[k], preferred_element_type=F32)
        g = _sigmoid(mg_ref[0, :, k * D_MODEL:(k + 1) * D_MODEL].astype(F32))
        acc = g * proj if acc is None else acc + g * proj
    y = jnp.dot(acc.astype(BF16), wo_ref[...], preferred_element_type=F32)
    ms = jnp.mean(y * y, axis=-1, keepdims=True)
    yn = y * lax.rsqrt(ms + EPS) * gpost_ref[...]
    o_ref[0] = h_ref[0] + gate_ref[0, 0] * yn


def _merge(p_main, ya, yw, yg, hf, hb, h, gate2, wbr_bf, wo_bf, g_post, m_norm, *, tile0, n_tiles, tm):
    B, T, _ = h.shape
    n_ctx_tiles = CTX_LEN // tm
    t_out = n_tiles * tm

    def tok(w, col):
        return pl.BlockSpec((1, tm, w), lambda b, t: (b, t + tile0, col))

    def branch():
        return pl.BlockSpec((1, tm, BRANCH_W), lambda b, t: (b, t, 0))

    def const(shape):
        nd = len(shape)
        return pl.BlockSpec(shape, lambda b, t: (0,) * nd)

    bw = BRANCH_W
    return pl.pallas_call(
        _merge_kernel,
        grid=(B, n_tiles),
        in_specs=[branch(), branch(), branch(), tok(bw, 0), tok(bw, 0),
                  tok(bw, COL_AGATE // bw), tok(bw, COL_WGATE // bw), tok(bw, COL_GGATE // bw),
                  tok(bw, COL_MGATE // bw), tok(bw, COL_MO // bw),
                  tok(N_BRANCH * D_MODEL, COL_MERGE // (N_BRANCH * D_MODEL)),
                  tok(D_MODEL, 0),
                  pl.BlockSpec((1, 1, 1, D_MODEL),
                               lambda b, t: (b, jnp.where(t + tile0 < n_ctx_tiles, 0, 1), 0, 0)),
                  const((N_BRANCH, bw, D_MODEL)), const((D_MODEL, D_MODEL)),
                  const((1, D_MODEL)), const((1, bw))],
        out_specs=pl.BlockSpec((1, tm, D_MODEL), lambda b, t: (b, t, 0)),
        out_shape=jax.ShapeDtypeStruct((B, t_out, D_MODEL), F32),
        compiler_params=_params("parallel", "parallel"),
        name="merge",
    )(ya, yw, yg, hf, hb, p_main, p_main, p_main, p_main, p_main, p_main, h, gate2,
      wbr_bf, wo_bf, g_post, m_norm)


def _rope_tables(n_lat):
    rows = n_lat // GRID_W
    row = jnp.repeat(jnp.arange(rows), GRID_W).astype(F32)
    col = jnp.tile(jnp.arange(GRID_W), rows).astype(F32)
    n_freq = HEAD_DIM // 4
    inv = ROPE_THETA ** (-jnp.arange(n_freq, dtype=F32) / n_freq)
    ang = jnp.concatenate([row[:, None] * inv, col[:, None] * inv], axis=-1)
    cos, sin = jnp.cos(ang), jnp.sin(ang)
    cos_h = jnp.concatenate([cos, cos], axis=-1)
    sin_h = jnp.concatenate([-sin, sin], axis=-1)
    cos_t = jnp.concatenate([jnp.ones((CTX_LEN, HEAD_DIM), F32), cos_h], axis=0)
    sin_t = jnp.concatenate([jnp.zeros((CTX_LEN, HEAD_DIM), F32), sin_h], axis=0)
    return jnp.tile(cos_t, (1, LANES // HEAD_DIM)), jnp.tile(sin_t, (1, LANES // HEAD_DIM))


def _reorder_w_in(w):
    sl = _ref_slices()
    main = jnp.concatenate([w[:, sl[n][0]:sl[n][0] + sl[n][1]] for n in _MAIN_ORDER], axis=1)
    gate_cols = jnp.concatenate([w[:, sl['m_i'][0]:sl['m_i'][0] + 8], w[:, sl['m_f'][0]:sl['m_f'][0] + 8]],
                                axis=1)
    gate_cols = jnp.pad(gate_cols, ((0, 0), (0, LANES - 16)))
    return main.astype(BF16), gate_cols.astype(BF16)


def kernel(x, c, ctx, c_ctx, w_mod, b_mod, g_pre, g_post, w_in, a_q_norm, a_k_norm, w_sink, sg_ln_g, sg_ln_b,
           sg_w, sg_b, m_conv, m_b_i, m_b_f, m_norm, w_branch, w_out):
    B, n_lat, D = x.shape
    depth = w_mod.shape[0]
    T = CTX_LEN + n_lat
    n_ctx_chunks = CTX_LEN // CHUNK
    n_lat_chunks = n_lat // CHUNK
    rope = _rope_tables(n_lat)
    gmat = (jnp.arange(LANES)[:, None] // HEAD_DIM == jnp.arange(LANES)[None, :] // HEAD_DIM).astype(BF16)

    h = jnp.concatenate([ctx, x], axis=1)
    mod_rows = 16
    cc = jnp.zeros((mod_rows, D), F32).at[:B].set(c).at[B].set(c_ctx)
    tm_tok = 256

    for l in range(depth):
        need_ctx = l < depth - 1
        mod = _modulation(cc, w_mod[l].astype(BF16), b_mod[l][None, :])
        shift, scale, gate = mod[:, :D], mod[:, D:2 * D], mod[:, 2 * D:]
        mult = g_pre[l][None, :] * (1.0 + scale)
        mod4 = jnp.stack([jnp.broadcast_to(mult[B], (B, D)), jnp.broadcast_to(shift[B], (B, D)),
                          mult[:B], shift[:B]], axis=1)
        gate2 = jnp.stack([jnp.broadcast_to(gate[B], (B, D)), gate[:B]], axis=1)[:, :, None, :]

        w_main, w_gates = _reorder_w_in(w_in[l])
        p_main = _in_projection(h, mod4, w_main, BF16, tm=T, tn=768)
        p_gates = _in_projection(h, mod4, w_gates, F32, tm=T, tn=LANES)

        qn = jnp.tile(a_q_norm[l], LANES // HEAD_DIM)[None, :]
        kn = jnp.tile(a_k_norm[l], LANES // HEAD_DIM)[None, :]
        ya = _global_attention(p_main, rope, qn, kn, gmat,
                               q_tile0=n_ctx_chunks, n_q_tiles=n_lat_chunks, nk=T)
        yw = _window_attention(p_main, rope, w_sink[l], q_tile0=n_ctx_chunks, n_q_tiles=n_lat_chunks,
                               nk=T, banded=True)
        if need_ctx:
            ya_c = _global_attention(p_main, rope, qn, kn, gmat, q_tile0=0, n_q_tiles=n_ctx_chunks, nk=CTX_LEN)
            yw_c = _window_attention(p_main, rope, w_sink[l], q_tile0=0, n_q_tiles=n_ctx_chunks,
                                     nk=CTX_LEN, banded=False)
            ya = jnp.concatenate([ya_c, ya], axis=1)
            yw = jnp.concatenate([yw_c, yw], axis=1)

        tile0 = 0 if need_ctx else CTX_LEN // tm_tok
        n_tiles = T // tm_tok - tile0
        yg = _spatial_gating(p_main, sg_ln_g[l][None, :], sg_ln_b[l][None, :], sg_w[l].astype(BF16),
                             sg_b[l].T, tile0=tile0, n_tiles=n_tiles, tm=tm_tok)

        z = _mlstm_conv(p_main, m_conv[l])
        hf, hb = _mlstm_scan(z, p_gates, m_b_i[l].reshape(-1), m_b_f[l].reshape(-1))

        out = _merge(p_main, ya, yw, yg, hf, hb, h, gate2, w_branch[l].astype(BF16), w_out[l].astype(BF16),
                     g_post[l][None, :], m_norm[l][None, :], tile0=tile0, n_tiles=n_tiles, tm=tm_tok)
        if need_ctx:
            h = out
        else:
            return out
    return h[:, CTX_LEN:]
```

```python
import functools

import jax
import jax.numpy as jnp
from jax import lax
from jax.experimental import pallas as pl
from jax.experimental.pallas import tpu as pltpu

F32 = jnp.float32
BF16 = jnp.bfloat16

D_MODEL = 1024
GRID_W = 64
CTX_LEN = 256
N_BRANCH = 4
BRANCH_W = 512
HEAD_DIM = 64
N_HEADS = 8
N_KV = 2
GROUP = N_HEADS // N_KV
WINDOW = 128
CHUNK = 128
B_GROUPS = 4
M_HEADS = 4
M_HD = 128
N_CHAINS = 2 * M_HEADS
ROPE_THETA = 10000.0
EPS = 1e-6
LOG2E = 1.4426950408889634

LANES = 128
VMEM_LIMIT = 48 * 1024 * 1024

COL_MERGE = 0
COL_AQ = 4096
COL_MQKV = 4608
COL_AK = 6144
COL_AV = 6272
COL_WK = 6400
COL_WV = 6528
COL_WQ = 6656
COL_GU = 7168
COL_GV = 7680
COL_AGATE = 8192
COL_WGATE = 8704
COL_GGATE = 9216
COL_MGATE = 9728
COL_MO = 10240
N_MAIN = 10752

_REF_LAYOUT = (
    ('a_q', 512), ('a_k', 128), ('a_v', 128), ('a_gate', 512),
    ('w_q', 512), ('w_k', 128), ('w_v', 128), ('w_gate', 512),
    ('g_u', 512), ('g_v', 512), ('g_gate', 512),
    ('m_qkv', 1536), ('m_i', 8), ('m_f', 8), ('m_o', 512), ('m_gate', 512),
    ('merge', 4096),
)
_MAIN_ORDER = ('merge', 'a_q', 'm_qkv', 'a_k', 'a_v', 'w_k', 'w_v', 'w_q', 'g_u', 'g_v',
               'a_gate', 'w_gate', 'g_gate', 'm_gate', 'm_o')


def _ref_slices():
    out, start = {}, 0
    for name, size in _REF_LAYOUT:
        out[name] = (start, size)
        start += size
    return out


def _params(*sem):
    return pltpu.CompilerParams(dimension_semantics=sem, vmem_limit_bytes=VMEM_LIMIT)


def _sigmoid(x):
    return 1.0 / (1.0 + jnp.exp(-x))


def _silu(x):
    return x * _sigmoid(x)


def _gelu_tanh(x):
    c = 0.7978845608028654
    return 0.5 * x * (1.0 + jnp.tanh(c * (x + 0.044715 * (x * x * x))))


def _log_sigmoid(x):
    return -(jnp.maximum(-x, 0.0) + jnp.log1p(jnp.exp(-jnp.abs(x))))


def _mod_kernel(c_ref, w_ref, b_ref, o_ref):
    a = _silu(c_ref[...]).astype(BF16)
    o_ref[...] = jnp.dot(a, w_ref[...], preferred_element_type=F32) + b_ref[...]


def _modulation(cc, w_mod_bf, b_mod):
    rows = cc.shape[0]
    n = w_mod_bf.shape[1]
    tn = 1024
    return pl.pallas_call(
        _mod_kernel,
        grid=(n // tn,),
        in_specs=[pl.BlockSpec((rows, D_MODEL), lambda j: (0, 0)),
                  pl.BlockSpec((D_MODEL, tn), lambda j: (0, j)),
                  pl.BlockSpec((1, tn), lambda j: (0, j))],
        out_specs=pl.BlockSpec((rows, tn), lambda j: (0, j)),
        out_shape=jax.ShapeDtypeStruct((rows, n), F32),
        compiler_params=_params("parallel"),
        name="modulation",
    )(cc, w_mod_bf, b_mod)


def _inproj_kernel(h_ref, mod_ref, w_ref, o_ref, xn_ref, *, tm):
    t = pl.program_id(1)
    j = pl.program_id(2)

    @pl.when(j == 0)
    def _():
        rb = 256

        def norm_rows(r, carry):
            off = pl.multiple_of(r * rb, rb)
            x = h_ref[0, pl.ds(off, rb), :]
            ms = jnp.mean(x * x, axis=-1, keepdims=True)
            xh = x * lax.rsqrt(ms + EPS)
            row = t * tm + off + lax.broadcasted_iota(jnp.int32, (rb, 1), 0)
            is_ctx = row < CTX_LEN
            mult = jnp.where(is_ctx, mod_ref[0, 0:1, :], mod_ref[0, 2:3, :])
            shift = jnp.where(is_ctx, mod_ref[0, 1:2, :], mod_ref[0, 3:4, :])
            xn_ref[pl.ds(off, rb), :] = (xh * mult + shift).astype(BF16)
            return carry

        lax.fori_loop(0, tm // rb, norm_rows, 0)

    o_ref[0] = jnp.dot(xn_ref[...], w_ref[...], preferred_element_type=F32).astype(o_ref.dtype)


def _in_projection(h, mod4, w_bf, out_dtype, tm, tn):
    B, T, _ = h.shape
    n = w_bf.shape[1]
    return pl.pallas_call(
        functools.partial(_inproj_kernel, tm=tm),
        grid=(B, T // tm, n // tn),
        in_specs=[pl.BlockSpec((1, tm, D_MODEL), lambda b, t, j: (b, t, 0)),
                  pl.BlockSpec((1, 4, D_MODEL), lambda b, t, j: (b, 0, 0)),
                  pl.BlockSpec((D_MODEL, tn), lambda b, t, j: (0, j))],
        out_specs=pl.BlockSpec((1, tm, tn), lambda b, t, j: (b, t, j)),
        out_shape=jax.ShapeDtypeStruct((B, T, n), out_dtype),
        scratch_shapes=[pltpu.VMEM((tm, D_MODEL), BF16)],
        compiler_params=_params("parallel", "parallel", "arbitrary"),
        name="in_projection",
    )(h, mod4, w_bf)


def _rope_slab(x, cos, sin):
    lane = lax.broadcasted_iota(jnp.int32, x.shape, 1)
    first_half = (lane & (HEAD_DIM - 1)) < (HEAD_DIM // 2)
    partner = jnp.where(first_half,
                        pltpu.roll(x, LANES - HEAD_DIM // 2, axis=1),
                        pltpu.roll(x, HEAD_DIM // 2, axis=1))
    return x * cos + partner * sin


def _head_norm_slab(x, gain, gmat):
    ss = jnp.dot((x * x).astype(BF16), gmat, preferred_element_type=F32)
    return x * lax.rsqrt(ss * (1.0 / HEAD_DIM) + EPS) * gain


def _prep_kv(k_ref, v_ref, cosk_ref, sink_ref, kn_ref, gmat_ref, kp_ref, vt_ref, nk, use_norm):
    k = k_ref[0, 0:nk, :].astype(F32)
    if use_norm:
        k = _head_norm_slab(k, kn_ref[...], gmat_ref[...])
    k = _rope_slab(k, cosk_ref[0:nk, :], sink_ref[0:nk, :]).astype(BF16)
    for g in range(N_KV):
        kp_ref[g, 0:nk, :] = k[:, g * HEAD_DIM:(g + 1) * HEAD_DIM]
    for blk in range(nk // CHUNK):
        v = v_ref[0, blk * CHUNK:(blk + 1) * CHUNK, :].astype(F32)
        vt_ref[blk] = v.T.astype(BF16)


def _prep_qt(q_ref, cosq_ref, sinq_ref, qn_ref, gmat_ref, use_norm):
    cos = cosq_ref[...]
    sin = sinq_ref[...]
    xts = []
    for s in range(N_HEADS // 2):
        x = q_ref[0, :, s * LANES:(s + 1) * LANES].astype(F32)
        if use_norm:
            x = _head_norm_slab(x, qn_ref[...], gmat_ref[...])
        x = _rope_slab(x, cos, sin) * (HEAD_DIM ** -0.5 * LOG2E)
        xts.append(x.T.astype(BF16))
    per = GROUP // 2
    out = []
    for g in range(N_KV):
        parts = []
        for s in range(g * per, (g + 1) * per):
            parts.append(xts[s][:HEAD_DIM, :])
            parts.append(xts[s][HEAD_DIM:, :])
        out.append(jnp.concatenate(parts, axis=1))
    return out


def _heads_from_t(ot, tq):
    slabs = []
    for j in range(GROUP // 2):
        two = jnp.concatenate([ot[:, (2 * j) * tq:(2 * j + 1) * tq],
                               ot[:, (2 * j + 1) * tq:(2 * j + 2) * tq]], axis=0)
        slabs.append(two.T)
    return jnp.concatenate(slabs, axis=1)


def _gattn_kernel(q_ref, k_ref, v_ref, cosq_ref, sinq_ref, cosk_ref, sink_ref, qn_ref, kn_ref, gmat_ref,
                  o_ref, kp_ref, vt_ref, s_ref, *, tq, nk, ck):
    i = pl.program_id(1)

    @pl.when(i == 0)
    def _():
        _prep_kv(k_ref, v_ref, cosk_ref, sink_ref, kn_ref, gmat_ref, kp_ref, vt_ref, nk, True)

    qts = _prep_qt(q_ref, cosq_ref, sinq_ref, qn_ref, gmat_ref, True)
    n_chunks = nk // ck
    blocks_per_chunk = ck // CHUNK
    outs = []
    for g in range(N_KV):
        qt = qts[g]
        m = None
        for c in range(n_chunks):
            st = jnp.dot(kp_ref[g, c * ck:(c + 1) * ck, :], qt, preferred_element_type=F32)
            s_ref[g, c] = st
            cm = jnp.max(st, axis=0, keepdims=True)
            m = cm if m is None else jnp.maximum(m, cm)
        l = None
        acc = None
        for c in range(n_chunks):
            p = jnp.exp2(s_ref[g, c] - m)
            cl = jnp.sum(p, axis=0, keepdims=True)
            l = cl if l is None else l + cl
            vt = jnp.concatenate(
                [vt_ref[c * blocks_per_chunk + j, g * HEAD_DIM:(g + 1) * HEAD_DIM, :]
                 for j in range(blocks_per_chunk)], axis=1)
            pv = jnp.dot(vt, p.astype(BF16), preferred_element_type=F32)
            acc = pv if acc is None else acc + pv
        outs.append(_heads_from_t(acc / l, tq))
    o_ref[0] = jnp.concatenate(outs, axis=1).astype(o_ref.dtype)


def _global_attention(p_main, rope, qn, kn, gmat, *, q_tile0, n_q_tiles, nk):
    B, T, _ = p_main.shape
    cos, sin = rope
    tq = CHUNK
    ck = 256
    kernel = functools.partial(_gattn_kernel, tq=tq, nk=nk, ck=ck)
    qcol = COL_AQ // BRANCH_W
    return pl.pallas_call(
        kernel,
        grid=(B, n_q_tiles),
        in_specs=[pl.BlockSpec((1, tq, BRANCH_W), lambda b, i: (b, i + q_tile0, qcol)),
                  pl.BlockSpec((1, T, LANES), lambda b, i: (b, 0, COL_AK // LANES)),
                  pl.BlockSpec((1, T, LANES), lambda b, i: (b, 0, COL_AV // LANES)),
                  pl.BlockSpec((tq, LANES), lambda b, i: (i + q_tile0, 0)),
                  pl.BlockSpec((tq, LANES), lambda b, i: (i + q_tile0, 0)),
                  pl.BlockSpec((T, LANES), lambda b, i: (0, 0)),
                  pl.BlockSpec((T, LANES), lambda b, i: (0, 0)),
                  pl.BlockSpec((1, LANES), lambda b, i: (0, 0)),
                  pl.BlockSpec((1, LANES), lambda b, i: (0, 0)),
                  pl.BlockSpec((LANES, LANES), lambda b, i: (0, 0))],
        out_specs=pl.BlockSpec((1, tq, BRANCH_W), lambda b, i: (b, i, 0)),
        out_shape=jax.ShapeDtypeStruct((B, n_q_tiles * tq, BRANCH_W), BF16),
        scratch_shapes=[pltpu.VMEM((N_KV, nk, HEAD_DIM), BF16),
                        pltpu.VMEM((nk // CHUNK, LANES, CHUNK), BF16),
                        pltpu.VMEM((N_KV, nk // ck, ck, GROUP * tq), F32)],
        compiler_params=_params("parallel", "arbitrary"),
        name="global_attention",
    )(p_main, p_main, p_main, cos, sin, cos, sin, qn, kn, gmat)


def _wattn_kernel(wsink_ref, q_ref, k_ref, v_ref, cosq_ref, sinq_ref, cosk_ref, sink_ref,
                  o_ref, kp_ref, vt_ref, *, tq, nk, banded, n_lat_blocks):
    i = pl.program_id(1)

    @pl.when(i == 0)
    def _():
        _prep_kv(k_ref, v_ref, cosk_ref, sink_ref, None, None, kp_ref, vt_ref, nk, False)

    qts = _prep_qt(q_ref, cosq_ref, sinq_ref, None, None, False)
    cols = GROUP * tq
    n_ctx_chunks = CTX_LEN // CHUNK
    lane = lax.broadcasted_iota(jnp.int32, (1, cols), 1)

    if banded:
        prev_blk = i + n_ctx_chunks - 1
        own_blk = i + n_ctx_chunks
        next_blk = jnp.minimum(i + n_ctx_chunks + 1, n_ctx_chunks + n_lat_blocks - 1)
        kk = lax.broadcasted_iota(jnp.int32, (CHUNK, cols), 0)
        qq = lax.broadcasted_iota(jnp.int32, (CHUNK, cols), 1) & (tq - 1)
        ok_prev = jnp.logical_and(kk >= qq, i > 0)
        ok_next = jnp.logical_and(kk <= qq, i < n_lat_blocks - 1)

    outs = []
    for g in range(N_KV):
        qt = qts[g]
        sink_row = jnp.zeros((1, cols), F32)
        for h in range(GROUP):
            in_head = jnp.logical_and(lane >= h * tq, lane < (h + 1) * tq)
            sink_row = jnp.where(in_head, wsink_ref[g * GROUP + h] * LOG2E, sink_row)

        def scores(blk):
            off = blk * CHUNK
            if not isinstance(off, int):
                off = pl.multiple_of(off, CHUNK)
            return jnp.dot(kp_ref[g, pl.ds(off, CHUNK), :], qt, preferred_element_type=F32)

        blocks = [(scores(cb), cb) for cb in range(n_ctx_chunks)]
        if banded:
            blocks.append((jnp.where(ok_prev, scores(prev_blk), -jnp.inf), prev_blk))
            blocks.append((scores(own_blk), own_blk))
            blocks.append((jnp.where(ok_next, scores(next_blk), -jnp.inf), next_blk))
        m = sink_row
        for st, _ in blocks:
            m = jnp.maximum(m, jnp.max(st, axis=0, keepdims=True))
        l = jnp.exp2(sink_row - m)
        acc = None
        for st, blk in blocks:
            p = jnp.exp2(st - m)
            l = l + jnp.sum(p, axis=0, keepdims=True)
            pv = jnp.dot(vt_ref[blk, g * HEAD_DIM:(g + 1) * HEAD_DIM, :], p.astype(BF16),
                         preferred_element_type=F32)
            acc = pv if acc is None else acc + pv
        outs.append(_heads_from_t(acc / l, tq))
    o_ref[0] = jnp.concatenate(outs, axis=1).astype(o_ref.dtype)


def _window_attention(p_main, rope, w_sink, *, q_tile0, n_q_tiles, nk, banded):
    B, T, _ = p_main.shape
    cos, sin = rope
    tq = CHUNK
    n_lat_blocks = (T - CTX_LEN) // CHUNK
    kernel = functools.partial(_wattn_kernel, tq=tq, nk=nk, banded=banded, n_lat_blocks=n_lat_blocks)
    qcol = COL_WQ // BRANCH_W
    return pl.pallas_call(
        kernel,
        grid=(B, n_q_tiles),
        in_specs=[pl.BlockSpec(memory_space=pltpu.SMEM),
                  pl.BlockSpec((1, tq, BRANCH_W), lambda b, i: (b, i + q_tile0, qcol)),
                  pl.BlockSpec((1, T, LANES), lambda b, i: (b, 0, COL_WK // LANES)),
                  pl.BlockSpec((1, T, LANES), lambda b, i: (b, 0, COL_WV // LANES)),
                  pl.BlockSpec((tq, LANES), lambda b, i: (i + q_tile0, 0)),
                  pl.BlockSpec((tq, LANES), lambda b, i: (i + q_tile0, 0)),
                  pl.BlockSpec((T, LANES), lambda b, i: (0, 0)),
                  pl.BlockSpec((T, LANES), lambda b, i: (0, 0))],
        out_specs=pl.BlockSpec((1, tq, BRANCH_W), lambda b, i: (b, i, 0)),
        out_shape=jax.ShapeDtypeStruct((B, n_q_tiles * tq, BRANCH_W), BF16),
        scratch_shapes=[pltpu.VMEM((N_KV, nk, HEAD_DIM), BF16),
                        pltpu.VMEM((nk // CHUNK, LANES, CHUNK), BF16)],
        compiler_params=_params("parallel", "arbitrary"),
        name="window_attention",
    )(w_sink, p_main, p_main, p_main, cos, sin, cos, sin)


def _sgate_kernel(u_ref, v_ref, lng_ref, lnb_ref, ws_ref, bs_ref, o_ref, *, tm):
    u = _gelu_tanh(u_ref[0].astype(F32))
    v = _gelu_tanh(v_ref[0].astype(F32))
    mu = jnp.mean(v, axis=-1, keepdims=True)
    vc = v - mu
    var = jnp.mean(vc * vc, axis=-1, keepdims=True)
    vn = (vc * lax.rsqrt(var + EPS) * lng_ref[...] + lnb_ref[...]).astype(BF16)
    gw = BRANCH_W // B_GROUPS
    chunks = []
    for c in range(tm // CHUNK):
        cols = []
        for g in range(B_GROUPS):
            mixed = jnp.dot(ws_ref[g], vn[c * CHUNK:(c + 1) * CHUNK, g * gw:(g + 1) * gw],
                            preferred_element_type=F32)
            cols.append(mixed + bs_ref[:, g:g + 1])
        chunks.append(jnp.concatenate(cols, axis=1))
    o_ref[0] = (u * jnp.concatenate(chunks, axis=0)).astype(o_ref.dtype)


def _spatial_gating(p_main, ln_g, ln_b, ws_bf, bs_t, *, tile0, n_tiles, tm):
    B, T, _ = p_main.shape
    return pl.pallas_call(
        functools.partial(_sgate_kernel, tm=tm),
        grid=(B, n_tiles),
        in_specs=[pl.BlockSpec((1, tm, BRANCH_W), lambda b, t: (b, t + tile0, COL_GU // BRANCH_W)),
                  pl.BlockSpec((1, tm, BRANCH_W), lambda b, t: (b, t + tile0, COL_GV // BRANCH_W)),
                  pl.BlockSpec((1, BRANCH_W), lambda b, t: (0, 0)),
                  pl.BlockSpec((1, BRANCH_W), lambda b, t: (0, 0)),
                  pl.BlockSpec((B_GROUPS, CHUNK, CHUNK), lambda b, t: (0, 0, 0)),
                  pl.BlockSpec((CHUNK, B_GROUPS), lambda b, t: (0, 0))],
        out_specs=pl.BlockSpec((1, tm, BRANCH_W), lambda b, t: (b, t, 0)),
        out_shape=jax.ShapeDtypeStruct((B, n_tiles * tm, BRANCH_W), BF16),
        compiler_params=_params("parallel", "parallel"),
        name="spatial_gating",
    )(p_main, p_main, ln_g, ln_b, ws_bf, bs_t)


def _conv_kernel(x_ref, w_ref, o_ref, *, T, mode):
    x = x_ref[0].astype(F32)
    row = lax.broadcasted_iota(jnp.int32, (T, 1), 0)
    prev = pltpu.roll(x, 1, axis=0)
    nxt = pltpu.roll(x, T - 1, axis=0)
    prev = jnp.where(jnp.logical_or(row == 0, row == CTX_LEN), 0.0, prev)
    nxt = jnp.where(jnp.logical_or(row == CTX_LEN - 1, row == T - 1), 0.0, nxt)
    y = prev * w_ref[0:1, :] + x * w_ref[1:2, :] + nxt * w_ref[2:3, :]
    z = _silu(y)
    if mode == 'q':
        z = z * (M_HD ** -0.5)
    if mode == 'k':
        o_ref[0] = z.T.astype(o_ref.dtype)
    else:
        o_ref[0] = z.astype(o_ref.dtype)


def _mlstm_conv(p_main, conv_w, mode):
    B, T, _ = p_main.shape
    n_slabs = BRANCH_W // LANES
    part = {'q': 0, 'k': 1, 'v': 2}[mode]
    col0 = COL_MQKV // LANES + part * n_slabs
    if mode == 'k':
        out_spec = pl.BlockSpec((1, LANES, T), lambda b, j: (b, j, 0))
        out_shape = jax.ShapeDtypeStruct((B, BRANCH_W, T), BF16)
    else:
        out_spec = pl.BlockSpec((1, T, LANES), lambda b, j: (b, 0, j))
        out_shape = jax.ShapeDtypeStruct((B, T, BRANCH_W), BF16)
    return pl.pallas_call(
        functools.partial(_conv_kernel, T=T, mode=mode),
        grid=(B, n_slabs),
        in_specs=[pl.BlockSpec((1, T, LANES), lambda b, j: (b, 0, col0 + j)),
                  pl.BlockSpec((3, LANES), lambda b, j: (0, part * n_slabs + j))],
        out_specs=out_spec,
        out_shape=out_shape,
        compiler_params=_params("parallel", "parallel"),
        name="mlstm_conv_" + mode,
    )(p_main, conv_w)


def _split3(x):
    x1 = x.astype(BF16)
    r = x - x1.astype(F32)
    x2 = r.astype(BF16)
    x3 = (r - x2.astype(F32)).astype(BF16)
    return x1, x2, x3


def _mgates_kernel(g_ref, bias_ref, ct_ref, ur_ref, ar_ref, *, T):
    L = CHUNK
    n_chunks = T // L
    n_ctx = CTX_LEN // L
    gall = g_ref[0] + bias_ref[...]
    i_all = gall[0:N_CHAINS]
    f_all = _log_sigmoid(gall[N_CHAINS:2 * N_CHAINS])
    fwd_rows = lax.broadcasted_iota(jnp.int32, (N_CHAINS, L), 0) < M_HEADS
    fwd_col = lax.broadcasted_iota(jnp.int32, (N_CHAINS, 1), 0) < M_HEADS
    lane = lax.broadcasted_iota(jnp.int32, (N_CHAINS, L), 1)
    s_idx = lax.broadcasted_iota(jnp.int32, (L, L), 0)
    t_idx = lax.broadcasted_iota(jnp.int32, (L, L), 1)
    sum_upto = jnp.where(s_idx <= t_idx, 1.0, 0.0).astype(BF16)
    sum_from = jnp.where(s_idx >= t_idx, 1.0, 0.0).astype(BF16)

    def cumsum_dir(f):
        bf = None
        bb = None
        for part in _split3(f):
            a = jnp.dot(part, sum_upto, preferred_element_type=F32)
            b = jnp.dot(part, sum_from, preferred_element_type=F32)
            bf = a if bf is None else bf + a
            bb = b if bb is None else bb + b
        return jnp.where(fwd_rows, bf, bb)

    def cummax_dir(u):
        xf = u
        xb = u
        sh = 1
        while sh < L:
            xf = jnp.maximum(xf, jnp.where(lane >= sh, pltpu.roll(xf, sh, axis=1), -jnp.inf))
            xb = jnp.maximum(xb, jnp.where(lane < L - sh, pltpu.roll(xb, L - sh, axis=1), -jnp.inf))
            sh *= 2
        return jnp.where(fwd_rows, xf, xb)

    b_c, u_c, mloc_c, bl_c, g_c, gmax_c = [], [], [], [], [], []
    for c in range(n_chunks):
        f = f_all[:, c * L:(c + 1) * L]
        i = i_all[:, c * L:(c + 1) * L]
        b = cumsum_dir(f)
        u = i - b
        bl = jnp.sum(f, axis=1, keepdims=True)
        g = bl + u
        b_c.append(b)
        u_c.append(u)
        mloc_c.append(b + cummax_dir(u))
        bl_c.append(bl)
        g_c.append(g)
        gmax_c.append(jnp.max(g, axis=1, keepdims=True))

    def scan(order):
        m = jnp.zeros((N_CHAINS, 1), F32)
        prev, new = {}, {}
        for c in order:
            prev[c] = m
            m = jnp.maximum(bl_c[c] + m, gmax_c[c])
            new[c] = m
        return prev, new

    order_f = list(range(n_chunks))
    order_b = [n_ctx - 1 - j for j in range(n_ctx)] + [n_chunks + n_ctx - 1 - j for j in range(n_ctx, n_chunks)]
    prev_f, new_f = scan(order_f)
    prev_b, new_b = scan(order_b)

    for c in range(n_chunks):
        m_prev = jnp.where(fwd_col, prev_f[c], prev_b[c])
        m_new = jnp.where(fwd_col, new_f[c], new_b[c])
        b = b_c[c]
        m_inter = b + m_prev
        m_t = jnp.maximum(m_inter, mloc_c[c])
        pack = jnp.concatenate([b - m_t, jnp.exp(m_inter - m_t), jnp.exp(-m_t), jnp.exp(g_c[c] - m_new),
                                jnp.zeros((LANES - 4 * N_CHAINS, L), F32)], axis=0)
        ct_ref[0, c * L:(c + 1) * L, :] = pack.T
        ur_ref[0, :, c * L:(c + 1) * L] = u_c[c]
        ar_ref[0, c] = jnp.broadcast_to(jnp.exp(bl_c[c] + m_prev - m_new), (N_CHAINS, LANES))


def _mlstm_gates(gates_row, bias_row):
    B, _, T = gates_row.shape
    n_chunks = T // CHUNK
    return pl.pallas_call(
        functools.partial(_mgates_kernel, T=T),
        grid=(B,),
        in_specs=[pl.BlockSpec((1, 2 * N_CHAINS, T), lambda b: (b, 0, 0)),
                  pl.BlockSpec((2 * N_CHAINS, T), lambda b: (0, 0))],
        out_specs=[pl.BlockSpec((1, T, LANES), lambda b: (b, 0, 0)),
                   pl.BlockSpec((1, N_CHAINS, T), lambda b: (b, 0, 0)),
                   pl.BlockSpec((1, n_chunks, N_CHAINS, LANES), lambda b: (b, 0, 0, 0))],
        out_shape=[jax.ShapeDtypeStruct((B, T, LANES), F32),
                   jax.ShapeDtypeStruct((B, N_CHAINS, T), F32),
                   jax.ShapeDtypeStruct((B, n_chunks, N_CHAINS, LANES), F32)],
        compiler_params=_params("parallel"),
        name="mlstm_gates",
    )(gates_row, bias_row)


def _mlstm_kernel(qf_ref, ktf_ref, vf_ref, ctf_ref, urf_ref, arf_ref,
                  qb_ref, ktb_ref, vb_ref, ctb_ref, urb_ref, arb_ref, of_ref, ob_ref, st_ref):
    L = CHUNK

    @pl.when(pl.program_id(1) == 0)
    def _():
        st_ref[...] = jnp.zeros_like(st_ref)

    t_idx = lax.broadcasted_iota(jnp.int32, (L, L), 0)
    s_idx = lax.broadcasted_iota(jnp.int32, (L, L), 1)
    ones = jnp.ones((L, M_HD), BF16)
    dirs = ((qf_ref, ktf_ref, vf_ref, ctf_ref, urf_ref, arf_ref, of_ref),
            (qb_ref, ktb_ref, vb_ref, ctb_ref, urb_ref, arb_ref, ob_ref))
    for d, (q_ref, kt_ref, v_ref, ct_ref, ur_ref, ar_ref, o_ref) in enumerate(dirs):
        mask = (s_idx <= t_idx) if d == 0 else (s_idx >= t_idx)
        ct = ct_ref[0]
        ur = ur_ref[0]
        ar = ar_ref[0, 0]
        outs = []
        for h in range(M_HEADS):
            c = d * M_HEADS + h
            q = q_ref[0, :, h * M_HD:(h + 1) * M_HD]
            kt = kt_ref[0, h * M_HD:(h + 1) * M_HD, :]
            v = v_ref[0, :, h * M_HD:(h + 1) * M_HD]
            v_ext = jnp.concatenate([v, ones], axis=1)
            cq = ct[:, c:c + 1]
            w_inter = ct[:, N_CHAINS + c:N_CHAINS + c + 1]
            e_negm = ct[:, 2 * N_CHAINS + c:2 * N_CHAINS + c + 1]
            wk = ct[:, 3 * N_CHAINS + c:3 * N_CHAINS + c + 1]
            w_intra = jnp.exp(jnp.where(mask, cq + ur[c:c + 1, :], -jnp.inf))
            s = jnp.dot(q, kt, preferred_element_type=F32) * w_intra
            st_prev = st_ref[d, h]
            tot = (jnp.dot(s.astype(BF16), v_ext, preferred_element_type=F32)
                   + w_inter * jnp.dot(q, st_prev.astype(BF16), preferred_element_type=F32))
            den = jnp.maximum(jnp.abs(tot[:, M_HD:]), e_negm)
            outs.append(tot[:, :M_HD] / den)
            x = (v_ext.astype(F32) * wk).astype(BF16)
            a = jnp.concatenate([ar[c:c + 1, :], ar[c:c + 1, :]], axis=1)
            st_ref[d, h] = a * st_prev + jnp.dot(kt, x, preferred_element_type=F32)
        o_ref[0, 0] = jnp.concatenate(outs, axis=1).astype(o_ref.dtype)


def _mlstm_scan(zq, zkt, zv, ct, ur, ar):
    B, T, _ = zq.shape
    n_chunks = T // CHUNK
    n_ctx = CTX_LEN // CHUNK

    def fwd(j):
        return j

    def bwd(j):
        return jnp.where(j < n_ctx, n_ctx - 1 - j, n_chunks + n_ctx - 1 - j)

    def specs(order):
        return [pl.BlockSpec((1, CHUNK, BRANCH_W), lambda b, j: (b, order(j), 0)),
                pl.BlockSpec((1, BRANCH_W, CHUNK), lambda b, j: (b, 0, order(j))),
                pl.BlockSpec((1, CHUNK, BRANCH_W), lambda b, j: (b, order(j), 0)),
                pl.BlockSpec((1, CHUNK, LANES), lambda b, j: (b, order(j), 0)),
                pl.BlockSpec((1, N_CHAINS, CHUNK), lambda b, j: (b, 0, order(j))),
                pl.BlockSpec((1, 1, N_CHAINS, LANES), lambda b, j: (b, order(j), 0, 0))]

    out = pl.pallas_call(
        _mlstm_kernel,
        grid=(B, n_chunks),
        in_specs=specs(fwd) + specs(bwd),
        out_specs=[pl.BlockSpec((1, 1, CHUNK, BRANCH_W), lambda b, j: (0, b, fwd(j), 0)),
                   pl.BlockSpec((1, 1, CHUNK, BRANCH_W), lambda b, j: (0, b, bwd(j), 0))],
        out_shape=[jax.ShapeDtypeStruct((1, B, T, BRANCH_W), F32),
                   jax.ShapeDtypeStruct((1, B, T, BRANCH_W), F32)],
        scratch_shapes=[pltpu.VMEM((2, M_HEADS, M_HD, 2 * M_HD), F32)],
        compiler_params=_params("parallel", "arbitrary"),
        name="mlstm_scan",
    )(zq, zkt, zv, ct, ur, ar, zq, zkt, zv, ct, ur, ar)
    return out[0][0], out[1][0]


def _merge_kernel(ya_ref, yw_ref, yg_ref, hf_ref, hb_ref, ga_ref, gw_ref, gg_ref, gm_ref, mo_ref, mg_ref,
                  h_ref, gate_ref, wbr_ref, wo_ref, gpost_ref, mnorm_ref, o_ref):
    hm = hf_ref[0] + hb_ref[0]
    parts = []
    for hd in range(M_HEADS):
        x = hm[:, hd * M_HD:(hd + 1) * M_HD]
        ms = jnp.mean(x * x, axis=-1, keepdims=True)
        parts.append(x * lax.rsqrt(ms + EPS) * mnorm_ref[:, hd * M_HD:(hd + 1) * M_HD])
    ym = _sigmoid(mo_ref[0].astype(F32)) * jnp.concatenate(parts, axis=1)
    ys = (ya_ref[0].astype(F32), yw_ref[0].astype(F32), yg_ref[0].astype(F32), ym)
    gates = (ga_ref, gw_ref, gg_ref, gm_ref)
    acc = None
    for k in range(N_BRANCH):
        yk = (ys[k] * _silu(gates[k][0].astype(F32))).astype(BF16)
        proj = jnp.dot(yk, wbr_ref[k], preferred_element_type=F32)
        g = _sigmoid(mg_ref[0, :, k * D_MODEL:(k + 1) * D_MODEL].astype(F32))
        acc = g * proj if acc is None else acc + g * proj
    y = jnp.dot(acc.astype(BF16), wo_ref[...], preferred_element_type=F32)
    ms = jnp.mean(y * y, axis=-1, keepdims=True)
    yn = y * lax.rsqrt(ms + EPS) * gpost_ref[...]
    o_ref[0] = h_ref[0] + gate_ref[0, 0] * yn


def _merge(p_main, ya, yw, yg, hf, hb, h, gate2, wbr_bf, wo_bf, g_post, m_norm, *, tile0, n_tiles, tm):
    B, T, _ = h.shape
    n_ctx_tiles = CTX_LEN // tm
    t_out = n_tiles * tm

    def tok(w, col):
        return pl.BlockSpec((1, tm, w), lambda b, t: (b, t + tile0, col))

    def branch():
        return pl.BlockSpec((1, tm, BRANCH_W), lambda b, t: (b, t, 0))

    def const(shape):
        nd = len(shape)
        return pl.BlockSpec(shape, lambda b, t: (0,) * nd)

    bw = BRANCH_W
    return pl.pallas_call(
        _merge_kernel,
        grid=(B, n_tiles),
        in_specs=[branch(), branch(), branch(), tok(bw, 0), tok(bw, 0),
                  tok(bw, COL_AGATE // bw), tok(bw, COL_WGATE // bw), tok(bw, COL_GGATE // bw),
                  tok(bw, COL_MGATE // bw), tok(bw, COL_MO // bw),
                  tok(N_BRANCH * D_MODEL, COL_MERGE // (N_BRANCH * D_MODEL)),
                  tok(D_MODEL, 0),
                  pl.BlockSpec((1, 1, 1, D_MODEL),
                               lambda b, t: (b, jnp.where(t + tile0 < n_ctx_tiles, 0, 1), 0, 0)),
                  const((N_BRANCH, bw, D_MODEL)), const((D_MODEL, D_MODEL)),
                  const((1, D_MODEL)), const((1, bw))],
        out_specs=pl.BlockSpec((1, tm, D_MODEL), lambda b, t: (b, t, 0)),
        out_shape=jax.ShapeDtypeStruct((B, t_out, D_MODEL), F32),
        compiler_params=_params("parallel", "parallel"),
        name="merge",
    )(ya, yw, yg, hf, hb, p_main, p_main, p_main, p_main, p_main, p_main, h, gate2,
      wbr_bf, wo_bf, g_post, m_norm)


def _rope_tables(n_lat):
    rows = n_lat // GRID_W
    row = jnp.repeat(jnp.arange(rows), GRID_W).astype(F32)
    col = jnp.tile(jnp.arange(GRID_W), rows).astype(F32)
    n_freq = HEAD_DIM // 4
    inv = ROPE_THETA ** (-jnp.arange(n_freq, dtype=F32) / n_freq)
    ang = jnp.concatenate([row[:, None] * inv, col[:, None] * inv], axis=-1)
    cos, sin = jnp.cos(ang), jnp.sin(ang)
    cos_h = jnp.concatenate([cos, cos], axis=-1)
    sin_h = jnp.concatenate([-sin, sin], axis=-1)
    cos_t = jnp.concatenate([jnp.ones((CTX_LEN, HEAD_DIM), F32), cos_h], axis=0)
    sin_t = jnp.concatenate([jnp.zeros((CTX_LEN, HEAD_DIM), F32), sin_h], axis=0)
    return jnp.tile(cos_t, (1, LANES // HEAD_DIM)), jnp.tile(sin_t, (1, LANES // HEAD_DIM))


def _reorder_w_in(w):
    sl = _ref_slices()
    main = jnp.concatenate([w[:, sl[n][0]:sl[n][0] + sl[n][1]] for n in _MAIN_ORDER], axis=1)
    gate_cols = jnp.concatenate([w[:, sl['m_i'][0]:sl['m_i'][0] + 8], w[:, sl['m_f'][0]:sl['m_f'][0] + 8]],
                                axis=1)
    gate_cols = jnp.pad(gate_cols, ((0, 0), (0, LANES - 16)))
    return main.astype(BF16), gate_cols.astype(BF16)


def kernel(x, c, ctx, c_ctx, w_mod, b_mod, g_pre, g_post, w_in, a_q_norm, a_k_norm, w_sink, sg_ln_g, sg_ln_b,
           sg_w, sg_b, m_conv, m_b_i, m_b_f, m_norm, w_branch, w_out):
    B, n_lat, D = x.shape
    depth = w_mod.shape[0]
    T = CTX_LEN + n_lat
    n_ctx_chunks = CTX_LEN // CHUNK
    n_lat_chunks = n_lat // CHUNK
    rope = _rope_tables(n_lat)
    gmat = (jnp.arange(LANES)[:, None] // HEAD_DIM == jnp.arange(LANES)[None, :] // HEAD_DIM).astype(BF16)

    h = jnp.concatenate([ctx, x], axis=1)
    mod_rows = 16
    cc = jnp.zeros((mod_rows, D), F32).at[:B].set(c).at[B].set(c_ctx)
    tm_tok = 256

    for l in range(depth):
        need_ctx = l < depth - 1
        mod = _modulation(cc, w_mod[l].astype(BF16), b_mod[l][None, :])
        shift, scale, gate = mod[:, :D], mod[:, D:2 * D], mod[:, 2 * D:]
        mult = g_pre[l][None, :] * (1.0 + scale)
        mod4 = jnp.stack([jnp.broadcast_to(mult[B], (B, D)), jnp.broadcast_to(shift[B], (B, D)),
                          mult[:B], shift[:B]], axis=1)
        gate2 = jnp.stack([jnp.broadcast_to(gate[B], (B, D)), gate[:B]], axis=1)[:, :, None, :]

        w_main, w_gates = _reorder_w_in(w_in[l])
        p_main = _in_projection(h, mod4, w_main, BF16, tm=T, tn=768)
        p_gates = _in_projection(h, mod4, w_gates, F32, tm=T, tn=LANES)

        qn = jnp.tile(a_q_norm[l], LANES // HEAD_DIM)[None, :]
        kn = jnp.tile(a_k_norm[l], LANES // HEAD_DIM)[None, :]
        ya = _global_attention(p_main, rope, qn, kn, gmat,
                               q_tile0=n_ctx_chunks, n_q_tiles=n_lat_chunks, nk=T)
        yw = _window_attention(p_main, rope, w_sink[l], q_tile0=n_ctx_chunks, n_q_tiles=n_lat_chunks,
                               nk=T, banded=True)
        if need_ctx:
            ya_c = _global_attention(p_main, rope, qn, kn, gmat, q_tile0=0, n_q_tiles=n_ctx_chunks, nk=CTX_LEN)
            yw_c = _window_attention(p_main, rope, w_sink[l], q_tile0=0, n_q_tiles=n_ctx_chunks,
                                     nk=CTX_LEN, banded=False)
            ya = jnp.concatenate([ya_c, ya], axis=1)
            yw = jnp.concatenate([yw_c, yw], axis=1)

        tile0 = 0 if need_ctx else CTX_LEN // tm_tok
        n_tiles = T // tm_tok - tile0
        yg = _spatial_gating(p_main, sg_ln_g[l][None, :], sg_ln_b[l][None, :], sg_w[l].astype(BF16),
                             sg_b[l].T, tile0=tile0, n_tiles=n_tiles, tm=tm_tok)

        zq = _mlstm_conv(p_main, m_conv[l], 'q')
        zkt = _mlstm_conv(p_main, m_conv[l], 'k')
        zv = _mlstm_conv(p_main, m_conv[l], 'v')
        gates_row = jnp.swapaxes(p_gates[:, :, :2 * N_CHAINS], 1, 2)
        bias_row = jnp.broadcast_to(
            jnp.concatenate([m_b_i[l].reshape(-1), m_b_f[l].reshape(-1)])[:, None], (2 * N_CHAINS, T))
        ct, ur, ar = _mlstm_gates(gates_row, bias_row)
        hf, hb = _mlstm_scan(zq, zkt, zv, ct, ur, ar)

        out = _merge(p_main, ya, yw, yg, hf, hb, h, gate2, w_branch[l].astype(BF16), w_out[l].astype(BF16),
                     g_post[l][None, :], m_norm[l][None, :], tile0=tile0, n_tiles=n_tiles, tm=tm_tok)
        if need_ctx:
            h = out
        else:
            return out
    return h[:, CTX_LEN:]
```

```python
import functools

import jax
import jax.numpy as jnp
from jax import lax
from jax.experimental import pallas as pl
from jax.experimental.pallas import tpu as pltpu

F32 = jnp.float32
BF16 = jnp.bfloat16

D_MODEL = 1024
GRID_W = 64
CTX_LEN = 256
N_BRANCH = 4
BRANCH_W = 512
HEAD_DIM = 64
N_HEADS = 8
N_KV = 2
GROUP = N_HEADS // N_KV
WINDOW = 128
CHUNK = 128
B_GROUPS = 4
M_HEADS = 4
M_HD = 128
N_CHAINS = 2 * M_HEADS
ROPE_THETA = 10000.0
EPS = 1e-6
LOG2E = 1.4426950408889634

LANES = 128
VMEM_LIMIT = 48 * 1024 * 1024

COL_AQ = 0
COL_WQ = 512
COL_AK = 1024
COL_AV = 1152
COL_WK = 1280
COL_WV = 1408
COL_GU = 1536
COL_GV = 2048
COL_MQKV = 2560
N_MAIN = 4096
GCOL_BRANCH_GATE = (0, 512, 1024, 2048)
GCOL_MO = 1536
GCOL_MERGE = 2560
N_GATE = GCOL_MERGE + N_BRANCH * D_MODEL

_REF_LAYOUT = (
    ('a_q', 512), ('a_k', 128), ('a_v', 128), ('a_gate', 512),
    ('w_q', 512), ('w_k', 128), ('w_v', 128), ('w_gate', 512),
    ('g_u', 512), ('g_v', 512), ('g_gate', 512),
    ('m_qkv', 1536), ('m_i', 8), ('m_f', 8), ('m_o', 512), ('m_gate', 512),
    ('merge', 4096),
)
_MAIN_ORDER = ('a_q', 'w_q', 'a_k', 'a_v', 'w_k', 'w_v', 'g_u', 'g_v', 'm_qkv')
_GATE_ORDER = ('a_gate', 'w_gate', 'g_gate', 'm_o', 'm_gate', 'merge')


def _ref_slices():
    out, start = {}, 0
    for name, size in _REF_LAYOUT:
        out[name] = (start, size)
        start += size
    return out


def _params(*sem):
    return pltpu.CompilerParams(dimension_semantics=sem, vmem_limit_bytes=VMEM_LIMIT)


def _sigmoid(x):
    return 1.0 / (1.0 + jnp.exp(-x))


def _silu(x):
    return x * _sigmoid(x)


def _gelu_tanh(x):
    c = 0.7978845608028654
    return 0.5 * x * (1.0 + jnp.tanh(c * (x + 0.044715 * (x * x * x))))


def _log_sigmoid(x):
    return -(jnp.maximum(-x, 0.0) + jnp.log1p(jnp.exp(-jnp.abs(x))))


def _mod_kernel(c_ref, w_ref, b_ref, o_ref):
    a = _silu(c_ref[...]).astype(BF16)
    o_ref[...] = jnp.dot(a, w_ref[...], preferred_element_type=F32) + b_ref[...]


def _modulation(cc, w_mod_bf, b_mod):
    rows = cc.shape[0]
    n = w_mod_bf.shape[1]
    tn = 1024
    return pl.pallas_call(
        _mod_kernel,
        grid=(n // tn,),
        in_specs=[pl.BlockSpec((rows, D_MODEL), lambda j: (0, 0)),
                  pl.BlockSpec((D_MODEL, tn), lambda j: (0, j)),
                  pl.BlockSpec((1, tn), lambda j: (0, j))],
        out_specs=pl.BlockSpec((rows, tn), lambda j: (0, j)),
        out_shape=jax.ShapeDtypeStruct((rows, n), F32),
        compiler_params=_params("parallel"),
        name="modulation",
    )(cc, w_mod_bf, b_mod)


MOD_ROWS = 8


def _norm_mod(x, mult, shift):
    ms = jnp.mean(x * x, axis=-1, keepdims=True)
    return (x * lax.rsqrt(ms + EPS) * mult + shift).astype(BF16)


def _inproj_kernel(*refs, split, n_lat):
    if split:
        hc_ref, hx_ref, mod_ref, w_ref, wg_ref, o_ref, og_ref, xn_ref = refs
    else:
        h_ref, mod_ref, w_ref, wg_ref, o_ref, og_ref, xn_ref = refs
    rb = CTX_LEN

    @pl.when(pl.program_id(1) == 0)
    def _():
        if split:
            ctx_rows, src, src_off = hc_ref[0], hx_ref, 0
        else:
            ctx_rows, src, src_off = h_ref[0, 0:rb, :], h_ref, rb
        xn_ref[0:rb, :] = _norm_mod(ctx_rows, mod_ref[0, 0:1, :], mod_ref[0, 1:2, :])

        def norm_rows(r, carry):
            off = pl.multiple_of(r * rb, rb)
            x = src[0, pl.ds(src_off + off, rb), :]
            xn_ref[pl.ds(rb + off, rb), :] = _norm_mod(x, mod_ref[0, 3:4, :], mod_ref[0, 4:5, :])
            return carry

        lax.fori_loop(0, n_lat // rb, norm_rows, 0)
        og_ref[0] = jnp.dot(xn_ref[...], wg_ref[...], preferred_element_type=F32)

    o_ref[0] = jnp.dot(xn_ref[...], w_ref[...], preferred_element_type=F32).astype(o_ref.dtype)


def _in_projection(hs, mod8, w_bf, wg_bf, tn):
    split = len(hs) == 2
    B = hs[0].shape[0]
    n_lat = hs[1].shape[1] if split else hs[0].shape[1] - CTX_LEN
    T = CTX_LEN + n_lat
    n = w_bf.shape[1]
    h_specs = [pl.BlockSpec((1, a.shape[1], D_MODEL), lambda b, j: (b, 0, 0)) for a in hs]
    return pl.pallas_call(
        functools.partial(_inproj_kernel, split=split, n_lat=n_lat),
        grid=(B, n // tn),
        in_specs=h_specs + [pl.BlockSpec((1, MOD_ROWS, D_MODEL), lambda b, j: (b, 0, 0)),
                            pl.BlockSpec((D_MODEL, tn), lambda b, j: (0, j)),
                            pl.BlockSpec((D_MODEL, LANES), lambda b, j: (0, 0))],
        out_specs=[pl.BlockSpec((1, T, tn), lambda b, j: (b, 0, j)),
                   pl.BlockSpec((1, T, LANES), lambda b, j: (b, 0, 0))],
        out_shape=[jax.ShapeDtypeStruct((B, T, n), BF16),
                   jax.ShapeDtypeStruct((B, T, LANES), F32)],
        scratch_shapes=[pltpu.VMEM((T, D_MODEL), BF16)],
        compiler_params=_params("parallel", "arbitrary"),
        name="in_projection",
    )(*hs, mod8, w_bf, wg_bf)


def _rope_slab(x, cos, sin):
    lane = lax.broadcasted_iota(jnp.int32, x.shape, 1)
    first_half = (lane & (HEAD_DIM - 1)) < (HEAD_DIM // 2)
    partner = jnp.where(first_half,
                        pltpu.roll(x, LANES - HEAD_DIM // 2, axis=1),
                        pltpu.roll(x, HEAD_DIM // 2, axis=1))
    return x * cos + partner * sin


def _head_norm_slab(x, gain, gmat):
    ss = jnp.dot((x * x).astype(BF16), gmat, preferred_element_type=F32)
    return x * lax.rsqrt(ss * (1.0 / HEAD_DIM) + EPS) * gain


def _prep_kv(k_ref, v_ref, cosk_ref, sink_ref, kn_ref, gmat_ref, kp_ref, vt_ref, nk, use_norm):
    k = k_ref[0, 0:nk, :].astype(F32)
    if use_norm:
        k = _head_norm_slab(k, kn_ref[...], gmat_ref[...])
    k = _rope_slab(k, cosk_ref[0:nk, :], sink_ref[0:nk, :]).astype(BF16)
    for g in range(N_KV):
        kp_ref[g, 0:nk, :] = k[:, g * HEAD_DIM:(g + 1) * HEAD_DIM]
    for blk in range(nk // CHUNK):
        v = v_ref[0, blk * CHUNK:(blk + 1) * CHUNK, :].astype(F32)
        vt_ref[blk] = v.T.astype(BF16)


def _prep_qt(q_ref, cosq_ref, sinq_ref, qn_ref, gmat_ref, use_norm):
    cos = cosq_ref[...]
    sin = sinq_ref[...]
    xts = []
    for s in range(N_HEADS // 2):
        x = q_ref[0, :, s * LANES:(s + 1) * LANES].astype(F32)
        if use_norm:
            x = _head_norm_slab(x, qn_ref[...], gmat_ref[...])
        x = _rope_slab(x, cos, sin) * (HEAD_DIM ** -0.5 * LOG2E)
        xts.append(x.T.astype(BF16))
    per = GROUP // 2
    out = []
    for g in range(N_KV):
        parts = []
        for s in range(g * per, (g + 1) * per):
            parts.append(xts[s][:HEAD_DIM, :])
            parts.append(xts[s][HEAD_DIM:, :])
        out.append(jnp.concatenate(parts, axis=1))
    return out


def _heads_from_t(ot, tq):
    slabs = []
    for j in range(GROUP // 2):
        two = jnp.concatenate([ot[:, (2 * j) * tq:(2 * j + 1) * tq],
                               ot[:, (2 * j + 1) * tq:(2 * j + 2) * tq]], axis=0)
        slabs.append(two.T)
    return jnp.concatenate(slabs, axis=1)


def _gattn_kernel(q_ref, k_ref, v_ref, cosq_ref, sinq_ref, cosk_ref, sink_ref, qn_ref, kn_ref, gmat_ref,
                  o_ref, kp_ref, vt_ref, s_ref, *, tq, nk, ck):
    i = pl.program_id(1)

    @pl.when(i == 0)
    def _():
        _prep_kv(k_ref, v_ref, cosk_ref, sink_ref, kn_ref, gmat_ref, kp_ref, vt_ref, nk, True)

    qts = _prep_qt(q_ref, cosq_ref, sinq_ref, qn_ref, gmat_ref, True)
    n_chunks = nk // ck
    blocks_per_chunk = ck // CHUNK
    outs = []
    for g in range(N_KV):
        qt = qts[g]
        m = None
        for c in range(n_chunks):
            st = jnp.dot(kp_ref[g, c * ck:(c + 1) * ck, :], qt, preferred_element_type=F32)
            s_ref[g, c] = st
            cm = jnp.max(st, axis=0, keepdims=True)
            m = cm if m is None else jnp.maximum(m, cm)
        l = None
        acc = None
        for c in range(n_chunks):
            p = jnp.exp2(s_ref[g, c] - m)
            cl = jnp.sum(p, axis=0, keepdims=True)
            l = cl if l is None else l + cl
            vt = jnp.concatenate(
                [vt_ref[c * blocks_per_chunk + j, g * HEAD_DIM:(g + 1) * HEAD_DIM, :]
                 for j in range(blocks_per_chunk)], axis=1)
            pv = jnp.dot(vt, p.astype(BF16), preferred_element_type=F32)
            acc = pv if acc is None else acc + pv
        outs.append(_heads_from_t(acc / l, tq))
    o_ref[0] = jnp.concatenate(outs, axis=1).astype(o_ref.dtype)


def _global_attention(p_main, rope, qn, kn, gmat, *, q_tile0, n_q_tiles, nk):
    B, T, _ = p_main.shape
    cos, sin = rope
    tq = CHUNK
    ck = 256
    kernel = functools.partial(_gattn_kernel, tq=tq, nk=nk, ck=ck)
    qcol = COL_AQ // BRANCH_W
    return pl.pallas_call(
        kernel,
        grid=(B, n_q_tiles),
        in_specs=[pl.BlockSpec((1, tq, BRANCH_W), lambda b, i: (b, i + q_tile0, qcol)),
                  pl.BlockSpec((1, T, LANES), lambda b, i: (b, 0, COL_AK // LANES)),
                  pl.BlockSpec((1, T, LANES), lambda b, i: (b, 0, COL_AV // LANES)),
                  pl.BlockSpec((tq, LANES), lambda b, i: (i + q_tile0, 0)),
                  pl.BlockSpec((tq, LANES), lambda b, i: (i + q_tile0, 0)),
                  pl.BlockSpec((T, LANES), lambda b, i: (0, 0)),
                  pl.BlockSpec((T, LANES), lambda b, i: (0, 0)),
                  pl.BlockSpec((1, LANES), lambda b, i: (0, 0)),
                  pl.BlockSpec((1, LANES), lambda b, i: (0, 0)),
                  pl.BlockSpec((LANES, LANES), lambda b, i: (0, 0))],
        out_specs=pl.BlockSpec((1, tq, BRANCH_W), lambda b, i: (b, i, 0)),
        out_shape=jax.ShapeDtypeStruct((B, n_q_tiles * tq, BRANCH_W), BF16),
        scratch_shapes=[pltpu.VMEM((N_KV, nk, HEAD_DIM), BF16),
                        pltpu.VMEM((nk // CHUNK, LANES, CHUNK), BF16),
                        pltpu.VMEM((N_KV, nk // ck, ck, GROUP * tq), F32)],
        compiler_params=_params("parallel", "arbitrary"),
        name="global_attention",
    )(p_main, p_main, p_main, cos, sin, cos, sin, qn, kn, gmat)


def _wattn_kernel(wsink_ref, q_ref, k_ref, v_ref, cosq_ref, sinq_ref, cosk_ref, sink_ref,
                  o_ref, kp_ref, vt_ref, *, tq, nk, banded, n_lat_blocks):
    i = pl.program_id(1)

    @pl.when(i == 0)
    def _():
        _prep_kv(k_ref, v_ref, cosk_ref, sink_ref, None, None, kp_ref, vt_ref, nk, False)

    qts = _prep_qt(q_ref, cosq_ref, sinq_ref, None, None, False)
    cols = GROUP * tq
    n_ctx_chunks = CTX_LEN // CHUNK
    lane = lax.broadcasted_iota(jnp.int32, (1, cols), 1)

    if banded:
        prev_blk = i + n_ctx_chunks - 1
        own_blk = i + n_ctx_chunks
        next_blk = jnp.minimum(i + n_ctx_chunks + 1, n_ctx_chunks + n_lat_blocks - 1)
        kk = lax.broadcasted_iota(jnp.int32, (CHUNK, cols), 0)
        qq = lax.broadcasted_iota(jnp.int32, (CHUNK, cols), 1) & (tq - 1)
        ok_prev = jnp.logical_and(kk >= qq, i > 0)
        ok_next = jnp.logical_and(kk <= qq, i < n_lat_blocks - 1)

    outs = []
    for g in range(N_KV):
        qt = qts[g]
        sink_row = jnp.zeros((1, cols), F32)
        for h in range(GROUP):
            in_head = jnp.logical_and(lane >= h * tq, lane < (h + 1) * tq)
            sink_row = jnp.where(in_head, wsink_ref[g * GROUP + h] * LOG2E, sink_row)

        def scores(blk):
            off = blk * CHUNK
            if not isinstance(off, int):
                off = pl.multiple_of(off, CHUNK)
            return jnp.dot(kp_ref[g, pl.ds(off, CHUNK), :], qt, preferred_element_type=F32)

        blocks = [(scores(cb), cb) for cb in range(n_ctx_chunks)]
        if banded:
            blocks.append((jnp.where(ok_prev, scores(prev_blk), -jnp.inf), prev_blk))
            blocks.append((scores(own_blk), own_blk))
            blocks.append((jnp.where(ok_next, scores(next_blk), -jnp.inf), next_blk))
        m = sink_row
        for st, _ in blocks:
            m = jnp.maximum(m, jnp.max(st, axis=0, keepdims=True))
        l = jnp.exp2(sink_row - m)
        acc = None
        for st, blk in blocks:
            p = jnp.exp2(st - m)
            l = l + jnp.sum(p, axis=0, keepdims=True)
            pv = jnp.dot(vt_ref[blk, g * HEAD_DIM:(g + 1) * HEAD_DIM, :], p.astype(BF16),
                         preferred_element_type=F32)
            acc = pv if acc is None else acc + pv
        outs.append(_heads_from_t(acc / l, tq))
    o_ref[0] = jnp.concatenate(outs, axis=1).astype(o_ref.dtype)


def _window_attention(p_main, rope, w_sink, *, q_tile0, n_q_tiles, nk, banded):
    B, T, _ = p_main.shape
    cos, sin = rope
    tq = CHUNK
    n_lat_blocks = (T - CTX_LEN) // CHUNK
    kernel = functools.partial(_wattn_kernel, tq=tq, nk=nk, banded=banded, n_lat_blocks=n_lat_blocks)
    qcol = COL_WQ // BRANCH_W
    return pl.pallas_call(
        kernel,
        grid=(B, n_q_tiles),
        in_specs=[pl.BlockSpec(memory_space=pltpu.SMEM),
                  pl.BlockSpec((1, tq, BRANCH_W), lambda b, i: (b, i + q_tile0, qcol)),
                  pl.BlockSpec((1, T, LANES), lambda b, i: (b, 0, COL_WK // LANES)),
                  pl.BlockSpec((1, T, LANES), lambda b, i: (b, 0, COL_WV // LANES)),
                  pl.BlockSpec((tq, LANES), lambda b, i: (i + q_tile0, 0)),
                  pl.BlockSpec((tq, LANES), lambda b, i: (i + q_tile0, 0)),
                  pl.BlockSpec((T, LANES), lambda b, i: (0, 0)),
                  pl.BlockSpec((T, LANES), lambda b, i: (0, 0))],
        out_specs=pl.BlockSpec((1, tq, BRANCH_W), lambda b, i: (b, i, 0)),
        out_shape=jax.ShapeDtypeStruct((B, n_q_tiles * tq, BRANCH_W), BF16),
        scratch_shapes=[pltpu.VMEM((N_KV, nk, HEAD_DIM), BF16),
                        pltpu.VMEM((nk // CHUNK, LANES, CHUNK), BF16)],
        compiler_params=_params("parallel", "arbitrary"),
        name="window_attention",
    )(w_sink, p_main, p_main, p_main, cos, sin, cos, sin)


def _sgate_kernel(u_ref, v_ref, lng_ref, lnb_ref, ws_ref, bs_ref, o_ref, *, tm):
    u = _gelu_tanh(u_ref[0].astype(F32))
    v = _gelu_tanh(v_ref[0].astype(F32))
    mu = jnp.mean(v, axis=-1, keepdims=True)
    vc = v - mu
    var = jnp.mean(vc * vc, axis=-1, keepdims=True)
    vn = (vc * lax.rsqrt(var + EPS) * lng_ref[...] + lnb_ref[...]).astype(BF16)
    gw = BRANCH_W // B_GROUPS
    chunks = []
    for c in range(tm // CHUNK):
        cols = []
        for g in range(B_GROUPS):
            mixed = jnp.dot(ws_ref[g], vn[c * CHUNK:(c + 1) * CHUNK, g * gw:(g + 1) * gw],
                            preferred_element_type=F32)
            cols.append(mixed + bs_ref[:, g:g + 1])
        chunks.append(jnp.concatenate(cols, axis=1))
    o_ref[0] = (u * jnp.concatenate(chunks, axis=0)).astype(o_ref.dtype)


def _spatial_gating(p_main, ln_g, ln_b, ws_bf, bs_t, *, tile0, n_tiles, tm):
    B, T, _ = p_main.shape
    return pl.pallas_call(
        functools.partial(_sgate_kernel, tm=tm),
        grid=(B, n_tiles),
        in_specs=[pl.BlockSpec((1, tm, BRANCH_W), lambda b, t: (b, t + tile0, COL_GU // BRANCH_W)),
                  pl.BlockSpec((1, tm, BRANCH_W), lambda b, t: (b, t + tile0, COL_GV // BRANCH_W)),
                  pl.BlockSpec((1, BRANCH_W), lambda b, t: (0, 0)),
                  pl.BlockSpec((1, BRANCH_W), lambda b, t: (0, 0)),
                  pl.BlockSpec((B_GROUPS, CHUNK, CHUNK), lambda b, t: (0, 0, 0)),
                  pl.BlockSpec((CHUNK, B_GROUPS), lambda b, t: (0, 0))],
        out_specs=pl.BlockSpec((1, tm, BRANCH_W), lambda b, t: (b, t, 0)),
        out_shape=jax.ShapeDtypeStruct((B, n_tiles * tm, BRANCH_W), BF16),
        compiler_params=_params("parallel", "parallel"),
        name="spatial_gating",
    )(p_main, p_main, ln_g, ln_b, ws_bf, bs_t)


def _conv_kernel(x_ref, w_ref, o_ref, *, T, mode):
    x = x_ref[0].astype(F32)
    row = lax.broadcasted_iota(jnp.int32, (T, 1), 0)
    prev = pltpu.roll(x, 1, axis=0)
    nxt = pltpu.roll(x, T - 1, axis=0)
    prev = jnp.where(jnp.logical_or(row == 0, row == CTX_LEN), 0.0, prev)
    nxt = jnp.where(jnp.logical_or(row == CTX_LEN - 1, row == T - 1), 0.0, nxt)
    y = prev * w_ref[0:1, :] + x * w_ref[1:2, :] + nxt * w_ref[2:3, :]
    z = _silu(y)
    if mode == 'q':
        z = z * (M_HD ** -0.5)
    if mode == 'k':
        o_ref[0] = z.T.astype(o_ref.dtype)
    else:
        o_ref[0] = z.astype(o_ref.dtype)


def _mlstm_conv(p_main, conv_w, mode):
    B, T, _ = p_main.shape
    n_slabs = BRANCH_W // LANES
    part = {'q': 0, 'k': 1, 'v': 2}[mode]
    col0 = COL_MQKV // LANES + part * n_slabs
    if mode == 'k':
        out_spec = pl.BlockSpec((1, LANES, T), lambda b, j: (b, j, 0))
        out_shape = jax.ShapeDtypeStruct((B, BRANCH_W, T), BF16)
    else:
        out_spec = pl.BlockSpec((1, T, LANES), lambda b, j: (b, 0, j))
        out_shape = jax.ShapeDtypeStruct((B, T, BRANCH_W), BF16)
    return pl.pallas_call(
        functools.partial(_conv_kernel, T=T, mode=mode),
        grid=(B, n_slabs),
        in_specs=[pl.BlockSpec((1, T, LANES), lambda b, j: (b, 0, col0 + j)),
                  pl.BlockSpec((3, LANES), lambda b, j: (0, part * n_slabs + j))],
        out_specs=out_spec,
        out_shape=out_shape,
        compiler_params=_params("parallel", "parallel"),
        name="mlstm_conv_" + mode,
    )(p_main, conv_w)


def _split3(x):
    x1 = x.astype(BF16)
    r = x - x1.astype(F32)
    x2 = r.astype(BF16)
    x3 = (r - x2.astype(F32)).astype(BF16)
    return x1, x2, x3


def _mgates_kernel(g_ref, bias_ref, ct_ref, ur_ref, ar_ref, *, T):
    L = CHUNK
    n_chunks = T // L
    n_ctx = CTX_LEN // L
    gall = g_ref[0] + bias_ref[...]
    i_all = gall[0:N_CHAINS]
    f_all = _log_sigmoid(gall[N_CHAINS:2 * N_CHAINS])
    fwd_rows = lax.broadcasted_iota(jnp.int32, (N_CHAINS, L), 0) < M_HEADS
    fwd_col = lax.broadcasted_iota(jnp.int32, (N_CHAINS, 1), 0) < M_HEADS
    lane = lax.broadcasted_iota(jnp.int32, (N_CHAINS, L), 1)
    s_idx = lax.broadcasted_iota(jnp.int32, (L, L), 0)
    t_idx = lax.broadcasted_iota(jnp.int32, (L, L), 1)
    sum_upto = jnp.where(s_idx <= t_idx, 1.0, 0.0).astype(BF16)
    sum_from = jnp.where(s_idx >= t_idx, 1.0, 0.0).astype(BF16)

    def cumsum_dir(f):
        bf = None
        bb = None
        for part in _split3(f):
            a = jnp.dot(part, sum_upto, preferred_element_type=F32)
            b = jnp.dot(part, sum_from, preferred_element_type=F32)
            bf = a if bf is None else bf + a
            bb = b if bb is None else bb + b
        return jnp.where(fwd_rows, bf, bb)

    def cummax_dir(u):
        xf = u
        xb = u
        sh = 1
        while sh < L:
            xf = jnp.maximum(xf, jnp.where(lane >= sh, pltpu.roll(xf, sh, axis=1), -jnp.inf))
            xb = jnp.maximum(xb, jnp.where(lane < L - sh, pltpu.roll(xb, L - sh, axis=1), -jnp.inf))
            sh *= 2
        return jnp.where(fwd_rows, xf, xb)

    b_c, u_c, mloc_c, bl_c, g_c, gmax_c = [], [], [], [], [], []
    for c in range(n_chunks):
        f = f_all[:, c * L:(c + 1) * L]
        i = i_all[:, c * L:(c + 1) * L]
        b = cumsum_dir(f)
        u = i - b
        bl = jnp.sum(f, axis=1, keepdims=True)
        g = bl + u
        b_c.append(b)
        u_c.append(u)
        mloc_c.append(b + cummax_dir(u))
        bl_c.append(bl)
        g_c.append(g)
        gmax_c.append(jnp.max(g, axis=1, keepdims=True))

    def scan(order):
        m = jnp.zeros((N_CHAINS, 1), F32)
        prev, new = {}, {}
        for c in order:
            prev[c] = m
            m = jnp.maximum(bl_c[c] + m, gmax_c[c])
            new[c] = m
        return prev, new

    order_f = list(range(n_chunks))
    order_b = [n_ctx - 1 - j for j in range(n_ctx)] + [n_chunks + n_ctx - 1 - j for j in range(n_ctx, n_chunks)]
    prev_f, new_f = scan(order_f)
    prev_b, new_b = scan(order_b)

    for c in range(n_chunks):
        m_prev = jnp.where(fwd_col, prev_f[c], prev_b[c])
        m_new = jnp.where(fwd_col, new_f[c], new_b[c])
        b = b_c[c]
        m_inter = b + m_prev
        m_t = jnp.maximum(m_inter, mloc_c[c])
        pack = jnp.concatenate([b - m_t, jnp.exp(m_inter - m_t), jnp.exp(-m_t), jnp.exp(g_c[c] - m_new),
                                jnp.zeros((LANES - 4 * N_CHAINS, L), F32)], axis=0)
        ct_ref[0, c * L:(c + 1) * L, :] = pack.T
        ur_ref[0, :, c * L:(c + 1) * L] = u_c[c]
        ar_ref[0, c] = jnp.broadcast_to(jnp.exp(bl_c[c] + m_prev - m_new), (N_CHAINS, LANES))


def _mlstm_gates(gates_row, bias_row):
    B, _, T = gates_row.shape
    n_chunks = T // CHUNK
    return pl.pallas_call(
        functools.partial(_mgates_kernel, T=T),
        grid=(B,),
        in_specs=[pl.BlockSpec((1, 2 * N_CHAINS, T), lambda b: (b, 0, 0)),
                  pl.BlockSpec((2 * N_CHAINS, T), lambda b: (0, 0))],
        out_specs=[pl.BlockSpec((1, T, LANES), lambda b: (b, 0, 0)),
                   pl.BlockSpec((1, N_CHAINS, T), lambda b: (b, 0, 0)),
                   pl.BlockSpec((1, n_chunks, N_CHAINS, LANES), lambda b: (b, 0, 0, 0))],
        out_shape=[jax.ShapeDtypeStruct((B, T, LANES), F32),
                   jax.ShapeDtypeStruct((B, N_CHAINS, T), F32),
                   jax.ShapeDtypeStruct((B, n_chunks, N_CHAINS, LANES), F32)],
        compiler_params=_params("parallel"),
        name="mlstm_gates",
    )(gates_row, bias_row)


def _mlstm_kernel(qf_ref, ktf_ref, vf_ref, ctf_ref, urf_ref, arf_ref,
                  qb_ref, ktb_ref, vb_ref, ctb_ref, urb_ref, arb_ref, of_ref, ob_ref, st_ref):
    L = CHUNK

    @pl.when(pl.program_id(1) == 0)
    def _():
        st_ref[...] = jnp.zeros_like(st_ref)

    t_idx = lax.broadcasted_iota(jnp.int32, (L, L), 0)
    s_idx = lax.broadcasted_iota(jnp.int32, (L, L), 1)
    ones = jnp.ones((L, M_HD), BF16)
    dirs = ((qf_ref, ktf_ref, vf_ref, ctf_ref, urf_ref, arf_ref, of_ref),
            (qb_ref, ktb_ref, vb_ref, ctb_ref, urb_ref, arb_ref, ob_ref))
    for d, (q_ref, kt_ref, v_ref, ct_ref, ur_ref, ar_ref, o_ref) in enumerate(dirs):
        mask = (s_idx <= t_idx) if d == 0 else (s_idx >= t_idx)
        ct = ct_ref[0]
        ur = ur_ref[0]
        ar = ar_ref[0, 0]
        outs = []
        for h in range(M_HEADS):
            c = d * M_HEADS + h
            q = q_ref[0, :, h * M_HD:(h + 1) * M_HD]
            kt = kt_ref[0, h * M_HD:(h + 1) * M_HD, :]
            v = v_ref[0, :, h * M_HD:(h + 1) * M_HD]
            v_ext = jnp.concatenate([v, ones], axis=1)
            cq = ct[:, c:c + 1]
            w_inter = ct[:, N_CHAINS + c:N_CHAINS + c + 1]
            e_negm = ct[:, 2 * N_CHAINS + c:2 * N_CHAINS + c + 1]
            wk = ct[:, 3 * N_CHAINS + c:3 * N_CHAINS + c + 1]
            w_intra = jnp.exp(jnp.where(mask, cq + ur[c:c + 1, :], -jnp.inf))
            s = jnp.dot(q, kt, preferred_element_type=F32) * w_intra
            st_prev = st_ref[d, h]
            tot = (jnp.dot(s.astype(BF16), v_ext, preferred_element_type=F32)
                   + w_inter * jnp.dot(q, st_prev.astype(BF16), preferred_element_type=F32))
            den = jnp.maximum(jnp.abs(tot[:, M_HD:]), e_negm)
            outs.append(tot[:, :M_HD] / den)
            x = (v_ext.astype(F32) * wk).astype(BF16)
            a = jnp.concatenate([ar[c:c + 1, :], ar[c:c + 1, :]], axis=1)
            st_ref[d, h] = a * st_prev + jnp.dot(kt, x, preferred_element_type=F32)
        o_ref[0, 0] = jnp.concatenate(outs, axis=1).astype(o_ref.dtype)


def _mlstm_scan(zq, zkt, zv, ct, ur, ar):
    B, T, _ = zq.shape
    n_chunks = T // CHUNK
    n_ctx = CTX_LEN // CHUNK

    def fwd(j):
        return j

    def bwd(j):
        return jnp.where(j < n_ctx, n_ctx - 1 - j, n_chunks + n_ctx - 1 - j)

    def specs(order):
        return [pl.BlockSpec((1, CHUNK, BRANCH_W), lambda b, j: (b, order(j), 0)),
                pl.BlockSpec((1, BRANCH_W, CHUNK), lambda b, j: (b, 0, order(j))),
                pl.BlockSpec((1, CHUNK, BRANCH_W), lambda b, j: (b, order(j), 0)),
                pl.BlockSpec((1, CHUNK, LANES), lambda b, j: (b, order(j), 0)),
                pl.BlockSpec((1, N_CHAINS, CHUNK), lambda b, j: (b, 0, order(j))),
                pl.BlockSpec((1, 1, N_CHAINS, LANES), lambda b, j: (b, order(j), 0, 0))]

    out = pl.pallas_call(
        _mlstm_kernel,
        grid=(B, n_chunks),
        in_specs=specs(fwd) + specs(bwd),
        out_specs=[pl.BlockSpec((1, 1, CHUNK, BRANCH_W), lambda b, j: (0, b, fwd(j), 0)),
                   pl.BlockSpec((1, 1, CHUNK, BRANCH_W), lambda b, j: (0, b, bwd(j), 0))],
        out_shape=[jax.ShapeDtypeStruct((1, B, T, BRANCH_W), F32),
                   jax.ShapeDtypeStruct((1, B, T, BRANCH_W), F32)],
        scratch_shapes=[pltpu.VMEM((2, M_HEADS, M_HD, 2 * M_HD), F32)],
        compiler_params=_params("parallel", "arbitrary"),
        name="mlstm_scan",
    )(zq, zkt, zv, ct, ur, ar, zq, zkt, zv, ct, ur, ar)
    return out[0][0], out[1][0]


def _merge_kernel(*refs, split):
    if split:
        (yac_ref, yax_ref, ywc_ref, ywx_ref, yg_ref, hf_ref, hb_ref, hc_ref, hx_ref,
         mod_ref, wg_ref, wbr_ref, wo_ref, gpost_ref, mnorm_ref, o_ref) = refs
        is_ctx = pl.program_id(1) == 0
        h = jnp.where(is_ctx, hc_ref[0], hx_ref[0])
        ya = jnp.where(is_ctx, yac_ref[0], yax_ref[0])
        yw = jnp.where(is_ctx, ywc_ref[0], ywx_ref[0])
        mult = jnp.where(is_ctx, mod_ref[0, 0:1, :], mod_ref[0, 3:4, :])
        shift = jnp.where(is_ctx, mod_ref[0, 1:2, :], mod_ref[0, 4:5, :])
        gate = jnp.where(is_ctx, mod_ref[0, 2:3, :], mod_ref[0, 5:6, :])
    else:
        (yax_ref, ywx_ref, yg_ref, hf_ref, hb_ref, hx_ref,
         mod_ref, wg_ref, wbr_ref, wo_ref, gpost_ref, mnorm_ref, o_ref) = refs
        h, ya, yw = hx_ref[0], yax_ref[0], ywx_ref[0]
        mult, shift, gate = mod_ref[0, 3:4, :], mod_ref[0, 4:5, :], mod_ref[0, 5:6, :]

    xn = _norm_mod(h, mult, shift)

    def proj(col, width):
        return jnp.dot(xn, wg_ref[:, col:col + width], preferred_element_type=F32)

    hm = hf_ref[0] + hb_ref[0]
    parts = []
    for hd in range(M_HEADS):
        x = hm[:, hd * M_HD:(hd + 1) * M_HD]
        ms = jnp.mean(x * x, axis=-1, keepdims=True)
        parts.append(x * lax.rsqrt(ms + EPS) * mnorm_ref[:, hd * M_HD:(hd + 1) * M_HD])
    ym = _sigmoid(proj(GCOL_MO, BRANCH_W)) * jnp.concatenate(parts, axis=1)
    ys = (ya.astype(F32), yw.astype(F32), yg_ref[0].astype(F32), ym)
    acc = None
    for k in range(N_BRANCH):
        yk = (ys[k] * _silu(proj(GCOL_BRANCH_GATE[k], BRANCH_W))).astype(BF16)
        bp = jnp.dot(yk, wbr_ref[k], preferred_element_type=F32)
        g = _sigmoid(proj(GCOL_MERGE + k * D_MODEL, D_MODEL))
        acc = g * bp if acc is None else acc + g * bp
    y = jnp.dot(acc.astype(BF16), wo_ref[...], preferred_element_type=F32)
    ms = jnp.mean(y * y, axis=-1, keepdims=True)
    yn = y * lax.rsqrt(ms + EPS) * gpost_ref[...]
    o_ref[0] = h + gate * yn


def _merge(ya, yw, yg, hf, hb, hs, mod8, wg_bf, wbr_bf, wo_bf, g_post, m_norm):
    split = len(hs) == 2
    tm = CTX_LEN
    B = hf.shape[0]
    T = hf.shape[1]
    n_ctx_tiles = CTX_LEN // tm
    n_tiles = T // tm if split else T // tm - n_ctx_tiles
    bw = BRANCH_W

    def lat(w):
        if split:
            return pl.BlockSpec((1, tm, w), lambda b, t: (b, jnp.maximum(t - n_ctx_tiles, 0), 0))
        return pl.BlockSpec((1, tm, w), lambda b, t: (b, t, 0))

    def ctx(w):
        return pl.BlockSpec((1, tm, w), lambda b, t: (b, 0, 0))

    def stream(w):
        off = 0 if split else n_ctx_tiles
        return pl.BlockSpec((1, tm, w), lambda b, t: (b, t + off, 0))

    def const(shape):
        nd = len(shape)
        return pl.BlockSpec(shape, lambda b, t: (0,) * nd)

    if split:
        branch_specs = [ctx(bw), lat(bw), ctx(bw), lat(bw), stream(bw)]
        branch_args = [ya[0], ya[1], yw[0], yw[1], yg]
        h_specs = [ctx(D_MODEL), lat(D_MODEL)]
    else:
        branch_specs = [lat(bw), lat(bw), lat(bw)]
        branch_args = [ya, yw, yg]
        h_specs = [stream(D_MODEL)]
    return pl.pallas_call(
        functools.partial(_merge_kernel, split=split),
        grid=(B, n_tiles),
        in_specs=branch_specs + [stream(bw), stream(bw)] + h_specs + [
            pl.BlockSpec((1, MOD_ROWS, D_MODEL), lambda b, t: (b, 0, 0)),
            const((D_MODEL, N_GATE)), const((N_BRANCH, bw, D_MODEL)), const((D_MODEL, D_MODEL)),
            const((1, D_MODEL)), const((1, bw))],
        out_specs=pl.BlockSpec((1, tm, D_MODEL), lambda b, t: (b, t, 0)),
        out_shape=jax.ShapeDtypeStruct((B, n_tiles * tm, D_MODEL), F32),
        compiler_params=_params("parallel", "parallel"),
        name="merge",
    )(*branch_args, hf, hb, *hs, mod8, wg_bf, wbr_bf, wo_bf, g_post, m_norm)


def _rope_tables(n_lat):
    rows = n_lat // GRID_W
    row = jnp.repeat(jnp.arange(rows), GRID_W).astype(F32)
    col = jnp.tile(jnp.arange(GRID_W), rows).astype(F32)
    n_freq = HEAD_DIM // 4
    inv = ROPE_THETA ** (-jnp.arange(n_freq, dtype=F32) / n_freq)
    ang = jnp.concatenate([row[:, None] * inv, col[:, None] * inv], axis=-1)
    cos, sin = jnp.cos(ang), jnp.sin(ang)
    cos_h = jnp.concatenate([cos, cos], axis=-1)
    sin_h = jnp.concatenate([-sin, sin], axis=-1)
    cos_t = jnp.concatenate([jnp.ones((CTX_LEN, HEAD_DIM), F32), cos_h], axis=0)
    sin_t = jnp.concatenate([jnp.zeros((CTX_LEN, HEAD_DIM), F32), sin_h], axis=0)
    return jnp.tile(cos_t, (1, LANES // HEAD_DIM)), jnp.tile(sin_t, (1, LANES // HEAD_DIM))


def _gather_cols(w, names):
    sl = _ref_slices()
    runs = []
    for n in names:
        start, size = sl[n]
        if runs and runs[-1][1] == start:
            runs[-1][1] = start + size
        else:
            runs.append([start, start + size])
    return jnp.concatenate([w[:, a:b] for a, b in runs], axis=1)


def _split_w_in(w):
    w = w.astype(BF16)
    lstm_gates = jnp.pad(_gather_cols(w, ('m_i', 'm_f')), ((0, 0), (0, LANES - 2 * N_CHAINS)))
    return _gather_cols(w, _MAIN_ORDER), _gather_cols(w, _GATE_ORDER), lstm_gates


def kernel(x, c, ctx, c_ctx, w_mod, b_mod, g_pre, g_post, w_in, a_q_norm, a_k_norm, w_sink, sg_ln_g, sg_ln_b,
           sg_w, sg_b, m_conv, m_b_i, m_b_f, m_norm, w_branch, w_out):
    B, n_lat, D = x.shape
    depth = w_mod.shape[0]
    assert depth == 2, "the context stream is only carried from the first layer to the last"
    T = CTX_LEN + n_lat
    n_ctx_chunks = CTX_LEN // CHUNK
    n_lat_chunks = n_lat // CHUNK
    rope = _rope_tables(n_lat)
    gmat = (jnp.arange(LANES)[:, None] // HEAD_DIM == jnp.arange(LANES)[None, :] // HEAD_DIM).astype(BF16)

    hs = (ctx, x)
    mod_rows = 16
    cc = jnp.zeros((mod_rows, D), F32).at[:B].set(c).at[B].set(c_ctx)
    tm_tok = CTX_LEN

    for l in range(depth):
        need_ctx = l < depth - 1
        mod = _modulation(cc, w_mod[l].astype(BF16), b_mod[l][None, :])
        shift, scale, gate = mod[:, :D], mod[:, D:2 * D], mod[:, 2 * D:]
        mult = g_pre[l][None, :] * (1.0 + scale)
        zero = jnp.zeros((B, D), F32)
        mod8 = jnp.stack([jnp.broadcast_to(mult[B], (B, D)), jnp.broadcast_to(shift[B], (B, D)),
                          jnp.broadcast_to(gate[B], (B, D)), mult[:B], shift[:B], gate[:B], zero, zero],
                         axis=1)

        w_main, w_gate, w_lstm = _split_w_in(w_in[l])
        p_main, p_gates = _in_projection(hs, mod8, w_main, w_lstm, tn=1024)

        qn = jnp.tile(a_q_norm[l], LANES // HEAD_DIM)[None, :]
        kn = jnp.tile(a_k_norm[l], LANES // HEAD_DIM)[None, :]
        ya = _global_attention(p_main, rope, qn, kn, gmat,
                               q_tile0=n_ctx_chunks, n_q_tiles=n_lat_chunks, nk=T)
        yw = _window_attention(p_main, rope, w_sink[l], q_tile0=n_ctx_chunks, n_q_tiles=n_lat_chunks,
                               nk=T, banded=True)
        if need_ctx:
            ya_c = _global_attention(p_main, rope, qn, kn, gmat, q_tile0=0, n_q_tiles=n_ctx_chunks, nk=CTX_LEN)
            yw_c = _window_attention(p_main, rope, w_sink[l], q_tile0=0, n_q_tiles=n_ctx_chunks,
                                     nk=CTX_LEN, banded=False)
            ya = (ya_c, ya)
            yw = (yw_c, yw)

        tile0 = 0 if need_ctx else CTX_LEN // tm_tok
        n_tiles = T // tm_tok - tile0
        yg = _spatial_gating(p_main, sg_ln_g[l][None, :], sg_ln_b[l][None, :], sg_w[l].astype(BF16),
                             sg_b[l].T, tile0=tile0, n_tiles=n_tiles, tm=tm_tok)

        zq = _mlstm_conv(p_main, m_conv[l], 'q')
        zkt = _mlstm_conv(p_main, m_conv[l], 'k')
        zv = _mlstm_conv(p_main, m_conv[l], 'v')
        gates_row = jnp.swapaxes(p_gates[:, :, :2 * N_CHAINS], 1, 2)
        bias_row = jnp.broadcast_to(
            jnp.concatenate([m_b_i[l].reshape(-1), m_b_f[l].reshape(-1)])[:, None], (2 * N_CHAINS, T))
        ct, ur, ar = _mlstm_gates(gates_row, bias_row)
        hf, hb = _mlstm_scan(zq, zkt, zv, ct, ur, ar)

        out = _merge(ya, yw, yg, hf, hb, hs, mod8, w_gate, w_branch[l].astype(BF16), w_out[l].astype(BF16),
                     g_post[l][None, :], m_norm[l][None, :])
        if not need_ctx:
            return out
        hs = (out,)
    return hs[0][:, CTX_LEN:]
```

```python
import functools

import jax
import jax.numpy as jnp
from jax import lax
from jax.experimental import pallas as pl
from jax.experimental.pallas import tpu as pltpu

F32 = jnp.float32
BF16 = jnp.bfloat16

D_MODEL = 1024
GRID_W = 64
CTX_LEN = 256
N_BRANCH = 4
BRANCH_W = 512
HEAD_DIM = 64
N_HEADS = 8
N_KV = 2
GROUP = N_HEADS // N_KV
WINDOW = 128
CHUNK = 128
B_GROUPS = 4
M_HEADS = 4
M_HD = 128
N_CHAINS = 2 * M_HEADS
ROPE_THETA = 10000.0
EPS = 1e-6
LOG2E = 1.4426950408889634
ATTN_TILES_PER_STEP = 2

LANES = 128
VMEM_LIMIT = 48 * 1024 * 1024

COL_AQ = 0
COL_WQ = 512
COL_AK = 1024
COL_AV = 1152
COL_WK = 1280
COL_WV = 1408
COL_GU = 1536
COL_GV = 2048
COL_MQKV = 2560
N_MAIN = 4096
GCOL_BRANCH_GATE = (0, 512, 1024, 2048)
GCOL_MO = 1536
GCOL_MERGE = 2560
N_GATE = GCOL_MERGE + N_BRANCH * D_MODEL

_REF_LAYOUT = (
    ('a_q', 512), ('a_k', 128), ('a_v', 128), ('a_gate', 512),
    ('w_q', 512), ('w_k', 128), ('w_v', 128), ('w_gate', 512),
    ('g_u', 512), ('g_v', 512), ('g_gate', 512),
    ('m_qkv', 1536), ('m_i', 8), ('m_f', 8), ('m_o', 512), ('m_gate', 512),
    ('merge', 4096),
)
_MAIN_ORDER = ('a_q', 'w_q', 'a_k', 'a_v', 'w_k', 'w_v', 'g_u', 'g_v', 'm_qkv')
_GATE_ORDER = ('a_gate', 'w_gate', 'g_gate', 'm_o', 'm_gate', 'merge')


def _ref_slices():
    out, start = {}, 0
    for name, size in _REF_LAYOUT:
        out[name] = (start, size)
        start += size
    return out


def _params(*sem):
    return pltpu.CompilerParams(dimension_semantics=sem, vmem_limit_bytes=VMEM_LIMIT)


def _sigmoid(x):
    return 1.0 / (1.0 + jnp.exp(-x))


def _silu(x):
    return x * _sigmoid(x)


def _gelu_tanh(x):
    c = 0.7978845608028654
    return 0.5 * x * (1.0 + jnp.tanh(c * (x + 0.044715 * (x * x * x))))


def _log_sigmoid(x):
    return -(jnp.maximum(-x, 0.0) + jnp.log1p(jnp.exp(-jnp.abs(x))))


def _mod_kernel(c_ref, w_ref, b_ref, o_ref):
    a = _silu(c_ref[...]).astype(BF16)
    o_ref[...] = jnp.dot(a, w_ref[...], preferred_element_type=F32) + b_ref[...]


def _modulation(cc, w_mod_bf, b_mod):
    rows = cc.shape[0]
    n = w_mod_bf.shape[1]
    tn = 1024
    return pl.pallas_call(
        _mod_kernel,
        grid=(n // tn,),
        in_specs=[pl.BlockSpec((rows, D_MODEL), lambda j: (0, 0)),
                  pl.BlockSpec((D_MODEL, tn), lambda j: (0, j)),
                  pl.BlockSpec((1, tn), lambda j: (0, j))],
        out_specs=pl.BlockSpec((rows, tn), lambda j: (0, j)),
        out_shape=jax.ShapeDtypeStruct((rows, n), F32),
        compiler_params=_params("parallel"),
        name="modulation",
    )(cc, w_mod_bf, b_mod)


MOD_ROWS = 8


def _norm_mod(x, mult, shift):
    ms = jnp.mean(x * x, axis=-1, keepdims=True)
    return (x * lax.rsqrt(ms + EPS) * mult + shift).astype(BF16)


def _inproj_kernel(hc_ref, hx_ref, mod_ref, w_ref, wg_ref, o_ref, og_ref, xn_ref, *, n_lat):
    rb = CTX_LEN

    @pl.when(pl.program_id(1) == 0)
    def _():
        xn_ref[0:rb, :] = _norm_mod(hc_ref[0], mod_ref[0, 0:1, :], mod_ref[0, 1:2, :])

        def norm_rows(r, carry):
            off = pl.multiple_of(r * rb, rb)
            x = hx_ref[0, pl.ds(off, rb), :]
            xn_ref[pl.ds(rb + off, rb), :] = _norm_mod(x, mod_ref[0, 3:4, :], mod_ref[0, 4:5, :])
            return carry

        lax.fori_loop(0, n_lat // rb, norm_rows, 0)
        og_ref[0] = jnp.dot(xn_ref[...], wg_ref[...], preferred_element_type=F32)

    o_ref[0] = jnp.dot(xn_ref[...], w_ref[...], preferred_element_type=F32).astype(o_ref.dtype)


def _in_projection(hc, hx, mod8, w_bf, wg_bf, tn):
    B, n_lat, _ = hx.shape
    T = CTX_LEN + n_lat
    n = w_bf.shape[1]
    h_specs = [pl.BlockSpec((1, a.shape[1], D_MODEL), lambda b, j: (b, 0, 0)) for a in (hc, hx)]
    return pl.pallas_call(
        functools.partial(_inproj_kernel, n_lat=n_lat),
        grid=(B, n // tn),
        in_specs=h_specs + [pl.BlockSpec((1, MOD_ROWS, D_MODEL), lambda b, j: (b, 0, 0)),
                            pl.BlockSpec((D_MODEL, tn), lambda b, j: (0, j)),
                            pl.BlockSpec((D_MODEL, LANES), lambda b, j: (0, 0))],
        out_specs=[pl.BlockSpec((1, T, tn), lambda b, j: (b, 0, j)),
                   pl.BlockSpec((1, T, LANES), lambda b, j: (b, 0, 0))],
        out_shape=[jax.ShapeDtypeStruct((B, T, n), BF16),
                   jax.ShapeDtypeStruct((B, T, LANES), F32)],
        scratch_shapes=[pltpu.VMEM((T, D_MODEL), BF16)],
        compiler_params=_params("parallel", "arbitrary"),
        name="in_projection",
    )(hc, hx, mod8, w_bf, wg_bf)


def _rope_slab(x, cos, sin):
    lane = lax.broadcasted_iota(jnp.int32, x.shape, 1)
    first_half = (lane & (HEAD_DIM - 1)) < (HEAD_DIM // 2)
    partner = jnp.where(first_half,
                        pltpu.roll(x, LANES - HEAD_DIM // 2, axis=1),
                        pltpu.roll(x, HEAD_DIM // 2, axis=1))
    return x * cos + partner * sin


def _head_norm_slab(x, gain, gmat):
    ss = jnp.dot((x * x).astype(BF16), gmat, preferred_element_type=F32)
    return x * lax.rsqrt(ss * (1.0 / HEAD_DIM) + EPS) * gain


def _prep_kv(k_ref, v_ref, cosk_ref, sink_ref, kn_ref, gmat_ref, kp_ref, vt_ref, nk, use_norm):
    k = k_ref[0, 0:nk, :].astype(F32)
    if use_norm:
        k = _head_norm_slab(k, kn_ref[...], gmat_ref[...])
    k = _rope_slab(k, cosk_ref[0:nk, :], sink_ref[0:nk, :]).astype(BF16)
    for g in range(N_KV):
        kp_ref[g, 0:nk, :] = k[:, g * HEAD_DIM:(g + 1) * HEAD_DIM]
    for blk in range(nk // CHUNK):
        v = v_ref[0, blk * CHUNK:(blk + 1) * CHUNK, :].astype(F32)
        vt_ref[blk] = v.T.astype(BF16)


def _prep_qt(q_ref, cosq_ref, sinq_ref, qn_ref, gmat_ref, use_norm, r0, tq):
    cos = cosq_ref[r0:r0 + tq, :]
    sin = sinq_ref[r0:r0 + tq, :]
    xts = []
    for s in range(N_HEADS // 2):
        x = q_ref[0, r0:r0 + tq, s * LANES:(s + 1) * LANES].astype(F32)
        if use_norm:
            x = _head_norm_slab(x, qn_ref[...], gmat_ref[...])
        x = _rope_slab(x, cos, sin) * (HEAD_DIM ** -0.5 * LOG2E)
        xts.append(x.T.astype(BF16))
    per = GROUP // 2
    out = []
    for g in range(N_KV):
        parts = []
        for s in range(g * per, (g + 1) * per):
            parts.append(xts[s][:HEAD_DIM, :])
            parts.append(xts[s][HEAD_DIM:, :])
        out.append(jnp.concatenate(parts, axis=1))
    return out


def _heads_from_t(ot, tq):
    slabs = []
    for j in range(GROUP // 2):
        two = jnp.concatenate([ot[:, (2 * j) * tq:(2 * j + 1) * tq],
                               ot[:, (2 * j + 1) * tq:(2 * j + 2) * tq]], axis=0)
        slabs.append(two.T)
    return jnp.concatenate(slabs, axis=1)


def _gattn_kernel(q_ref, k_ref, v_ref, cosq_ref, sinq_ref, cosk_ref, sink_ref, qn_ref, kn_ref, gmat_ref,
                  o_ref, kp_ref, vt_ref, s_ref, *, tq, nsub, nk, ck):
    i = pl.program_id(1)

    @pl.when(i == 0)
    def _():
        _prep_kv(k_ref, v_ref, cosk_ref, sink_ref, kn_ref, gmat_ref, kp_ref, vt_ref, nk, True)

    qts = [_prep_qt(q_ref, cosq_ref, sinq_ref, qn_ref, gmat_ref, True, sub * tq, tq) for sub in range(nsub)]
    items = [(sub, g) for sub in range(nsub) for g in range(N_KV)]
    n_chunks = nk // ck
    blocks_per_chunk = ck // CHUNK
    gw = GROUP * HEAD_DIM

    def score_chunk(k, c, m):
        sub, g = items[k]
        st = jnp.dot(kp_ref[g, c * ck:(c + 1) * ck, :], qts[sub][g], preferred_element_type=F32)
        s_ref[k, c * ck:(c + 1) * ck, :] = st
        cm = jnp.max(st, axis=0, keepdims=True)
        return cm if m is None else jnp.maximum(m, cm)

    def value_chunk(k, c, m, l, acc):
        _, g = items[k]
        p = jnp.exp2(s_ref[k, c * ck:(c + 1) * ck, :] - m)
        cl = jnp.sum(p, axis=0, keepdims=True)
        vt = jnp.concatenate(
            [vt_ref[c * blocks_per_chunk + j, g * HEAD_DIM:(g + 1) * HEAD_DIM, :]
             for j in range(blocks_per_chunk)], axis=1)
        pv = jnp.dot(vt, p.astype(BF16), preferred_element_type=F32)
        return (cl, pv) if l is None else (l + cl, acc + pv)

    m_next = None
    for c in range(n_chunks):
        m_next = score_chunk(0, c, m_next)
    for k, (sub, g) in enumerate(items):
        m, m_next = m_next, None
        l = acc = None
        for c in range(n_chunks):
            l, acc = value_chunk(k, c, m, l, acc)
            if k + 1 < len(items):
                m_next = score_chunk(k + 1, c, m_next)
        o_ref[0, sub * tq:(sub + 1) * tq, g * gw:(g + 1) * gw] = _heads_from_t(acc / l, tq).astype(o_ref.dtype)


def _global_attention(p_main, rope, qn, kn, gmat, *, q_tile0, n_q_tiles, nk):
    B, T, _ = p_main.shape
    cos, sin = rope
    tq = CHUNK
    nsub = ATTN_TILES_PER_STEP
    tb = nsub * tq
    blk0 = q_tile0 // nsub
    ck = 256
    kernel = functools.partial(_gattn_kernel, tq=tq, nsub=nsub, nk=nk, ck=ck)
    qcol = COL_AQ // BRANCH_W
    return pl.pallas_call(
        kernel,
        grid=(B, n_q_tiles // nsub),
        in_specs=[pl.BlockSpec((1, tb, BRANCH_W), lambda b, i: (b, i + blk0, qcol)),
                  pl.BlockSpec((1, T, LANES), lambda b, i: (b, 0, COL_AK // LANES)),
                  pl.BlockSpec((1, T, LANES), lambda b, i: (b, 0, COL_AV // LANES)),
                  pl.BlockSpec((tb, LANES), lambda b, i: (i + blk0, 0)),
                  pl.BlockSpec((tb, LANES), lambda b, i: (i + blk0, 0)),
                  pl.BlockSpec((T, LANES), lambda b, i: (0, 0)),
                  pl.BlockSpec((T, LANES), lambda b, i: (0, 0)),
                  pl.BlockSpec((1, LANES), lambda b, i: (0, 0)),
                  pl.BlockSpec((1, LANES), lambda b, i: (0, 0)),
                  pl.BlockSpec((LANES, LANES), lambda b, i: (0, 0))],
        out_specs=pl.BlockSpec((1, tb, BRANCH_W), lambda b, i: (b, i, 0)),
        out_shape=jax.ShapeDtypeStruct((B, n_q_tiles * tq, BRANCH_W), BF16),
        scratch_shapes=[pltpu.VMEM((N_KV, nk, HEAD_DIM), BF16),
                        pltpu.VMEM((nk // CHUNK, LANES, CHUNK), BF16),
                        pltpu.VMEM((nsub * N_KV, nk, GROUP * tq), F32)],
        compiler_params=_params("parallel", "arbitrary"),
        name="global_attention",
    )(p_main, p_main, p_main, cos, sin, cos, sin, qn, kn, gmat)


def _wattn_kernel(wsink_ref, q_ref, k_ref, v_ref, cosq_ref, sinq_ref, cosk_ref, sink_ref,
                  o_ref, kp_ref, vt_ref, *, tq, nsub, nk, banded, n_lat_blocks):
    step = pl.program_id(1)

    @pl.when(step == 0)
    def _():
        _prep_kv(k_ref, v_ref, cosk_ref, sink_ref, None, None, kp_ref, vt_ref, nk, False)

    cols = GROUP * tq
    gw = GROUP * HEAD_DIM
    n_ctx_chunks = CTX_LEN // CHUNK
    lane = lax.broadcasted_iota(jnp.int32, (1, cols), 1)
    sink_rows = []
    for g in range(N_KV):
        sink_row = jnp.zeros((1, cols), F32)
        for h in range(GROUP):
            in_head = jnp.logical_and(lane >= h * tq, lane < (h + 1) * tq)
            sink_row = jnp.where(in_head, wsink_ref[g * GROUP + h] * LOG2E, sink_row)
        sink_rows.append(sink_row)
    if banded:
        kk = lax.broadcasted_iota(jnp.int32, (CHUNK, cols), 0)
        qq = lax.broadcasted_iota(jnp.int32, (CHUNK, cols), 1) & (tq - 1)

    for sub in range(nsub):
        i = step * nsub + sub
        qts = _prep_qt(q_ref, cosq_ref, sinq_ref, None, None, False, sub * tq, tq)
        if banded:
            prev_blk = i + n_ctx_chunks - 1
            own_blk = i + n_ctx_chunks
            next_blk = jnp.minimum(i + n_ctx_chunks + 1, n_ctx_chunks + n_lat_blocks - 1)
            ok_prev = jnp.logical_and(kk >= qq, i > 0)
            ok_next = jnp.logical_and(kk <= qq, i < n_lat_blocks - 1)

        for g in range(N_KV):
            qt = qts[g]
            sink_row = sink_rows[g]

            def scores(blk):
                off = blk * CHUNK
                if not isinstance(off, int):
                    off = pl.multiple_of(off, CHUNK)
                return jnp.dot(kp_ref[g, pl.ds(off, CHUNK), :], qt, preferred_element_type=F32)

            blocks = [(scores(cb), cb) for cb in range(n_ctx_chunks)]
            if banded:
                blocks.append((jnp.where(ok_prev, scores(prev_blk), -jnp.inf), prev_blk))
                blocks.append((scores(own_blk), own_blk))
                blocks.append((jnp.where(ok_next, scores(next_blk), -jnp.inf), next_blk))
            m = sink_row
            for st, _ in blocks:
                m = jnp.maximum(m, jnp.max(st, axis=0, keepdims=True))
            l = jnp.exp2(sink_row - m)
            acc = None
            for st, blk in blocks:
                p = jnp.exp2(st - m)
                l = l + jnp.sum(p, axis=0, keepdims=True)
                pv = jnp.dot(vt_ref[blk, g * HEAD_DIM:(g + 1) * HEAD_DIM, :], p.astype(BF16),
                             preferred_element_type=F32)
                acc = pv if acc is None else acc + pv
            o_ref[0, sub * tq:(sub + 1) * tq, g * gw:(g + 1) * gw] = (
                _heads_from_t(acc / l, tq).astype(o_ref.dtype))


def _window_attention(p_main, rope, w_sink, *, q_tile0, n_q_tiles, nk, banded):
    B, T, _ = p_main.shape
    cos, sin = rope
    tq = CHUNK
    nsub = ATTN_TILES_PER_STEP
    tb = nsub * tq
    blk0 = q_tile0 // nsub
    n_lat_blocks = (T - CTX_LEN) // CHUNK
    kernel = functools.partial(_wattn_kernel, tq=tq, nsub=nsub, nk=nk, banded=banded,
                               n_lat_blocks=n_lat_blocks)
    qcol = COL_WQ // BRANCH_W
    return pl.pallas_call(
        kernel,
        grid=(B, n_q_tiles // nsub),
        in_specs=[pl.BlockSpec(memory_space=pltpu.SMEM),
                  pl.BlockSpec((1, tb, BRANCH_W), lambda b, i: (b, i + blk0, qcol)),
                  pl.BlockSpec((1, T, LANES), lambda b, i: (b, 0, COL_WK // LANES)),
                  pl.BlockSpec((1, T, LANES), lambda b, i: (b, 0, COL_WV // LANES)),
                  pl.BlockSpec((tb, LANES), lambda b, i: (i + blk0, 0)),
                  pl.BlockSpec((tb, LANES), lambda b, i: (i + blk0, 0)),
                  pl.BlockSpec((T, LANES), lambda b, i: (0, 0)),
                  pl.BlockSpec((T, LANES), lambda b, i: (0, 0))],
        out_specs=pl.BlockSpec((1, tb, BRANCH_W), lambda b, i: (b, i, 0)),
        out_shape=jax.ShapeDtypeStruct((B, n_q_tiles * tq, BRANCH_W), BF16),
        scratch_shapes=[pltpu.VMEM((N_KV, nk, HEAD_DIM), BF16),
                        pltpu.VMEM((nk // CHUNK, LANES, CHUNK), BF16)],
        compiler_params=_params("parallel", "arbitrary"),
        name="window_attention",
    )(w_sink, p_main, p_main, p_main, cos, sin, cos, sin)


def _sgate_kernel(*refs, nt, tm):
    u_refs, v_refs = refs[:nt], refs[nt:2 * nt]
    lng_ref, lnb_ref, ws_ref, bs_ref, o_ref = refs[2 * nt:]
    gw = BRANCH_W // B_GROUPS
    for i in range(nt):
        u = _gelu_tanh(u_refs[i][0].astype(F32))
        v = _gelu_tanh(v_refs[i][0].astype(F32))
        mu = jnp.mean(v, axis=-1, keepdims=True)
        vc = v - mu
        var = jnp.mean(vc * vc, axis=-1, keepdims=True)
        vn = (vc * lax.rsqrt(var + EPS) * lng_ref[...] + lnb_ref[...]).astype(BF16)
        chunks = []
        for c in range(tm // CHUNK):
            cols = []
            for g in range(B_GROUPS):
                mixed = jnp.dot(ws_ref[g], vn[c * CHUNK:(c + 1) * CHUNK, g * gw:(g + 1) * gw],
                                preferred_element_type=F32)
                cols.append(mixed + bs_ref[:, g:g + 1])
            chunks.append(jnp.concatenate(cols, axis=1))
        o_ref[0, i * tm:(i + 1) * tm, :] = (u * jnp.concatenate(chunks, axis=0)).astype(o_ref.dtype)


def _spatial_gating(p_main, ln_g, ln_b, ws_bf, bs_t, *, tile0, n_tiles, nt):
    B, T, _ = p_main.shape
    tm = CTX_LEN

    def tiles(col):
        return [pl.BlockSpec((1, tm, BRANCH_W), lambda b, t, i=i: (b, nt * t + i + tile0, col))
                for i in range(nt)]

    return pl.pallas_call(
        functools.partial(_sgate_kernel, nt=nt, tm=tm),
        grid=(B, n_tiles // nt),
        in_specs=tiles(COL_GU // BRANCH_W) + tiles(COL_GV // BRANCH_W) + [
            pl.BlockSpec((1, BRANCH_W), lambda b, t: (0, 0)),
            pl.BlockSpec((1, BRANCH_W), lambda b, t: (0, 0)),
            pl.BlockSpec((B_GROUPS, CHUNK, CHUNK), lambda b, t: (0, 0, 0)),
            pl.BlockSpec((CHUNK, B_GROUPS), lambda b, t: (0, 0))],
        out_specs=pl.BlockSpec((1, nt * tm, BRANCH_W), lambda b, t: (b, t, 0)),
        out_shape=jax.ShapeDtypeStruct((B, n_tiles * tm, BRANCH_W), BF16),
        compiler_params=_params("parallel", "parallel"),
        name="spatial_gating",
    )(*([p_main] * (2 * nt)), ln_g, ln_b, ws_bf, bs_t)


def _conv_kernel(xq_ref, xk_ref, xv_ref, w_ref, oq_ref, okt_ref, ov_ref, *, T):
    row = lax.broadcasted_iota(jnp.int32, (T, 1), 0)
    no_prev = jnp.logical_or(row == 0, row == CTX_LEN)
    no_next = jnp.logical_or(row == CTX_LEN - 1, row == T - 1)
    for part, (x_ref, o_ref) in enumerate(((xq_ref, oq_ref), (xk_ref, okt_ref), (xv_ref, ov_ref))):
        for j in range(BRANCH_W // LANES):
            lo = j * LANES
            wl = part * BRANCH_W + lo
            x = x_ref[0, :, lo:lo + LANES].astype(F32)
            prev = jnp.where(no_prev, 0.0, pltpu.roll(x, 1, axis=0))
            nxt = jnp.where(no_next, 0.0, pltpu.roll(x, T - 1, axis=0))
            y = (prev * w_ref[0:1, wl:wl + LANES] + x * w_ref[1:2, wl:wl + LANES]
                 + nxt * w_ref[2:3, wl:wl + LANES])
            z = _silu(y)
            if part == 0:
                o_ref[0, :, lo:lo + LANES] = (z * (M_HD ** -0.5)).astype(o_ref.dtype)
            elif part == 1:
                o_ref[0, lo:lo + LANES, :] = z.T.astype(o_ref.dtype)
            else:
                o_ref[0, :, lo:lo + LANES] = z.astype(o_ref.dtype)


def _mlstm_conv(p_main, conv_w):
    B, T, _ = p_main.shape
    col0 = COL_MQKV // BRANCH_W
    row_major = jax.ShapeDtypeStruct((B, T, BRANCH_W), BF16)
    return pl.pallas_call(
        functools.partial(_conv_kernel, T=T),
        grid=(B,),
        in_specs=[pl.BlockSpec((1, T, BRANCH_W), lambda b, p=p: (b, 0, col0 + p)) for p in range(3)]
        + [pl.BlockSpec((3, 3 * BRANCH_W), lambda b: (0, 0))],
        out_specs=[pl.BlockSpec((1, T, BRANCH_W), lambda b: (b, 0, 0)),
                   pl.BlockSpec((1, BRANCH_W, T), lambda b: (b, 0, 0)),
                   pl.BlockSpec((1, T, BRANCH_W), lambda b: (b, 0, 0))],
        out_shape=[row_major, jax.ShapeDtypeStruct((B, BRANCH_W, T), BF16), row_major],
        compiler_params=_params("parallel"),
        name="mlstm_conv",
    )(p_main, p_main, p_main, conv_w)


def _split3(x):
    x1 = x.astype(BF16)
    r = x - x1.astype(F32)
    x2 = r.astype(BF16)
    x3 = (r - x2.astype(F32)).astype(BF16)
    return x1, x2, x3


def _mgates_kernel(g_ref, bias_ref, ct_ref, ur_ref, ar_ref, *, T):
    L = CHUNK
    n_chunks = T // L
    n_ctx = CTX_LEN // L
    gall = g_ref[0] + bias_ref[...]
    i_all = gall[0:N_CHAINS]
    f_all = _log_sigmoid(gall[N_CHAINS:2 * N_CHAINS])
    fwd_rows = lax.broadcasted_iota(jnp.int32, (N_CHAINS, L), 0) < M_HEADS
    fwd_col = lax.broadcasted_iota(jnp.int32, (N_CHAINS, 1), 0) < M_HEADS
    lane = lax.broadcasted_iota(jnp.int32, (N_CHAINS, L), 1)
    s_idx = lax.broadcasted_iota(jnp.int32, (L, L), 0)
    t_idx = lax.broadcasted_iota(jnp.int32, (L, L), 1)
    sum_upto = jnp.where(s_idx <= t_idx, 1.0, 0.0).astype(BF16)
    sum_from = jnp.where(s_idx >= t_idx, 1.0, 0.0).astype(BF16)

    def cumsum_dir(f):
        bf = None
        bb = None
        for part in _split3(f):
            a = jnp.dot(part, sum_upto, preferred_element_type=F32)
            b = jnp.dot(part, sum_from, preferred_element_type=F32)
            bf = a if bf is None else bf + a
            bb = b if bb is None else bb + b
        return jnp.where(fwd_rows, bf, bb)

    def cummax_dir(u):
        xf = u
        xb = u
        sh = 1
        while sh < L:
            xf = jnp.maximum(xf, jnp.where(lane >= sh, pltpu.roll(xf, sh, axis=1), -jnp.inf))
            xb = jnp.maximum(xb, jnp.where(lane < L - sh, pltpu.roll(xb, L - sh, axis=1), -jnp.inf))
            sh *= 2
        return jnp.where(fwd_rows, xf, xb)

    b_c, u_c, mloc_c, bl_c, g_c, gmax_c = [], [], [], [], [], []
    for c in range(n_chunks):
        f = f_all[:, c * L:(c + 1) * L]
        i = i_all[:, c * L:(c + 1) * L]
        b = cumsum_dir(f)
        u = i - b
        bl = jnp.sum(f, axis=1, keepdims=True)
        g = bl + u
        b_c.append(b)
        u_c.append(u)
        mloc_c.append(b + cummax_dir(u))
        bl_c.append(bl)
        g_c.append(g)
        gmax_c.append(jnp.max(g, axis=1, keepdims=True))

    def scan(order):
        m = jnp.zeros((N_CHAINS, 1), F32)
        prev, new = {}, {}
        for c in order:
            prev[c] = m
            m = jnp.maximum(bl_c[c] + m, gmax_c[c])
            new[c] = m
        return prev, new

    order_f = list(range(n_chunks))
    order_b = [n_ctx - 1 - j for j in range(n_ctx)] + [n_chunks + n_ctx - 1 - j for j in range(n_ctx, n_chunks)]
    prev_f, new_f = scan(order_f)
    prev_b, new_b = scan(order_b)

    for c in range(n_chunks):
        m_prev = jnp.where(fwd_col, prev_f[c], prev_b[c])
        m_new = jnp.where(fwd_col, new_f[c], new_b[c])
        b = b_c[c]
        m_inter = b + m_prev
        m_t = jnp.maximum(m_inter, mloc_c[c])
        pack = jnp.concatenate([b - m_t, jnp.exp(m_inter - m_t), jnp.exp(-m_t), jnp.exp(g_c[c] - m_new),
                                jnp.zeros((LANES - 4 * N_CHAINS, L), F32)], axis=0)
        ct_ref[0, c * L:(c + 1) * L, :] = pack.T
        ur_ref[0, :, c * L:(c + 1) * L] = u_c[c]
        ar_ref[0, c] = jnp.broadcast_to(jnp.exp(bl_c[c] + m_prev - m_new), (N_CHAINS, LANES))


def _mlstm_gates(gates_row, bias_row):
    B, _, T = gates_row.shape
    n_chunks = T // CHUNK
    return pl.pallas_call(
        functools.partial(_mgates_kernel, T=T),
        grid=(B,),
        in_specs=[pl.BlockSpec((1, 2 * N_CHAINS, T), lambda b: (b, 0, 0)),
                  pl.BlockSpec((2 * N_CHAINS, T), lambda b: (0, 0))],
        out_specs=[pl.BlockSpec((1, T, LANES), lambda b: (b, 0, 0)),
                   pl.BlockSpec((1, N_CHAINS, T), lambda b: (b, 0, 0)),
                   pl.BlockSpec((1, n_chunks, N_CHAINS, LANES), lambda b: (b, 0, 0, 0))],
        out_shape=[jax.ShapeDtypeStruct((B, T, LANES), F32),
                   jax.ShapeDtypeStruct((B, N_CHAINS, T), F32),
                   jax.ShapeDtypeStruct((B, n_chunks, N_CHAINS, LANES), F32)],
        compiler_params=_params("parallel"),
        name="mlstm_gates",
    )(gates_row, bias_row)


SCAN_BATCH = 2


def _mlstm_kernel(qf_ref, ktf_ref, vf_ref, ctf_ref, urf_ref, arf_ref,
                  qb_ref, ktb_ref, vb_ref, ctb_ref, urb_ref, arb_ref, of_ref, ob_ref, st_ref):
    L = CHUNK

    @pl.when(pl.program_id(1) == 0)
    def _():
        st_ref[...] = jnp.zeros_like(st_ref)

    t_idx = lax.broadcasted_iota(jnp.int32, (L, L), 0)
    s_idx = lax.broadcasted_iota(jnp.int32, (L, L), 1)
    ones = jnp.ones((L, M_HD), BF16)
    dirs = ((qf_ref, ktf_ref, vf_ref, ctf_ref, urf_ref, arf_ref, of_ref),
            (qb_ref, ktb_ref, vb_ref, ctb_ref, urb_ref, arb_ref, ob_ref))
    for bi in range(SCAN_BATCH):
        for d, (q_ref, kt_ref, v_ref, ct_ref, ur_ref, ar_ref, o_ref) in enumerate(dirs):
            mask = (s_idx <= t_idx) if d == 0 else (s_idx >= t_idx)
            ct = ct_ref[bi]
            ur = ur_ref[bi]
            ar = ar_ref[bi, 0]
            outs = []
            for h in range(M_HEADS):
                c = d * M_HEADS + h
                q = q_ref[bi, :, h * M_HD:(h + 1) * M_HD]
                kt = kt_ref[bi, h * M_HD:(h + 1) * M_HD, :]
                v = v_ref[bi, :, h * M_HD:(h + 1) * M_HD]
                v_ext = jnp.concatenate([v, ones], axis=1)
                cq = ct[:, c:c + 1]
                w_inter = ct[:, N_CHAINS + c:N_CHAINS + c + 1]
                e_negm = ct[:, 2 * N_CHAINS + c:2 * N_CHAINS + c + 1]
                wk = ct[:, 3 * N_CHAINS + c:3 * N_CHAINS + c + 1]
                w_intra = jnp.exp(jnp.where(mask, cq + ur[c:c + 1, :], -jnp.inf))
                s = jnp.dot(q, kt, preferred_element_type=F32) * w_intra
                st_prev = st_ref[bi, d, h]
                lhs = jnp.concatenate([s.astype(BF16), (q.astype(F32) * w_inter).astype(BF16)], axis=1)
                rhs = jnp.concatenate([v_ext, st_prev.astype(BF16)], axis=0)
                tot = jnp.dot(lhs, rhs, preferred_element_type=F32)
                den = jnp.maximum(jnp.abs(tot[:, M_HD:]), e_negm)
                outs.append(tot[:, :M_HD] / den)
                x = (v_ext.astype(F32) * wk).astype(BF16)
                a = jnp.concatenate([ar[c:c + 1, :], ar[c:c + 1, :]], axis=1)
                st_ref[bi, d, h] = a * st_prev + jnp.dot(kt, x, preferred_element_type=F32)
            o_ref[0, bi] = jnp.concatenate(outs, axis=1).astype(o_ref.dtype)


def _mlstm_scan(zq, zkt, zv, ct, ur, ar):
    B, T, _ = zq.shape
    n_chunks = T // CHUNK
    n_ctx = CTX_LEN // CHUNK

    def fwd(j):
        return j

    def bwd(j):
        return jnp.where(j < n_ctx, n_ctx - 1 - j, n_chunks + n_ctx - 1 - j)

    nb = SCAN_BATCH

    def specs(order):
        return [pl.BlockSpec((nb, CHUNK, BRANCH_W), lambda b, j: (b, order(j), 0)),
                pl.BlockSpec((nb, BRANCH_W, CHUNK), lambda b, j: (b, 0, order(j))),
                pl.BlockSpec((nb, CHUNK, BRANCH_W), lambda b, j: (b, order(j), 0)),
                pl.BlockSpec((nb, CHUNK, LANES), lambda b, j: (b, order(j), 0)),
                pl.BlockSpec((nb, N_CHAINS, CHUNK), lambda b, j: (b, 0, order(j))),
                pl.BlockSpec((nb, 1, N_CHAINS, LANES), lambda b, j: (b, order(j), 0, 0))]

    out = pl.pallas_call(
        _mlstm_kernel,
        grid=(B // nb, n_chunks),
        in_specs=specs(fwd) + specs(bwd),
        out_specs=[pl.BlockSpec((1, nb, CHUNK, BRANCH_W), lambda b, j: (0, b, fwd(j), 0)),
                   pl.BlockSpec((1, nb, CHUNK, BRANCH_W), lambda b, j: (0, b, bwd(j), 0))],
        out_shape=[jax.ShapeDtypeStruct((1, B, T, BRANCH_W), F32),
                   jax.ShapeDtypeStruct((1, B, T, BRANCH_W), F32)],
        scratch_shapes=[pltpu.VMEM((nb, 2, M_HEADS, M_HD, 2 * M_HD), F32)],
        compiler_params=_params("parallel", "arbitrary"),
        name="mlstm_scan",
    )(zq, zkt, zv, ct, ur, ar, zq, zkt, zv, ct, ur, ar)
    return out[0][0], out[1][0]


def _merge_kernel(*refs, nt, mod_row):
    groups = [refs[i * nt:(i + 1) * nt] for i in range(6)]
    ya_refs, yw_refs, yg_refs, hf_refs, hb_refs, h_refs = groups
    mod_ref, wg_ref, wbr_ref, wo_ref, gpost_ref, mnorm_ref, o_ref = refs[6 * nt:]
    tm = CTX_LEN
    mult = mod_ref[0, mod_row:mod_row + 1, :]
    shift = mod_ref[0, mod_row + 1:mod_row + 2, :]
    gate = mod_ref[0, mod_row + 2:mod_row + 3, :]

    def rows(tile_refs):
        return jnp.concatenate([r[0] for r in tile_refs], axis=0)

    h = rows(h_refs)
    ya, yw, yg = rows(ya_refs), rows(yw_refs), rows(yg_refs)
    xn = _norm_mod(h, mult, shift)

    def proj(col, width):
        return jnp.dot(xn, wg_ref[:, col:col + width], preferred_element_type=F32)

    hm = rows(hf_refs) + rows(hb_refs)
    parts = []
    for hd in range(M_HEADS):
        x = hm[:, hd * M_HD:(hd + 1) * M_HD]
        ms = jnp.mean(x * x, axis=-1, keepdims=True)
        parts.append(x * lax.rsqrt(ms + EPS) * mnorm_ref[:, hd * M_HD:(hd + 1) * M_HD])
    ym = _sigmoid(proj(GCOL_MO, BRANCH_W)) * jnp.concatenate(parts, axis=1)
    ys = (ya.astype(F32), yw.astype(F32), yg.astype(F32), ym)
    acc = None
    for k in range(N_BRANCH):
        yk = (ys[k] * _silu(proj(GCOL_BRANCH_GATE[k], BRANCH_W))).astype(BF16)
        bp = jnp.dot(yk, wbr_ref[k], preferred_element_type=F32)
        g = _sigmoid(proj(GCOL_MERGE + k * D_MODEL, D_MODEL))
        acc = g * bp if acc is None else acc + g * bp
    y = jnp.dot(acc.astype(BF16), wo_ref[...], preferred_element_type=F32)
    ms = jnp.mean(y * y, axis=-1, keepdims=True)
    yn = y * lax.rsqrt(ms + EPS) * gpost_ref[...]
    o_ref[0] = h + gate * yn


MERGE_VMEM_LIMIT = 58 * 1024 * 1024


def _merge(ya, yw, yg, hf, hb, h, mod8, wg_bf, wbr_bf, wo_bf, g_post, m_norm, *, context):
    tm = CTX_LEN
    B, rows, _ = h.shape
    nt = 1 if context else 2
    n_steps = rows // (nt * tm)
    stream_off = 0 if context else CTX_LEN // tm
    bw = BRANCH_W

    def tiles(w, off):
        return [pl.BlockSpec((1, tm, w), lambda b, t, i=i: (b, nt * t + i + off, 0)) for i in range(nt)]

    def const(shape):
        nd = len(shape)
        return pl.BlockSpec(shape, lambda b, t: (0,) * nd)

    token_specs = (tiles(bw, 0) + tiles(bw, 0) + tiles(bw, 0) + tiles(bw, stream_off) + tiles(bw, stream_off)
                   + tiles(D_MODEL, 0))
    token_args = [ya] * nt + [yw] * nt + [yg] * nt + [hf] * nt + [hb] * nt + [h] * nt
    return pl.pallas_call(
        functools.partial(_merge_kernel, nt=nt, mod_row=0 if context else 3),
        grid=(B, n_steps),
        in_specs=token_specs + [
            pl.BlockSpec((1, MOD_ROWS, D_MODEL), lambda b, t: (b, 0, 0)),
            const((D_MODEL, N_GATE)), const((N_BRANCH, bw, D_MODEL)), const((D_MODEL, D_MODEL)),
            const((1, D_MODEL)), const((1, bw))],
        out_specs=pl.BlockSpec((1, nt * tm, D_MODEL), lambda b, t: (b, t, 0)),
        out_shape=jax.ShapeDtypeStruct((B, rows, D_MODEL), F32),
        compiler_params=pltpu.CompilerParams(dimension_semantics=("parallel", "parallel"),
                                             vmem_limit_bytes=MERGE_VMEM_LIMIT),
        name="merge_ctx" if context else "merge",
    )(*token_args, mod8, wg_bf, wbr_bf, wo_bf, g_post, m_norm)


def _rope_tables(n_lat):
    rows = n_lat // GRID_W
    row = jnp.repeat(jnp.arange(rows), GRID_W).astype(F32)
    col = jnp.tile(jnp.arange(GRID_W), rows).astype(F32)
    n_freq = HEAD_DIM // 4
    inv = ROPE_THETA ** (-jnp.arange(n_freq, dtype=F32) / n_freq)
    ang = jnp.concatenate([row[:, None] * inv, col[:, None] * inv], axis=-1)
    cos, sin = jnp.cos(ang), jnp.sin(ang)
    cos_h = jnp.concatenate([cos, cos], axis=-1)
    sin_h = jnp.concatenate([-sin, sin], axis=-1)
    cos_t = jnp.concatenate([jnp.ones((CTX_LEN, HEAD_DIM), F32), cos_h], axis=0)
    sin_t = jnp.concatenate([jnp.zeros((CTX_LEN, HEAD_DIM), F32), sin_h], axis=0)
    return jnp.tile(cos_t, (1, LANES // HEAD_DIM)), jnp.tile(sin_t, (1, LANES // HEAD_DIM))


def _gather_cols(w, names):
    sl = _ref_slices()
    runs = []
    for n in names:
        start, size = sl[n]
        if runs and runs[-1][1] == start:
            runs[-1][1] = start + size
        else:
            runs.append([start, start + size])
    return jnp.concatenate([w[:, a:b].astype(BF16) for a, b in runs], axis=1)


def _split_w_in(w):
    lstm_gates = jnp.pad(_gather_cols(w, ('m_i', 'm_f')), ((0, 0), (0, LANES - 2 * N_CHAINS)))
    return _gather_cols(w, _MAIN_ORDER), _gather_cols(w, _GATE_ORDER), lstm_gates


def kernel(x, c, ctx, c_ctx, w_mod, b_mod, g_pre, g_post, w_in, a_q_norm, a_k_norm, w_sink, sg_ln_g, sg_ln_b,
           sg_w, sg_b, m_conv, m_b_i, m_b_f, m_norm, w_branch, w_out):
    B, n_lat, D = x.shape
    depth = w_mod.shape[0]
    assert depth == 2, "the context stream is only carried from the first layer to the last"
    T = CTX_LEN + n_lat
    n_ctx_chunks = CTX_LEN // CHUNK
    n_lat_chunks = n_lat // CHUNK
    rope = _rope_tables(n_lat)
    gmat = (jnp.arange(LANES)[:, None] // HEAD_DIM == jnp.arange(LANES)[None, :] // HEAD_DIM).astype(BF16)

    hc, hx = ctx, x
    mod_rows = 16
    cc = jnp.zeros((mod_rows, D), F32).at[:B].set(c).at[B].set(c_ctx)
    tm_tok = CTX_LEN
    n_ctx_tiles = CTX_LEN // tm_tok
    n_lat_tiles = n_lat // tm_tok

    for l in range(depth):
        need_ctx = l < depth - 1
        mod = _modulation(cc, w_mod[l].astype(BF16), b_mod[l][None, :])
        shift, scale, gate = mod[:, :D], mod[:, D:2 * D], mod[:, 2 * D:]
        mult = g_pre[l][None, :] * (1.0 + scale)
        zero = jnp.zeros((B, D), F32)
        mod8 = jnp.stack([jnp.broadcast_to(mult[B], (B, D)), jnp.broadcast_to(shift[B], (B, D)),
                          jnp.broadcast_to(gate[B], (B, D)), mult[:B], shift[:B], gate[:B], zero, zero],
                         axis=1)

        w_main, w_gate, w_lstm = _split_w_in(w_in[l])
        p_main, p_gates = _in_projection(hc, hx, mod8, w_main, w_lstm, tn=1024)

        qn = jnp.tile(a_q_norm[l], LANES // HEAD_DIM)[None, :]
        kn = jnp.tile(a_k_norm[l], LANES // HEAD_DIM)[None, :]
        ya = _global_attention(p_main, rope, qn, kn, gmat,
                               q_tile0=n_ctx_chunks, n_q_tiles=n_lat_chunks, nk=T)
        yw = _window_attention(p_main, rope, w_sink[l], q_tile0=n_ctx_chunks, n_q_tiles=n_lat_chunks,
                               nk=T, banded=True)
        if need_ctx:
            ya_c = _global_attention(p_main, rope, qn, kn, gmat, q_tile0=0, n_q_tiles=n_ctx_chunks, nk=CTX_LEN)
            yw_c = _window_attention(p_main, rope, w_sink[l], q_tile0=0, n_q_tiles=n_ctx_chunks,
                                     nk=CTX_LEN, banded=False)

        sg_args = (p_main, sg_ln_g[l][None, :], sg_ln_b[l][None, :], sg_w[l].astype(BF16), sg_b[l].T)
        yg = _spatial_gating(*sg_args, tile0=n_ctx_tiles, n_tiles=n_lat_tiles, nt=4)
        if need_ctx:
            yg_c = _spatial_gating(*sg_args, tile0=0, n_tiles=n_ctx_tiles, nt=n_ctx_tiles)

        zq, zkt, zv = _mlstm_conv(p_main, m_conv[l])
        gates_row = jnp.swapaxes(p_gates[:, :, :2 * N_CHAINS], 1, 2)
        bias_row = jnp.broadcast_to(
            jnp.concatenate([m_b_i[l].reshape(-1), m_b_f[l].reshape(-1)])[:, None], (2 * N_CHAINS, T))
        ct, ur, ar = _mlstm_gates(gates_row, bias_row)
        hf, hb = _mlstm_scan(zq, zkt, zv, ct, ur, ar)

        weights = (mod8, w_gate, w_branch[l].astype(BF16), w_out[l].astype(BF16),
                   g_post[l][None, :], m_norm[l][None, :])
        hx_new = _merge(ya, yw, yg, hf, hb, hx, *weights, context=False)
        if need_ctx:
            hc = _merge(ya_c, yw_c, yg_c, hf, hb, hc, *weights, context=True)
        hx = hx_new
    return hx
```

```python
import functools

import jax
import jax.numpy as jnp
from jax import lax
from jax.experimental import pallas as pl
from jax.experimental.pallas import tpu as pltpu

F32 = jnp.float32
BF16 = jnp.bfloat16

D_MODEL = 1024
GRID_W = 64
CTX_LEN = 256
N_BRANCH = 4
BRANCH_W = 512
HEAD_DIM = 64
N_HEADS = 8
N_KV = 2
GROUP = N_HEADS // N_KV
WINDOW = 128
CHUNK = 128
B_GROUPS = 4
M_HEADS = 4
M_HD = 128
N_CHAINS = 2 * M_HEADS
ROPE_THETA = 10000.0
EPS = 1e-6
LOG2E = 1.4426950408889634
ATTN_TILES_PER_STEP = 2

LANES = 128
VMEM_LIMIT = 48 * 1024 * 1024
BIG_VMEM_LIMIT = 58 * 1024 * 1024

COL_AQ = 0
COL_WQ = 512
COL_AK = 1024
COL_AV = 1152
COL_WK = 1280
COL_WV = 1408
COL_GU = 1536
COL_GV = 2048
COL_MQKV = 2560
N_MAIN = 4096
GCOL_BRANCH_GATE = (0, 512, 1024, 2048)
GCOL_MO = 1536
GCOL_MERGE = 2560
N_GATE = GCOL_MERGE + N_BRANCH * D_MODEL

_REF_LAYOUT = (
    ('a_q', 512), ('a_k', 128), ('a_v', 128), ('a_gate', 512),
    ('w_q', 512), ('w_k', 128), ('w_v', 128), ('w_gate', 512),
    ('g_u', 512), ('g_v', 512), ('g_gate', 512),
    ('m_qkv', 1536), ('m_i', 8), ('m_f', 8), ('m_o', 512), ('m_gate', 512),
    ('merge', 4096),
)
_MAIN_ORDER = ('a_q', 'w_q', 'a_k', 'a_v', 'w_k', 'w_v', 'g_u', 'g_v', 'm_qkv')
_GATE_ORDER = ('a_gate', 'w_gate', 'g_gate', 'm_o', 'm_gate', 'merge')


def _ref_slices():
    out, start = {}, 0
    for name, size in _REF_LAYOUT:
        out[name] = (start, size)
        start += size
    return out


def _params(*sem):
    return pltpu.CompilerParams(dimension_semantics=sem, vmem_limit_bytes=VMEM_LIMIT)


def _sigmoid(x):
    return 1.0 / (1.0 + jnp.exp2(x * (-LOG2E)))


def _silu(x):
    return x * _sigmoid(x)


def _gelu_tanh(x):
    c = 0.7978845608028654
    return 0.5 * x * (1.0 + jnp.tanh(c * (x + 0.044715 * (x * x * x))))


def _log_sigmoid(x):
    return -(jnp.maximum(-x, 0.0) + jnp.log1p(jnp.exp(-jnp.abs(x))))


def _mod_kernel(c_ref, w_ref, b_ref, o_ref):
    a = _silu(c_ref[...]).astype(BF16)
    o_ref[...] = jnp.dot(a, w_ref[...], preferred_element_type=F32) + b_ref[...]


def _modulation(cc, w_mod_bf, b_mod):
    rows = cc.shape[0]
    n = w_mod_bf.shape[1]
    tn = 1024
    return pl.pallas_call(
        _mod_kernel,
        grid=(n // tn,),
        in_specs=[pl.BlockSpec((rows, D_MODEL), lambda j: (0, 0)),
                  pl.BlockSpec((D_MODEL, tn), lambda j: (0, j)),
                  pl.BlockSpec((1, tn), lambda j: (0, j))],
        out_specs=pl.BlockSpec((rows, tn), lambda j: (0, j)),
        out_shape=jax.ShapeDtypeStruct((rows, n), F32),
        compiler_params=_params("parallel"),
        name="modulation",
    )(cc, w_mod_bf, b_mod)


MOD_ROWS = 8


def _norm_mod(x, mult, shift):
    ms = jnp.mean(x * x, axis=-1, keepdims=True)
    return (x * lax.rsqrt(ms + EPS) * mult + shift).astype(BF16)


W_TILE = 256
TILES_PER_STEP = 4


def _main_tiles():
    sl = _ref_slices()
    tiles = []
    for name in _MAIN_ORDER:
        start, size = sl[name]
        if size < W_TILE:
            if start % W_TILE == 0:
                tiles.append(start // W_TILE)
            continue
        assert start % W_TILE == 0 and size % W_TILE == 0
        tiles.extend(range(start // W_TILE, (start + size) // W_TILE))
    return tiles


def _inproj_kernel(tbl_ref, hc_ref, hx_ref, mod_ref, *refs, n_lat):
    del tbl_ref
    w_refs = refs[:TILES_PER_STEP]
    wl_ref, o_ref, og_ref, xn_ref = refs[TILES_PER_STEP:]
    rb = CTX_LEN

    @pl.when(pl.program_id(1) == 0)
    def _():
        xn_ref[0:rb, :] = _norm_mod(hc_ref[0], mod_ref[0, 0:1, :], mod_ref[0, 1:2, :])

        def norm_rows(r, carry):
            off = pl.multiple_of(r * rb, rb)
            x = hx_ref[0, pl.ds(off, rb), :]
            xn_ref[pl.ds(rb + off, rb), :] = _norm_mod(x, mod_ref[0, 3:4, :], mod_ref[0, 4:5, :])
            return carry

        lax.fori_loop(0, n_lat // rb, norm_rows, 0)
        og_ref[0] = jnp.dot(xn_ref[...], wl_ref[:, 0:LANES].astype(BF16), preferred_element_type=F32)

    w = jnp.concatenate([r[...].astype(BF16) for r in w_refs], axis=1)
    o_ref[0] = jnp.dot(xn_ref[...], w, preferred_element_type=F32).astype(o_ref.dtype)


def _in_projection(hc, hx, mod8, w_in_l):
    B, n_lat, _ = hx.shape
    T = CTX_LEN + n_lat
    tiles = _main_tiles()
    assert len(tiles) * W_TILE == N_MAIN and len(tiles) % TILES_PER_STEP == 0
    table = jnp.asarray(tiles, jnp.int32)
    lstm_tile = _ref_slices()['m_i'][0] // W_TILE
    assert _ref_slices()['m_i'][0] % W_TILE == 0
    tn = TILES_PER_STEP * W_TILE
    h_specs = [pl.BlockSpec((1, a.shape[1], D_MODEL), lambda b, j, tbl: (b, 0, 0)) for a in (hc, hx)]
    w_specs = [pl.BlockSpec((D_MODEL, W_TILE), lambda b, j, tbl, i=i: (0, tbl[TILES_PER_STEP * j + i]))
               for i in range(TILES_PER_STEP)]
    grid_spec = pltpu.PrefetchScalarGridSpec(
        num_scalar_prefetch=1,
        grid=(B, len(tiles) // TILES_PER_STEP),
        in_specs=h_specs + [pl.BlockSpec((1, MOD_ROWS, D_MODEL), lambda b, j, tbl: (b, 0, 0))] + w_specs
        + [pl.BlockSpec((D_MODEL, W_TILE), lambda b, j, tbl: (0, lstm_tile))],
        out_specs=[pl.BlockSpec((1, T, tn), lambda b, j, tbl: (b, 0, j)),
                   pl.BlockSpec((1, T, LANES), lambda b, j, tbl: (b, 0, 0))],
        scratch_shapes=[pltpu.VMEM((T, D_MODEL), BF16)])
    return pl.pallas_call(
        functools.partial(_inproj_kernel, n_lat=n_lat),
        grid_spec=grid_spec,
        out_shape=[jax.ShapeDtypeStruct((B, T, N_MAIN), BF16),
                   jax.ShapeDtypeStruct((B, T, LANES), F32)],
        compiler_params=pltpu.CompilerParams(dimension_semantics=("parallel", "arbitrary"),
                                             vmem_limit_bytes=BIG_VMEM_LIMIT),
        name="in_projection",
    )(table, hc, hx, mod8, *([w_in_l] * (TILES_PER_STEP + 1)))


def _rope_slab(x, cos, sin):
    lane = lax.broadcasted_iota(jnp.int32, x.shape, 1)
    first_half = (lane & (HEAD_DIM - 1)) < (HEAD_DIM // 2)
    partner = jnp.where(first_half,
                        pltpu.roll(x, LANES - HEAD_DIM // 2, axis=1),
                        pltpu.roll(x, HEAD_DIM // 2, axis=1))
    return x * cos + partner * sin


def _head_norm_slab(x, gain, gmat):
    ss = jnp.dot((x * x).astype(BF16), gmat, preferred_element_type=F32)
    return x * lax.rsqrt(ss * (1.0 / HEAD_DIM) + EPS) * gain


def _prep_kv(k_ref, v_ref, cosk_ref, sink_ref, kn_ref, gmat_ref, kp_ref, vt_ref, nk, use_norm):
    k = k_ref[0, 0:nk, :].astype(F32)
    if use_norm:
        k = _head_norm_slab(k, kn_ref[...], gmat_ref[...])
    k = _rope_slab(k, cosk_ref[0:nk, :], sink_ref[0:nk, :]).astype(BF16)
    for g in range(N_KV):
        kp_ref[g, 0:nk, :] = k[:, g * HEAD_DIM:(g + 1) * HEAD_DIM]
    for blk in range(nk // CHUNK):
        v = v_ref[0, blk * CHUNK:(blk + 1) * CHUNK, :].astype(F32)
        vt_ref[blk] = v.T.astype(BF16)


def _prep_qt(q_ref, cosq_ref, sinq_ref, qn_ref, gmat_ref, use_norm, r0, tq):
    cos = cosq_ref[r0:r0 + tq, :]
    sin = sinq_ref[r0:r0 + tq, :]
    xts = []
    for s in range(N_HEADS // 2):
        x = q_ref[0, r0:r0 + tq, s * LANES:(s + 1) * LANES].astype(F32)
        if use_norm:
            x = _head_norm_slab(x, qn_ref[...], gmat_ref[...])
        x = _rope_slab(x, cos, sin) * (HEAD_DIM ** -0.5 * LOG2E)
        xts.append(x.T.astype(BF16))
    per = GROUP // 2
    out = []
    for g in range(N_KV):
        parts = []
        for s in range(g * per, (g + 1) * per):
            parts.append(xts[s][:HEAD_DIM, :])
            parts.append(xts[s][HEAD_DIM:, :])
        out.append(jnp.concatenate(parts, axis=1))
    return out


def _heads_from_t(ot, tq):
    slabs = []
    for j in range(GROUP // 2):
        two = jnp.concatenate([ot[:, (2 * j) * tq:(2 * j + 1) * tq],
                               ot[:, (2 * j + 1) * tq:(2 * j + 2) * tq]], axis=0)
        slabs.append(two.T)
    return jnp.concatenate(slabs, axis=1)


def _gattn_kernel(q_ref, k_ref, v_ref, cosq_ref, sinq_ref, cosk_ref, sink_ref, qn_ref, kn_ref, gmat_ref,
                  o_ref, kp_ref, vt_ref, s_ref, *, tq, nsub, nk, ck):
    i = pl.program_id(1)

    @pl.when(i == 0)
    def _():
        _prep_kv(k_ref, v_ref, cosk_ref, sink_ref, kn_ref, gmat_ref, kp_ref, vt_ref, nk, True)

    qts = [_prep_qt(q_ref, cosq_ref, sinq_ref, qn_ref, gmat_ref, True, sub * tq, tq) for sub in range(nsub)]
    items = [(sub, g) for sub in range(nsub) for g in range(N_KV)]
    n_chunks = nk // ck
    blocks_per_chunk = ck // CHUNK
    gw = GROUP * HEAD_DIM

    def score_chunk(k, c, m):
        sub, g = items[k]
        st = jnp.dot(kp_ref[g, c * ck:(c + 1) * ck, :], qts[sub][g], preferred_element_type=F32)
        s_ref[k, c * ck:(c + 1) * ck, :] = st
        cm = jnp.max(st, axis=0, keepdims=True)
        return cm if m is None else jnp.maximum(m, cm)

    def value_chunk(k, c, m, l, acc):
        _, g = items[k]
        p = jnp.exp2(s_ref[k, c * ck:(c + 1) * ck, :] - m)
        cl = jnp.sum(p, axis=0, keepdims=True)
        vt = jnp.concatenate(
            [vt_ref[c * blocks_per_chunk + j, g * HEAD_DIM:(g + 1) * HEAD_DIM, :]
             for j in range(blocks_per_chunk)], axis=1)
        pv = jnp.dot(vt, p.astype(BF16), preferred_element_type=F32)
        return (cl, pv) if l is None else (l + cl, acc + pv)

    m_next = None
    for c in range(n_chunks):
        m_next = score_chunk(0, c, m_next)
    for k, (sub, g) in enumerate(items):
        m, m_next = m_next, None
        l = acc = None
        for c in range(n_chunks):
            l, acc = value_chunk(k, c, m, l, acc)
            if k + 1 < len(items):
                m_next = score_chunk(k + 1, c, m_next)
        o_ref[0, sub * tq:(sub + 1) * tq, g * gw:(g + 1) * gw] = _heads_from_t(acc / l, tq).astype(o_ref.dtype)


def _global_attention(p_main, rope, qn, kn, gmat, *, q_tile0, n_q_tiles, nk):
    B, T, _ = p_main.shape
    cos, sin = rope
    tq = CHUNK
    nsub = ATTN_TILES_PER_STEP
    tb = nsub * tq
    blk0 = q_tile0 // nsub
    ck = 256
    kernel = functools.partial(_gattn_kernel, tq=tq, nsub=nsub, nk=nk, ck=ck)
    qcol = COL_AQ // BRANCH_W
    return pl.pallas_call(
        kernel,
        grid=(B, n_q_tiles // nsub),
        in_specs=[pl.BlockSpec((1, tb, BRANCH_W), lambda b, i: (b, i + blk0, qcol)),
                  pl.BlockSpec((1, T, LANES), lambda b, i: (b, 0, COL_AK // LANES)),
                  pl.BlockSpec((1, T, LANES), lambda b, i: (b, 0, COL_AV // LANES)),
                  pl.BlockSpec((tb, LANES), lambda b, i: (i + blk0, 0)),
                  pl.BlockSpec((tb, LANES), lambda b, i: (i + blk0, 0)),
                  pl.BlockSpec((T, LANES), lambda b, i: (0, 0)),
                  pl.BlockSpec((T, LANES), lambda b, i: (0, 0)),
                  pl.BlockSpec((1, LANES), lambda b, i: (0, 0)),
                  pl.BlockSpec((1, LANES), lambda b, i: (0, 0)),
                  pl.BlockSpec((LANES, LANES), lambda b, i: (0, 0))],
        out_specs=pl.BlockSpec((1, tb, BRANCH_W), lambda b, i: (b, i, 0)),
        out_shape=jax.ShapeDtypeStruct((B, n_q_tiles * tq, BRANCH_W), BF16),
        scratch_shapes=[pltpu.VMEM((N_KV, nk, HEAD_DIM), BF16),
                        pltpu.VMEM((nk // CHUNK, LANES, CHUNK), BF16),
                        pltpu.VMEM((nsub * N_KV, nk, GROUP * tq), F32)],
        compiler_params=_params("parallel", "arbitrary"),
        name="global_attention",
    )(p_main, p_main, p_main, cos, sin, cos, sin, qn, kn, gmat)


def _wattn_kernel(wsink_ref, q_ref, k_ref, v_ref, cosq_ref, sinq_ref, cosk_ref, sink_ref,
                  o_ref, kp_ref, vt_ref, *, tq, nsub, nk, banded, n_lat_blocks):
    step = pl.program_id(1)

    @pl.when(step == 0)
    def _():
        _prep_kv(k_ref, v_ref, cosk_ref, sink_ref, None, None, kp_ref, vt_ref, nk, False)

    cols = GROUP * tq
    gw = GROUP * HEAD_DIM
    n_ctx_chunks = CTX_LEN // CHUNK
    lane = lax.broadcasted_iota(jnp.int32, (1, cols), 1)
    sink_rows = []
    for g in range(N_KV):
        sink_row = jnp.zeros((1, cols), F32)
        for h in range(GROUP):
            in_head = jnp.logical_and(lane >= h * tq, lane < (h + 1) * tq)
            sink_row = jnp.where(in_head, wsink_ref[g * GROUP + h] * LOG2E, sink_row)
        sink_rows.append(sink_row)
    if banded:
        kk = lax.broadcasted_iota(jnp.int32, (CHUNK, cols), 0)
        qq = lax.broadcasted_iota(jnp.int32, (CHUNK, cols), 1) & (tq - 1)

    for sub in range(nsub):
        i = step * nsub + sub
        qts = _prep_qt(q_ref, cosq_ref, sinq_ref, None, None, False, sub * tq, tq)
        if banded:
            prev_blk = i + n_ctx_chunks - 1
            own_blk = i + n_ctx_chunks
            next_blk = jnp.minimum(i + n_ctx_chunks + 1, n_ctx_chunks + n_lat_blocks - 1)
            ok_prev = jnp.logical_and(kk >= qq, i > 0)
            ok_next = jnp.logical_and(kk <= qq, i < n_lat_blocks - 1)

        for g in range(N_KV):
            qt = qts[g]
            sink_row = sink_rows[g]

            def scores(blk):
                off = blk * CHUNK
                if not isinstance(off, int):
                    off = pl.multiple_of(off, CHUNK)
                return jnp.dot(kp_ref[g, pl.ds(off, CHUNK), :], qt, preferred_element_type=F32)

            blocks = [(scores(cb), cb) for cb in range(n_ctx_chunks)]
            if banded:
                blocks.append((jnp.where(ok_prev, scores(prev_blk), -jnp.inf), prev_blk))
                blocks.append((scores(own_blk), own_blk))
                blocks.append((jnp.where(ok_next, scores(next_blk), -jnp.inf), next_blk))
            m = sink_row
            for st, _ in blocks:
                m = jnp.maximum(m, jnp.max(st, axis=0, keepdims=True))
            l = jnp.exp2(sink_row - m)
            acc = None
            for st, blk in blocks:
                p = jnp.exp2(st - m)
                l = l + jnp.sum(p, axis=0, keepdims=True)
                pv = jnp.dot(vt_ref[blk, g * HEAD_DIM:(g + 1) * HEAD_DIM, :], p.astype(BF16),
                             preferred_element_type=F32)
                acc = pv if acc is None else acc + pv
            o_ref[0, sub * tq:(sub + 1) * tq, g * gw:(g + 1) * gw] = (
                _heads_from_t(acc / l, tq).astype(o_ref.dtype))


def _window_attention(p_main, rope, w_sink, *, q_tile0, n_q_tiles, nk, banded):
    B, T, _ = p_main.shape
    cos, sin = rope
    tq = CHUNK
    nsub = ATTN_TILES_PER_STEP
    tb = nsub * tq
    blk0 = q_tile0 // nsub
    n_lat_blocks = (T - CTX_LEN) // CHUNK
    kernel = functools.partial(_wattn_kernel, tq=tq, nsub=nsub, nk=nk, banded=banded,
                               n_lat_blocks=n_lat_blocks)
    qcol = COL_WQ // BRANCH_W
    return pl.pallas_call(
        kernel,
        grid=(B, n_q_tiles // nsub),
        in_specs=[pl.BlockSpec(memory_space=pltpu.SMEM),
                  pl.BlockSpec((1, tb, BRANCH_W), lambda b, i: (b, i + blk0, qcol)),
                  pl.BlockSpec((1, T, LANES), lambda b, i: (b, 0, COL_WK // LANES)),
                  pl.BlockSpec((1, T, LANES), lambda b, i: (b, 0, COL_WV // LANES)),
                  pl.BlockSpec((tb, LANES), lambda b, i: (i + blk0, 0)),
                  pl.BlockSpec((tb, LANES), lambda b, i: (i + blk0, 0)),
                  pl.BlockSpec((T, LANES), lambda b, i: (0, 0)),
                  pl.BlockSpec((T, LANES), lambda b, i: (0, 0))],
        out_specs=pl.BlockSpec((1, tb, BRANCH_W), lambda b, i: (b, i, 0)),
        out_shape=jax.ShapeDtypeStruct((B, n_q_tiles * tq, BRANCH_W), BF16),
        scratch_shapes=[pltpu.VMEM((N_KV, nk, HEAD_DIM), BF16),
                        pltpu.VMEM((nk // CHUNK, LANES, CHUNK), BF16)],
        compiler_params=_params("parallel", "arbitrary"),
        name="window_attention",
    )(w_sink, p_main, p_main, p_main, cos, sin, cos, sin)


def _sgate_kernel(*refs, nt, tm):
    u_refs, v_refs = refs[:nt], refs[nt:2 * nt]
    lng_ref, lnb_ref, ws_ref, bs_ref, o_ref = refs[2 * nt:]
    gw = BRANCH_W // B_GROUPS
    for i in range(nt):
        u = _gelu_tanh(u_refs[i][0].astype(F32))
        v = _gelu_tanh(v_refs[i][0].astype(F32))
        mu = jnp.mean(v, axis=-1, keepdims=True)
        vc = v - mu
        var = jnp.mean(vc * vc, axis=-1, keepdims=True)
        vn = (vc * lax.rsqrt(var + EPS) * lng_ref[...] + lnb_ref[...]).astype(BF16)
        chunks = []
        for c in range(tm // CHUNK):
            cols = []
            for g in range(B_GROUPS):
                mixed = jnp.dot(ws_ref[g], vn[c * CHUNK:(c + 1) * CHUNK, g * gw:(g + 1) * gw],
                                preferred_element_type=F32)
                cols.append(mixed + bs_ref[:, g:g + 1])
            chunks.append(jnp.concatenate(cols, axis=1))
        o_ref[0, i * tm:(i + 1) * tm, :] = (u * jnp.concatenate(chunks, axis=0)).astype(o_ref.dtype)


def _spatial_gating(p_main, ln_g, ln_b, ws_bf, bs_t, *, tile0, n_tiles, nt):
    B, T, _ = p_main.shape
    tm = CTX_LEN

    def tiles(col):
        return [pl.BlockSpec((1, tm, BRANCH_W), lambda b, t, i=i: (b, nt * t + i + tile0, col))
                for i in range(nt)]

    return pl.pallas_call(
        functools.partial(_sgate_kernel, nt=nt, tm=tm),
        grid=(B, n_tiles // nt),
        in_specs=tiles(COL_GU // BRANCH_W) + tiles(COL_GV // BRANCH_W) + [
            pl.BlockSpec((1, BRANCH_W), lambda b, t: (0, 0)),
            pl.BlockSpec((1, BRANCH_W), lambda b, t: (0, 0)),
            pl.BlockSpec((B_GROUPS, CHUNK, CHUNK), lambda b, t: (0, 0, 0)),
            pl.BlockSpec((CHUNK, B_GROUPS), lambda b, t: (0, 0))],
        out_specs=pl.BlockSpec((1, nt * tm, BRANCH_W), lambda b, t: (b, t, 0)),
        out_shape=jax.ShapeDtypeStruct((B, n_tiles * tm, BRANCH_W), BF16),
        compiler_params=_params("parallel", "parallel"),
        name="spatial_gating",
    )(*([p_main] * (2 * nt)), ln_g, ln_b, ws_bf, bs_t)


def _conv_kernel(xq_ref, xk_ref, xv_ref, w_ref, oq_ref, okt_ref, ov_ref, *, T):
    row = lax.broadcasted_iota(jnp.int32, (T, 1), 0)
    no_prev = jnp.logical_or(row == 0, row == CTX_LEN)
    no_next = jnp.logical_or(row == CTX_LEN - 1, row == T - 1)
    for part, (x_ref, o_ref) in enumerate(((xq_ref, oq_ref), (xk_ref, okt_ref), (xv_ref, ov_ref))):
        for j in range(BRANCH_W // LANES):
            lo = j * LANES
            wl = part * BRANCH_W + lo
            x = x_ref[0, :, lo:lo + LANES].astype(F32)
            prev = jnp.where(no_prev, 0.0, pltpu.roll(x, 1, axis=0))
            nxt = jnp.where(no_next, 0.0, pltpu.roll(x, T - 1, axis=0))
            y = (prev * w_ref[0:1, wl:wl + LANES] + x * w_ref[1:2, wl:wl + LANES]
                 + nxt * w_ref[2:3, wl:wl + LANES])
            z = _silu(y)
            if part == 0:
                o_ref[0, :, lo:lo + LANES] = (z * (M_HD ** -0.5)).astype(o_ref.dtype)
            elif part == 1:
                o_ref[0, lo:lo + LANES, :] = z.T.astype(o_ref.dtype)
            else:
                o_ref[0, :, lo:lo + LANES] = z.astype(o_ref.dtype)


def _mlstm_conv(p_main, conv_w):
    B, T, _ = p_main.shape
    col0 = COL_MQKV // BRANCH_W
    row_major = jax.ShapeDtypeStruct((B, T, BRANCH_W), BF16)
    return pl.pallas_call(
        functools.partial(_conv_kernel, T=T),
        grid=(B,),
        in_specs=[pl.BlockSpec((1, T, BRANCH_W), lambda b, p=p: (b, 0, col0 + p)) for p in range(3)]
        + [pl.BlockSpec((3, 3 * BRANCH_W), lambda b: (0, 0))],
        out_specs=[pl.BlockSpec((1, T, BRANCH_W), lambda b: (b, 0, 0)),
                   pl.BlockSpec((1, BRANCH_W, T), lambda b: (b, 0, 0)),
                   pl.BlockSpec((1, T, BRANCH_W), lambda b: (b, 0, 0))],
        out_shape=[row_major, jax.ShapeDtypeStruct((B, BRANCH_W, T), BF16), row_major],
        compiler_params=_params("parallel"),
        name="mlstm_conv",
    )(p_main, p_main, p_main, conv_w)


def _split3(x):
    x1 = x.astype(BF16)
    r = x - x1.astype(F32)
    x2 = r.astype(BF16)
    x3 = (r - x2.astype(F32)).astype(BF16)
    return x1, x2, x3


def _mgates_kernel(g_ref, bias_ref, ct_ref, ur_ref, ar_ref, *, T):
    L = CHUNK
    n_chunks = T // L
    n_ctx = CTX_LEN // L
    gall = g_ref[0] + bias_ref[...]
    i_all = gall[0:N_CHAINS]
    f_all = _log_sigmoid(gall[N_CHAINS:2 * N_CHAINS])
    fwd_rows = lax.broadcasted_iota(jnp.int32, (N_CHAINS, L), 0) < M_HEADS
    fwd_col = lax.broadcasted_iota(jnp.int32, (N_CHAINS, 1), 0) < M_HEADS
    lane = lax.broadcasted_iota(jnp.int32, (N_CHAINS, L), 1)
    s_idx = lax.broadcasted_iota(jnp.int32, (L, L), 0)
    t_idx = lax.broadcasted_iota(jnp.int32, (L, L), 1)
    sum_upto = jnp.where(s_idx <= t_idx, 1.0, 0.0).astype(BF16)
    sum_from = jnp.where(s_idx >= t_idx, 1.0, 0.0).astype(BF16)

    def cumsum_dir(f):
        bf = None
        bb = None
        for part in _split3(f):
            a = jnp.dot(part, sum_upto, preferred_element_type=F32)
            b = jnp.dot(part, sum_from, preferred_element_type=F32)
            bf = a if bf is None else bf + a
            bb = b if bb is None else bb + b
        return jnp.where(fwd_rows, bf, bb)

    def cummax_dir(u):
        xf = u
        xb = u
        sh = 1
        while sh < L:
            xf = jnp.maximum(xf, jnp.where(lane >= sh, pltpu.roll(xf, sh, axis=1), -jnp.inf))
            xb = jnp.maximum(xb, jnp.where(lane < L - sh, pltpu.roll(xb, L - sh, axis=1), -jnp.inf))
            sh *= 2
        return jnp.where(fwd_rows, xf, xb)

    b_c, u_c, mloc_c, bl_c, g_c, gmax_c = [], [], [], [], [], []
    for c in range(n_chunks):
        f = f_all[:, c * L:(c + 1) * L]
        i = i_all[:, c * L:(c + 1) * L]
        b = cumsum_dir(f)
        u = i - b
        bl = jnp.sum(f, axis=1, keepdims=True)
        g = bl + u
        b_c.append(b)
        u_c.append(u)
        mloc_c.append(b + cummax_dir(u))
        bl_c.append(bl)
        g_c.append(g)
        gmax_c.append(jnp.max(g, axis=1, keepdims=True))

    def scan(order):
        m = jnp.zeros((N_CHAINS, 1), F32)
        prev, new = {}, {}
        for c in order:
            prev[c] = m
            m = jnp.maximum(bl_c[c] + m, gmax_c[c])
            new[c] = m
        return prev, new

    order_f = list(range(n_chunks))
    order_b = [n_ctx - 1 - j for j in range(n_ctx)] + [n_chunks + n_ctx - 1 - j for j in range(n_ctx, n_chunks)]
    prev_f, new_f = scan(order_f)
    prev_b, new_b = scan(order_b)

    for c in range(n_chunks):
        m_prev = jnp.where(fwd_col, prev_f[c], prev_b[c])
        m_new = jnp.where(fwd_col, new_f[c], new_b[c])
        b = b_c[c]
        m_inter = b + m_prev
        m_t = jnp.maximum(m_inter, mloc_c[c])
        pack = jnp.concatenate([b - m_t, jnp.exp(m_inter - m_t), jnp.exp(-m_t), jnp.exp(g_c[c] - m_new),
                                jnp.zeros((LANES - 4 * N_CHAINS, L), F32)], axis=0)
        ct_ref[0, c * L:(c + 1) * L, :] = pack.T
        ur_ref[0, :, c * L:(c + 1) * L] = u_c[c]
        ar_ref[0, c] = jnp.broadcast_to(jnp.exp(bl_c[c] + m_prev - m_new), (N_CHAINS, LANES))


def _mlstm_gates(gates_row, bias_row):
    B, _, T = gates_row.shape
    n_chunks = T // CHUNK
    return pl.pallas_call(
        functools.partial(_mgates_kernel, T=T),
        grid=(B,),
        in_specs=[pl.BlockSpec((1, 2 * N_CHAINS, T), lambda b: (b, 0, 0)),
                  pl.BlockSpec((2 * N_CHAINS, T), lambda b: (0, 0))],
        out_specs=[pl.BlockSpec((1, T, LANES), lambda b: (b, 0, 0)),
                   pl.BlockSpec((1, N_CHAINS, T), lambda b: (b, 0, 0)),
                   pl.BlockSpec((1, n_chunks, N_CHAINS, LANES), lambda b: (b, 0, 0, 0))],
        out_shape=[jax.ShapeDtypeStruct((B, T, LANES), F32),
                   jax.ShapeDtypeStruct((B, N_CHAINS, T), F32),
                   jax.ShapeDtypeStruct((B, n_chunks, N_CHAINS, LANES), F32)],
        compiler_params=_params("parallel"),
        name="mlstm_gates",
    )(gates_row, bias_row)


SCAN_BATCH = 2


def _mlstm_kernel(qf_ref, ktf_ref, vf_ref, ctf_ref, urf_ref, arf_ref,
                  qb_ref, ktb_ref, vb_ref, ctb_ref, urb_ref, arb_ref, of_ref, ob_ref, st_ref):
    L = CHUNK

    @pl.when(pl.program_id(1) == 0)
    def _():
        st_ref[...] = jnp.zeros_like(st_ref)

    t_idx = lax.broadcasted_iota(jnp.int32, (L, L), 0)
    s_idx = lax.broadcasted_iota(jnp.int32, (L, L), 1)
    ones = jnp.ones((L, M_HD), BF16)
    dirs = ((qf_ref, ktf_ref, vf_ref, ctf_ref, urf_ref, arf_ref, of_ref),
            (qb_ref, ktb_ref, vb_ref, ctb_ref, urb_ref, arb_ref, ob_ref))
    for bi in range(SCAN_BATCH):
        for d, (q_ref, kt_ref, v_ref, ct_ref, ur_ref, ar_ref, o_ref) in enumerate(dirs):
            mask = (s_idx <= t_idx) if d == 0 else (s_idx >= t_idx)
            ct = ct_ref[bi]
            ur = ur_ref[bi]
            ar = ar_ref[bi, 0]
            outs = []
            for h in range(M_HEADS):
                c = d * M_HEADS + h
                q = q_ref[bi, :, h * M_HD:(h + 1) * M_HD]
                kt = kt_ref[bi, h * M_HD:(h + 1) * M_HD, :]
                v = v_ref[bi, :, h * M_HD:(h + 1) * M_HD]
                v_ext = jnp.concatenate([v, ones], axis=1)
                cq = ct[:, c:c + 1]
                w_inter = ct[:, N_CHAINS + c:N_CHAINS + c + 1]
                e_negm = ct[:, 2 * N_CHAINS + c:2 * N_CHAINS + c + 1]
                wk = ct[:, 3 * N_CHAINS + c:3 * N_CHAINS + c + 1]
                w_intra = jnp.exp(jnp.where(mask, cq + ur[c:c + 1, :], -jnp.inf))
                s = jnp.dot(q, kt, preferred_element_type=F32) * w_intra
                st_prev = st_ref[bi, d, h]
                lhs = jnp.concatenate([s.astype(BF16), (q.astype(F32) * w_inter).astype(BF16)], axis=1)
                rhs = jnp.concatenate([v_ext, st_prev.astype(BF16)], axis=0)
                tot = jnp.dot(lhs, rhs, preferred_element_type=F32)
                den = jnp.maximum(jnp.abs(tot[:, M_HD:]), e_negm)
                outs.append(tot[:, :M_HD] / den)
                x = (v_ext.astype(F32) * wk).astype(BF16)
                a = jnp.concatenate([ar[c:c + 1, :], ar[c:c + 1, :]], axis=1)
                st_ref[bi, d, h] = a * st_prev + jnp.dot(kt, x, preferred_element_type=F32)
            o_ref[0, bi] = jnp.concatenate(outs, axis=1).astype(o_ref.dtype)


def _mlstm_scan(zq, zkt, zv, ct, ur, ar):
    B, T, _ = zq.shape
    n_chunks = T // CHUNK
    n_ctx = CTX_LEN // CHUNK

    def fwd(j):
        return j

    def bwd(j):
        return jnp.where(j < n_ctx, n_ctx - 1 - j, n_chunks + n_ctx - 1 - j)

    nb = SCAN_BATCH

    def specs(order):
        return [pl.BlockSpec((nb, CHUNK, BRANCH_W), lambda b, j: (b, order(j), 0)),
                pl.BlockSpec((nb, BRANCH_W, CHUNK), lambda b, j: (b, 0, order(j))),
                pl.BlockSpec((nb, CHUNK, BRANCH_W), lambda b, j: (b, order(j), 0)),
                pl.BlockSpec((nb, CHUNK, LANES), lambda b, j: (b, order(j), 0)),
                pl.BlockSpec((nb, N_CHAINS, CHUNK), lambda b, j: (b, 0, order(j))),
                pl.BlockSpec((nb, 1, N_CHAINS, LANES), lambda b, j: (b, order(j), 0, 0))]

    out = pl.pallas_call(
        _mlstm_kernel,
        grid=(B // nb, n_chunks),
        in_specs=specs(fwd) + specs(bwd),
        out_specs=[pl.BlockSpec((1, nb, CHUNK, BRANCH_W), lambda b, j: (0, b, fwd(j), 0)),
                   pl.BlockSpec((1, nb, CHUNK, BRANCH_W), lambda b, j: (0, b, bwd(j), 0))],
        out_shape=[jax.ShapeDtypeStruct((1, B, T, BRANCH_W), F32),
                   jax.ShapeDtypeStruct((1, B, T, BRANCH_W), F32)],
        scratch_shapes=[pltpu.VMEM((nb, 2, M_HEADS, M_HD, 2 * M_HD), F32)],
        compiler_params=_params("parallel", "arbitrary"),
        name="mlstm_scan",
    )(zq, zkt, zv, ct, ur, ar, zq, zkt, zv, ct, ur, ar)
    return out[0][0], out[1][0]


def _merge_kernel(*refs, nt, mod_row):
    groups = [refs[i * nt:(i + 1) * nt] for i in range(6)]
    ya_refs, yw_refs, yg_refs, hf_refs, hb_refs, h_refs = groups
    mod_ref, wg_ref, wbr_ref, wo_ref, gpost_ref, mnorm_ref, o_ref = refs[6 * nt:]
    tm = CTX_LEN
    mult = mod_ref[0, mod_row:mod_row + 1, :]
    shift = mod_ref[0, mod_row + 1:mod_row + 2, :]
    gate = mod_ref[0, mod_row + 2:mod_row + 3, :]

    def rows(tile_refs):
        return jnp.concatenate([r[0] for r in tile_refs], axis=0)

    h = rows(h_refs)
    ya, yw, yg = rows(ya_refs), rows(yw_refs), rows(yg_refs)
    xn = _norm_mod(h, mult, shift)

    def proj(col, width):
        return jnp.dot(xn, wg_ref[:, col:col + width], preferred_element_type=F32)

    hm = rows(hf_refs) + rows(hb_refs)
    parts = []
    for hd in range(M_HEADS):
        x = hm[:, hd * M_HD:(hd + 1) * M_HD]
        ms = jnp.mean(x * x, axis=-1, keepdims=True)
        parts.append(x * lax.rsqrt(ms + EPS) * mnorm_ref[:, hd * M_HD:(hd + 1) * M_HD])
    ym = _sigmoid(proj(GCOL_MO, BRANCH_W)) * jnp.concatenate(parts, axis=1)
    ys = (ya.astype(F32), yw.astype(F32), yg.astype(F32), ym)
    acc = None
    for k in range(N_BRANCH):
        yk = (ys[k] * _silu(proj(GCOL_BRANCH_GATE[k], BRANCH_W))).astype(BF16)
        bp = jnp.dot(yk, wbr_ref[k], preferred_element_type=F32)
        g = _sigmoid(proj(GCOL_MERGE + k * D_MODEL, D_MODEL))
        acc = g * bp if acc is None else acc + g * bp
    y = jnp.dot(acc.astype(BF16), wo_ref[...], preferred_element_type=F32)
    ms = jnp.mean(y * y, axis=-1, keepdims=True)
    yn = y * lax.rsqrt(ms + EPS) * gpost_ref[...]
    o_ref[0] = h + gate * yn


def _merge(ya, yw, yg, hf, hb, h, mod8, wg_bf, wbr_bf, wo_bf, g_post, m_norm, *, context):
    tm = CTX_LEN
    B, rows, _ = h.shape
    nt = 1 if context else 2
    n_steps = rows // (nt * tm)
    stream_off = 0 if context else CTX_LEN // tm
    bw = BRANCH_W

    def tiles(w, off):
        return [pl.BlockSpec((1, tm, w), lambda b, t, i=i: (b, nt * t + i + off, 0)) for i in range(nt)]

    def const(shape):
        nd = len(shape)
        return pl.BlockSpec(shape, lambda b, t: (0,) * nd)

    token_specs = (tiles(bw, 0) + tiles(bw, 0) + tiles(bw, 0) + tiles(bw, stream_off) + tiles(bw, stream_off)
                   + tiles(D_MODEL, 0))
    token_args = [ya] * nt + [yw] * nt + [yg] * nt + [hf] * nt + [hb] * nt + [h] * nt
    return pl.pallas_call(
        functools.partial(_merge_kernel, nt=nt, mod_row=0 if context else 3),
        grid=(B, n_steps),
        in_specs=token_specs + [
            pl.BlockSpec((1, MOD_ROWS, D_MODEL), lambda b, t: (b, 0, 0)),
            const((D_MODEL, N_GATE)), const((N_BRANCH, bw, D_MODEL)), const((D_MODEL, D_MODEL)),
            const((1, D_MODEL)), const((1, bw))],
        out_specs=pl.BlockSpec((1, nt * tm, D_MODEL), lambda b, t: (b, t, 0)),
        out_shape=jax.ShapeDtypeStruct((B, rows, D_MODEL), F32),
        compiler_params=pltpu.CompilerParams(dimension_semantics=("parallel", "parallel"),
                                             vmem_limit_bytes=BIG_VMEM_LIMIT),
        name="merge_ctx" if context else "merge",
    )(*token_args, mod8, wg_bf, wbr_bf, wo_bf, g_post, m_norm)


def _rope_tables(n_lat):
    rows = n_lat // GRID_W
    row = jnp.repeat(jnp.arange(rows), GRID_W).astype(F32)
    col = jnp.tile(jnp.arange(GRID_W), rows).astype(F32)
    n_freq = HEAD_DIM // 4
    inv = ROPE_THETA ** (-jnp.arange(n_freq, dtype=F32) / n_freq)
    ang = jnp.concatenate([row[:, None] * inv, col[:, None] * inv], axis=-1)
    cos, sin = jnp.cos(ang), jnp.sin(ang)
    cos_h = jnp.concatenate([cos, cos], axis=-1)
    sin_h = jnp.concatenate([-sin, sin], axis=-1)
    cos_t = jnp.concatenate([jnp.ones((CTX_LEN, HEAD_DIM), F32), cos_h], axis=0)
    sin_t = jnp.concatenate([jnp.zeros((CTX_LEN, HEAD_DIM), F32), sin_h], axis=0)
    return jnp.tile(cos_t, (1, LANES // HEAD_DIM)), jnp.tile(sin_t, (1, LANES // HEAD_DIM))


def _gather_cols(w, names):
    sl = _ref_slices()
    runs = []
    for n in names:
        start, size = sl[n]
        if runs and runs[-1][1] == start:
            runs[-1][1] = start + size
        else:
            runs.append([start, start + size])
    return jnp.concatenate([w[:, a:b].astype(BF16) for a, b in runs], axis=1)


def kernel(x, c, ctx, c_ctx, w_mod, b_mod, g_pre, g_post, w_in, a_q_norm, a_k_norm, w_sink, sg_ln_g, sg_ln_b,
           sg_w, sg_b, m_conv, m_b_i, m_b_f, m_norm, w_branch, w_out):
    B, n_lat, D = x.shape
    depth = w_mod.shape[0]
    assert depth == 2, "the context stream is only carried from the first layer to the last"
    T = CTX_LEN + n_lat
    n_ctx_chunks = CTX_LEN // CHUNK
    n_lat_chunks = n_lat // CHUNK
    rope = _rope_tables(n_lat)
    gmat = (jnp.arange(LANES)[:, None] // HEAD_DIM == jnp.arange(LANES)[None, :] // HEAD_DIM).astype(BF16)

    hc, hx = ctx, x
    mod_rows = 16
    cc = jnp.zeros((mod_rows, D), F32).at[:B].set(c).at[B].set(c_ctx)
    tm_tok = CTX_LEN
    n_ctx_tiles = CTX_LEN // tm_tok
    n_lat_tiles = n_lat // tm_tok

    for l in range(depth):
        need_ctx = l < depth - 1
        mod = _modulation(cc, w_mod[l].astype(BF16), b_mod[l][None, :])
        shift, scale, gate = mod[:, :D], mod[:, D:2 * D], mod[:, 2 * D:]
        mult = g_pre[l][None, :] * (1.0 + scale)
        zero = jnp.zeros((B, D), F32)
        mod8 = jnp.stack([jnp.broadcast_to(mult[B], (B, D)), jnp.broadcast_to(shift[B], (B, D)),
                          jnp.broadcast_to(gate[B], (B, D)), mult[:B], shift[:B], gate[:B], zero, zero],
                         axis=1)

        w_gate = _gather_cols(w_in[l], _GATE_ORDER)
        p_main, p_gates = _in_projection(hc, hx, mod8, w_in[l])

        qn = jnp.tile(a_q_norm[l], LANES // HEAD_DIM)[None, :]
        kn = jnp.tile(a_k_norm[l], LANES // HEAD_DIM)[None, :]
        ya = _global_attention(p_main, rope, qn, kn, gmat,
                               q_tile0=n_ctx_chunks, n_q_tiles=n_lat_chunks, nk=T)
        yw = _window_attention(p_main, rope, w_sink[l], q_tile0=n_ctx_chunks, n_q_tiles=n_lat_chunks,
                               nk=T, banded=True)
        if need_ctx:
            ya_c = _global_attention(p_main, rope, qn, kn, gmat, q_tile0=0, n_q_tiles=n_ctx_chunks, nk=CTX_LEN)
            yw_c = _window_attention(p_main, rope, w_sink[l], q_tile0=0, n_q_tiles=n_ctx_chunks,
                                     nk=CTX_LEN, banded=False)

        sg_args = (p_main, sg_ln_g[l][None, :], sg_ln_b[l][None, :], sg_w[l].astype(BF16), sg_b[l].T)
        yg = _spatial_gating(*sg_args, tile0=n_ctx_tiles, n_tiles=n_lat_tiles, nt=4)
        if need_ctx:
            yg_c = _spatial_gating(*sg_args, tile0=0, n_tiles=n_ctx_tiles, nt=n_ctx_tiles)

        zq, zkt, zv = _mlstm_conv(p_main, m_conv[l])
        gates_row = jnp.swapaxes(p_gates[:, :, :2 * N_CHAINS], 1, 2)
        bias_row = jnp.broadcast_to(
            jnp.concatenate([m_b_i[l].reshape(-1), m_b_f[l].reshape(-1)])[:, None], (2 * N_CHAINS, T))
        ct, ur, ar = _mlstm_gates(gates_row, bias_row)
        hf, hb = _mlstm_scan(zq, zkt, zv, ct, ur, ar)

        weights = (mod8, w_gate, w_branch[l].astype(BF16), w_out[l].astype(BF16),
                   g_post[l][None, :], m_norm[l][None, :])
        hx_new = _merge(ya, yw, yg, hf, hb, hx, *weights, context=False)
        if need_ctx:
            hc = _merge(ya_c, yw_c, yg_c, hf, hb, hc, *weights, context=True)
        hx = hx_new
    return hx
```

```python
import functools

import jax
import jax.numpy as jnp
from jax import lax
from jax.experimental import pallas as pl
from jax.experimental.pallas import tpu as pltpu

F32 = jnp.float32
BF16 = jnp.bfloat16

D_MODEL = 1024
GRID_W = 64
CTX_LEN = 256
N_BRANCH = 4
BRANCH_W = 512
HEAD_DIM = 64
N_HEADS = 8
N_KV = 2
GROUP = N_HEADS // N_KV
WINDOW = 128
CHUNK = 128
B_GROUPS = 4
M_HEADS = 4
M_HD = 128
N_CHAINS = 2 * M_HEADS
ROPE_THETA = 10000.0
EPS = 1e-6
LOG2E = 1.4426950408889634
ATTN_TILES_PER_STEP = 2

LANES = 128
VMEM_LIMIT = 48 * 1024 * 1024
BIG_VMEM_LIMIT = 58 * 1024 * 1024

COL_AQ = 0
COL_WQ = 512
COL_AK = 1024
COL_AV = 1152
COL_WK = 1280
COL_WV = 1408
COL_GU = 1536
COL_GV = 2048
COL_MQKV = 2560
N_MAIN = 4096
GCOL_BRANCH_GATE = (0, 512, 1024, 2048)
GCOL_MO = 1536
GCOL_MERGE = 2560
N_GATE = GCOL_MERGE + N_BRANCH * D_MODEL

_REF_LAYOUT = (
    ('a_q', 512), ('a_k', 128), ('a_v', 128), ('a_gate', 512),
    ('w_q', 512), ('w_k', 128), ('w_v', 128), ('w_gate', 512),
    ('g_u', 512), ('g_v', 512), ('g_gate', 512),
    ('m_qkv', 1536), ('m_i', 8), ('m_f', 8), ('m_o', 512), ('m_gate', 512),
    ('merge', 4096),
)
_MAIN_ORDER = ('a_q', 'w_q', 'a_k', 'a_v', 'w_k', 'w_v', 'g_u', 'g_v', 'm_qkv')
_GATE_ORDER = ('a_gate', 'w_gate', 'g_gate', 'm_o', 'm_gate', 'merge')


def _ref_slices():
    out, start = {}, 0
    for name, size in _REF_LAYOUT:
        out[name] = (start, size)
        start += size
    return out


def _params(*sem):
    return pltpu.CompilerParams(dimension_semantics=sem, vmem_limit_bytes=VMEM_LIMIT)


def _sigmoid(x):
    return 1.0 / (1.0 + jnp.exp2(x * (-LOG2E)))


def _silu(x):
    return x * _sigmoid(x)


def _gelu_tanh(x):
    c = 0.7978845608028654
    return 0.5 * x * (1.0 + jnp.tanh(c * (x + 0.044715 * (x * x * x))))


def _log_sigmoid(x):
    return -(jnp.maximum(-x, 0.0) + jnp.log1p(jnp.exp(-jnp.abs(x))))


def _mod_kernel(c_ref, w_ref, b_ref, o_ref):
    a = _silu(c_ref[...]).astype(BF16)
    o_ref[...] = jnp.dot(a, w_ref[...], preferred_element_type=F32) + b_ref[...]


def _modulation(cc, w_mod_bf, b_mod):
    rows = cc.shape[0]
    n = w_mod_bf.shape[1]
    tn = 1024
    return pl.pallas_call(
        _mod_kernel,
        grid=(n // tn,),
        in_specs=[pl.BlockSpec((rows, D_MODEL), lambda j: (0, 0)),
                  pl.BlockSpec((D_MODEL, tn), lambda j: (0, j)),
                  pl.BlockSpec((1, tn), lambda j: (0, j))],
        out_specs=pl.BlockSpec((rows, tn), lambda j: (0, j)),
        out_shape=jax.ShapeDtypeStruct((rows, n), F32),
        compiler_params=_params("parallel"),
        name="modulation",
    )(cc, w_mod_bf, b_mod)


MOD_ROWS = 8


def _norm_mod(x, mult, shift):
    ms = jnp.mean(x * x, axis=-1, keepdims=True)
    return (x * lax.rsqrt(ms + EPS) * mult + shift).astype(BF16)


W_TILE = 256
TILES_PER_STEP = 4


def _main_tiles():
    sl = _ref_slices()
    tiles = []
    for name in _MAIN_ORDER:
        start, size = sl[name]
        if size < W_TILE:
            if start % W_TILE == 0:
                tiles.append(start // W_TILE)
            continue
        assert start % W_TILE == 0 and size % W_TILE == 0
        tiles.extend(range(start // W_TILE, (start + size) // W_TILE))
    return tiles


def _inproj_kernel(tbl_ref, hc_ref, hx_ref, mod_ref, *refs, n_lat):
    del tbl_ref
    w_refs = refs[:TILES_PER_STEP]
    wl_ref, o_ref, og_ref, xn_ref = refs[TILES_PER_STEP:]
    rb = CTX_LEN

    @pl.when(pl.program_id(1) == 0)
    def _():
        xn_ref[0:rb, :] = _norm_mod(hc_ref[0], mod_ref[0, 0:1, :], mod_ref[0, 1:2, :])

        def norm_rows(r, carry):
            off = pl.multiple_of(r * rb, rb)
            x = hx_ref[0, pl.ds(off, rb), :]
            xn_ref[pl.ds(rb + off, rb), :] = _norm_mod(x, mod_ref[0, 3:4, :], mod_ref[0, 4:5, :])
            return carry

        lax.fori_loop(0, n_lat // rb, norm_rows, 0)
        og_ref[0] = jnp.dot(xn_ref[...], wl_ref[0, :, 0:LANES].astype(BF16), preferred_element_type=F32)

    w = jnp.concatenate([r[0].astype(BF16) for r in w_refs], axis=1)
    o_ref[0] = jnp.dot(xn_ref[...], w, preferred_element_type=F32).astype(o_ref.dtype)


def _in_projection(hc, hx, mod8, w_in, layer):
    B, n_lat, _ = hx.shape
    T = CTX_LEN + n_lat
    tiles = _main_tiles()
    assert len(tiles) * W_TILE == N_MAIN and len(tiles) % TILES_PER_STEP == 0
    table = jnp.asarray(tiles, jnp.int32)
    lstm_tile = _ref_slices()['m_i'][0] // W_TILE
    assert _ref_slices()['m_i'][0] % W_TILE == 0
    tn = TILES_PER_STEP * W_TILE
    h_specs = [pl.BlockSpec((1, a.shape[1], D_MODEL), lambda b, j, tbl: (b, 0, 0)) for a in (hc, hx)]
    w_specs = [pl.BlockSpec((1, D_MODEL, W_TILE), lambda b, j, tbl, i=i: (layer, 0, tbl[TILES_PER_STEP * j + i]))
               for i in range(TILES_PER_STEP)]
    grid_spec = pltpu.PrefetchScalarGridSpec(
        num_scalar_prefetch=1,
        grid=(B, len(tiles) // TILES_PER_STEP),
        in_specs=h_specs + [pl.BlockSpec((1, MOD_ROWS, D_MODEL), lambda b, j, tbl: (b, 0, 0))] + w_specs
        + [pl.BlockSpec((1, D_MODEL, W_TILE), lambda b, j, tbl: (layer, 0, lstm_tile))],
        out_specs=[pl.BlockSpec((1, T, tn), lambda b, j, tbl: (b, 0, j)),
                   pl.BlockSpec((1, T, LANES), lambda b, j, tbl: (b, 0, 0))],
        scratch_shapes=[pltpu.VMEM((T, D_MODEL), BF16)])
    return pl.pallas_call(
        functools.partial(_inproj_kernel, n_lat=n_lat),
        grid_spec=grid_spec,
        out_shape=[jax.ShapeDtypeStruct((B, T, N_MAIN), BF16),
                   jax.ShapeDtypeStruct((B, T, LANES), F32)],
        compiler_params=pltpu.CompilerParams(dimension_semantics=("parallel", "arbitrary"),
                                             vmem_limit_bytes=BIG_VMEM_LIMIT),
        name="in_projection",
    )(table, hc, hx, mod8, *([w_in] * (TILES_PER_STEP + 1)))


def _rope_slab(x, cos, sin):
    lane = lax.broadcasted_iota(jnp.int32, x.shape, 1)
    first_half = (lane & (HEAD_DIM - 1)) < (HEAD_DIM // 2)
    partner = jnp.where(first_half,
                        pltpu.roll(x, LANES - HEAD_DIM // 2, axis=1),
                        pltpu.roll(x, HEAD_DIM // 2, axis=1))
    return x * cos + partner * sin


def _head_norm_slab(x, gain, gmat):
    ss = jnp.dot((x * x).astype(BF16), gmat, preferred_element_type=F32)
    return x * lax.rsqrt(ss * (1.0 / HEAD_DIM) + EPS) * gain


def _prep_kv(k_ref, v_ref, cosk_ref, sink_ref, kn_ref, gmat_ref, kp_ref, vt_ref, nk, use_norm):
    k = k_ref[0, 0:nk, :].astype(F32)
    if use_norm:
        k = _head_norm_slab(k, kn_ref[...], gmat_ref[...])
    k = _rope_slab(k, cosk_ref[0:nk, :], sink_ref[0:nk, :]).astype(BF16)
    for g in range(N_KV):
        kp_ref[g, 0:nk, :] = k[:, g * HEAD_DIM:(g + 1) * HEAD_DIM]
    for blk in range(nk // CHUNK):
        v = v_ref[0, blk * CHUNK:(blk + 1) * CHUNK, :].astype(F32)
        vt_ref[blk] = v.T.astype(BF16)


def _prep_qt(q_ref, cosq_ref, sinq_ref, qn_ref, gmat_ref, use_norm, r0, tq):
    cos = cosq_ref[r0:r0 + tq, :]
    sin = sinq_ref[r0:r0 + tq, :]
    xts = []
    for s in range(N_HEADS // 2):
        x = q_ref[0, r0:r0 + tq, s * LANES:(s + 1) * LANES].astype(F32)
        if use_norm:
            x = _head_norm_slab(x, qn_ref[...], gmat_ref[...])
        x = _rope_slab(x, cos, sin) * (HEAD_DIM ** -0.5 * LOG2E)
        xts.append(x.T.astype(BF16))
    per = GROUP // 2
    out = []
    for g in range(N_KV):
        parts = []
        for s in range(g * per, (g + 1) * per):
            parts.append(xts[s][:HEAD_DIM, :])
            parts.append(xts[s][HEAD_DIM:, :])
        out.append(jnp.concatenate(parts, axis=1))
    return out


def _heads_from_t(ot, tq):
    slabs = []
    for j in range(GROUP // 2):
        two = jnp.concatenate([ot[:, (2 * j) * tq:(2 * j + 1) * tq],
                               ot[:, (2 * j + 1) * tq:(2 * j + 2) * tq]], axis=0)
        slabs.append(two.T)
    return jnp.concatenate(slabs, axis=1)


def _window_pieces(wsink_ref, q_ref, cosq_ref, sinq_ref, o_ref, kp_ref, vt_ref, step, *,
                   tq, nsub, banded, n_lat_blocks):
    cols = GROUP * tq
    gw = GROUP * HEAD_DIM
    n_ctx_chunks = CTX_LEN // CHUNK
    lane = lax.broadcasted_iota(jnp.int32, (1, cols), 1)
    sink_rows = []
    for g in range(N_KV):
        sink_row = jnp.zeros((1, cols), F32)
        for h in range(GROUP):
            in_head = jnp.logical_and(lane >= h * tq, lane < (h + 1) * tq)
            sink_row = jnp.where(in_head, wsink_ref[g * GROUP + h] * LOG2E, sink_row)
        sink_rows.append(sink_row)
    if banded:
        kk = lax.broadcasted_iota(jnp.int32, (CHUNK, cols), 0)
        qq = lax.broadcasted_iota(jnp.int32, (CHUNK, cols), 1) & (tq - 1)

    for sub in range(nsub):
        i = step * nsub + sub
        qts = _prep_qt(q_ref, cosq_ref, sinq_ref, None, None, False, sub * tq, tq)
        yield
        if banded:
            prev_blk = i + n_ctx_chunks - 1
            own_blk = i + n_ctx_chunks
            next_blk = jnp.minimum(i + n_ctx_chunks + 1, n_ctx_chunks + n_lat_blocks - 1)
            ok_prev = jnp.logical_and(kk >= qq, i > 0)
            ok_next = jnp.logical_and(kk <= qq, i < n_lat_blocks - 1)
            plan = ([(cb, None) for cb in range(n_ctx_chunks)]
                    + [(prev_blk, ok_prev), (own_blk, None), (next_blk, ok_next)])
        else:
            plan = [(cb, None) for cb in range(n_ctx_chunks)]

        for g in range(N_KV):
            qt = qts[g]
            sink_row = sink_rows[g]
            blocks = []
            for blk, ok in plan:
                off = blk * CHUNK
                if not isinstance(off, int):
                    off = pl.multiple_of(off, CHUNK)
                st = jnp.dot(kp_ref[g, pl.ds(off, CHUNK), :], qt, preferred_element_type=F32)
                blocks.append((st if ok is None else jnp.where(ok, st, -jnp.inf), blk))
                yield
            m = sink_row
            for st, _ in blocks:
                m = jnp.maximum(m, jnp.max(st, axis=0, keepdims=True))
            l = jnp.exp2(sink_row - m)
            acc = None
            for st, blk in blocks:
                p = jnp.exp2(st - m)
                l = l + jnp.sum(p, axis=0, keepdims=True)
                pv = jnp.dot(vt_ref[blk, g * HEAD_DIM:(g + 1) * HEAD_DIM, :], p.astype(BF16),
                             preferred_element_type=F32)
                acc = pv if acc is None else acc + pv
                yield
            o_ref[0, sub * tq:(sub + 1) * tq, g * gw:(g + 1) * gw] = (
                _heads_from_t(acc / l, tq).astype(o_ref.dtype))
            yield


def _attn_kernel(wsink_ref, qa_ref, ka_ref, va_ref, qw_ref, kw_ref, vw_ref, cosq_ref, sinq_ref,
                 cosk_ref, sink_ref, qn_ref, kn_ref, gmat_ref, oa_ref, ow_ref,
                 kpa_ref, vta_ref, s_ref, kpw_ref, vtw_ref, *, tq, nsub, nk, ck, banded, n_lat_blocks):
    step = pl.program_id(1)

    @pl.when(step == 0)
    def _():
        _prep_kv(ka_ref, va_ref, cosk_ref, sink_ref, kn_ref, gmat_ref, kpa_ref, vta_ref, nk, True)
        _prep_kv(kw_ref, vw_ref, cosk_ref, sink_ref, None, None, kpw_ref, vtw_ref, nk, False)

    window = _window_pieces(wsink_ref, qw_ref, cosq_ref, sinq_ref, ow_ref, kpw_ref, vtw_ref, step,
                            tq=tq, nsub=nsub, banded=banded, n_lat_blocks=n_lat_blocks)

    qts = [_prep_qt(qa_ref, cosq_ref, sinq_ref, qn_ref, gmat_ref, True, sub * tq, tq) for sub in range(nsub)]
    items = [(sub, g) for sub in range(nsub) for g in range(N_KV)]
    n_chunks = nk // ck
    blocks_per_chunk = ck // CHUNK
    gw = GROUP * HEAD_DIM

    def score_chunk(k, c, m):
        sub, g = items[k]
        st = jnp.dot(kpa_ref[g, c * ck:(c + 1) * ck, :], qts[sub][g], preferred_element_type=F32)
        s_ref[k, c * ck:(c + 1) * ck, :] = st
        cm = jnp.max(st, axis=0, keepdims=True)
        return cm if m is None else jnp.maximum(m, cm)

    def value_chunk(k, c, m, l, acc):
        _, g = items[k]
        p = jnp.exp2(s_ref[k, c * ck:(c + 1) * ck, :] - m)
        cl = jnp.sum(p, axis=0, keepdims=True)
        vt = jnp.concatenate(
            [vta_ref[c * blocks_per_chunk + j, g * HEAD_DIM:(g + 1) * HEAD_DIM, :]
             for j in range(blocks_per_chunk)], axis=1)
        pv = jnp.dot(vt, p.astype(BF16), preferred_element_type=F32)
        return (cl, pv) if l is None else (l + cl, acc + pv)

    m_next = None
    for c in range(n_chunks):
        m_next = score_chunk(0, c, m_next)
        next(window, None)
    for k, (sub, g) in enumerate(items):
        m, m_next = m_next, None
        l = acc = None
        for c in range(n_chunks):
            l, acc = value_chunk(k, c, m, l, acc)
            if k + 1 < len(items):
                m_next = score_chunk(k + 1, c, m_next)
            next(window, None)
        oa_ref[0, sub * tq:(sub + 1) * tq, g * gw:(g + 1) * gw] = _heads_from_t(acc / l, tq).astype(oa_ref.dtype)
    for _ in window:
        pass


def _attention(p_main, rope, qn, kn, gmat, w_sink, *, q_tile0, n_q_tiles, nk, banded):
    B, T, _ = p_main.shape
    cos, sin = rope
    tq = CHUNK
    nsub = ATTN_TILES_PER_STEP
    tb = nsub * tq
    blk0 = q_tile0 // nsub
    ck = 256
    n_lat_blocks = (T - CTX_LEN) // CHUNK
    kernel = functools.partial(_attn_kernel, tq=tq, nsub=nsub, nk=nk, ck=ck, banded=banded,
                               n_lat_blocks=n_lat_blocks)

    def q_spec(col):
        return pl.BlockSpec((1, tb, BRANCH_W), lambda b, i: (b, i + blk0, col // BRANCH_W))

    def kv_spec(col):
        return pl.BlockSpec((1, T, LANES), lambda b, i: (b, 0, col // LANES))

    def const(shape):
        return pl.BlockSpec(shape, lambda b, i: (0, 0))

    out = jax.ShapeDtypeStruct((B, n_q_tiles * tq, BRANCH_W), BF16)
    kv_scratch = [pltpu.VMEM((N_KV, nk, HEAD_DIM), BF16), pltpu.VMEM((nk // CHUNK, LANES, CHUNK), BF16)]
    return pl.pallas_call(
        kernel,
        grid=(B, n_q_tiles // nsub),
        in_specs=[pl.BlockSpec(memory_space=pltpu.SMEM),
                  q_spec(COL_AQ), kv_spec(COL_AK), kv_spec(COL_AV),
                  q_spec(COL_WQ), kv_spec(COL_WK), kv_spec(COL_WV),
                  pl.BlockSpec((tb, LANES), lambda b, i: (i + blk0, 0)),
                  pl.BlockSpec((tb, LANES), lambda b, i: (i + blk0, 0)),
                  const((T, LANES)), const((T, LANES)),
                  const((1, LANES)), const((1, LANES)), const((LANES, LANES))],
        out_specs=[pl.BlockSpec((1, tb, BRANCH_W), lambda b, i: (b, i, 0)),
                   pl.BlockSpec((1, tb, BRANCH_W), lambda b, i: (b, i, 0))],
        out_shape=[out, out],
        scratch_shapes=kv_scratch + [pltpu.VMEM((nsub * N_KV, nk, GROUP * tq), F32)] + kv_scratch,
        compiler_params=_params("parallel", "arbitrary"),
        name="attention",
    )(w_sink, p_main, p_main, p_main, p_main, p_main, p_main, cos, sin, cos, sin, qn, kn, gmat)


def _sgate_kernel(*refs, nt, tm):
    u_refs, v_refs = refs[:nt], refs[nt:2 * nt]
    lng_ref, lnb_ref, ws_ref, bs_ref, o_ref = refs[2 * nt:]
    gw = BRANCH_W // B_GROUPS
    for i in range(nt):
        u = _gelu_tanh(u_refs[i][0].astype(F32))
        v = _gelu_tanh(v_refs[i][0].astype(F32))
        mu = jnp.mean(v, axis=-1, keepdims=True)
        vc = v - mu
        var = jnp.mean(vc * vc, axis=-1, keepdims=True)
        vn = (vc * lax.rsqrt(var + EPS) * lng_ref[...] + lnb_ref[...]).astype(BF16)
        chunks = []
        for c in range(tm // CHUNK):
            cols = []
            for g in range(B_GROUPS):
                mixed = jnp.dot(ws_ref[g], vn[c * CHUNK:(c + 1) * CHUNK, g * gw:(g + 1) * gw],
                                preferred_element_type=F32)
                cols.append(mixed + bs_ref[:, g:g + 1])
            chunks.append(jnp.concatenate(cols, axis=1))
        o_ref[0, i * tm:(i + 1) * tm, :] = (u * jnp.concatenate(chunks, axis=0)).astype(o_ref.dtype)


def _spatial_gating(p_main, ln_g, ln_b, ws_bf, bs_t, *, tile0, n_tiles, nt):
    B, T, _ = p_main.shape
    tm = CTX_LEN

    def tiles(col):
        return [pl.BlockSpec((1, tm, BRANCH_W), lambda b, t, i=i: (b, nt * t + i + tile0, col))
                for i in range(nt)]

    return pl.pallas_call(
        functools.partial(_sgate_kernel, nt=nt, tm=tm),
        grid=(B, n_tiles // nt),
        in_specs=tiles(COL_GU // BRANCH_W) + tiles(COL_GV // BRANCH_W) + [
            pl.BlockSpec((1, BRANCH_W), lambda b, t: (0, 0)),
            pl.BlockSpec((1, BRANCH_W), lambda b, t: (0, 0)),
            pl.BlockSpec((B_GROUPS, CHUNK, CHUNK), lambda b, t: (0, 0, 0)),
            pl.BlockSpec((CHUNK, B_GROUPS), lambda b, t: (0, 0))],
        out_specs=pl.BlockSpec((1, nt * tm, BRANCH_W), lambda b, t: (b, t, 0)),
        out_shape=jax.ShapeDtypeStruct((B, n_tiles * tm, BRANCH_W), BF16),
        compiler_params=_params("parallel", "parallel"),
        name="spatial_gating",
    )(*([p_main] * (2 * nt)), ln_g, ln_b, ws_bf, bs_t)


def _conv_kernel(xq_ref, xk_ref, xv_ref, w_ref, oq_ref, okt_ref, ov_ref, *, T):
    row = lax.broadcasted_iota(jnp.int32, (T, 1), 0)
    no_prev = jnp.logical_or(row == 0, row == CTX_LEN)
    no_next = jnp.logical_or(row == CTX_LEN - 1, row == T - 1)
    for part, (x_ref, o_ref) in enumerate(((xq_ref, oq_ref), (xk_ref, okt_ref), (xv_ref, ov_ref))):
        for j in range(BRANCH_W // LANES):
            lo = j * LANES
            wl = part * BRANCH_W + lo
            x = x_ref[0, :, lo:lo + LANES].astype(F32)
            prev = jnp.where(no_prev, 0.0, pltpu.roll(x, 1, axis=0))
            nxt = jnp.where(no_next, 0.0, pltpu.roll(x, T - 1, axis=0))
            y = (prev * w_ref[0:1, wl:wl + LANES] + x * w_ref[1:2, wl:wl + LANES]
                 + nxt * w_ref[2:3, wl:wl + LANES])
            z = _silu(y)
            if part == 0:
                o_ref[0, :, lo:lo + LANES] = (z * (M_HD ** -0.5)).astype(o_ref.dtype)
            elif part == 1:
                o_ref[0, lo:lo + LANES, :] = z.T.astype(o_ref.dtype)
            else:
                o_ref[0, :, lo:lo + LANES] = z.astype(o_ref.dtype)


def _mlstm_conv(p_main, conv_w):
    B, T, _ = p_main.shape
    col0 = COL_MQKV // BRANCH_W
    row_major = jax.ShapeDtypeStruct((B, T, BRANCH_W), BF16)
    return pl.pallas_call(
        functools.partial(_conv_kernel, T=T),
        grid=(B,),
        in_specs=[pl.BlockSpec((1, T, BRANCH_W), lambda b, p=p: (b, 0, col0 + p)) for p in range(3)]
        + [pl.BlockSpec((3, 3 * BRANCH_W), lambda b: (0, 0))],
        out_specs=[pl.BlockSpec((1, T, BRANCH_W), lambda b: (b, 0, 0)),
                   pl.BlockSpec((1, BRANCH_W, T), lambda b: (b, 0, 0)),
                   pl.BlockSpec((1, T, BRANCH_W), lambda b: (b, 0, 0))],
        out_shape=[row_major, jax.ShapeDtypeStruct((B, BRANCH_W, T), BF16), row_major],
        compiler_params=_params("parallel"),
        name="mlstm_conv",
    )(p_main, p_main, p_main, conv_w)


def _split3(x):
    x1 = x.astype(BF16)
    r = x - x1.astype(F32)
    x2 = r.astype(BF16)
    x3 = (r - x2.astype(F32)).astype(BF16)
    return x1, x2, x3


def _mgates_kernel(g_ref, bias_ref, ct_ref, ur_ref, ar_ref, *, T):
    L = CHUNK
    n_chunks = T // L
    n_ctx = CTX_LEN // L
    gall = g_ref[0] + bias_ref[...]
    i_all = gall[0:N_CHAINS]
    f_all = _log_sigmoid(gall[N_CHAINS:2 * N_CHAINS])
    fwd_rows = lax.broadcasted_iota(jnp.int32, (N_CHAINS, L), 0) < M_HEADS
    fwd_col = lax.broadcasted_iota(jnp.int32, (N_CHAINS, 1), 0) < M_HEADS
    lane = lax.broadcasted_iota(jnp.int32, (N_CHAINS, L), 1)
    s_idx = lax.broadcasted_iota(jnp.int32, (L, L), 0)
    t_idx = lax.broadcasted_iota(jnp.int32, (L, L), 1)
    sum_upto = jnp.where(s_idx <= t_idx, 1.0, 0.0).astype(BF16)
    sum_from = jnp.where(s_idx >= t_idx, 1.0, 0.0).astype(BF16)

    def cumsum_dir(f):
        bf = None
        bb = None
        for part in _split3(f):
            a = jnp.dot(part, sum_upto, preferred_element_type=F32)
            b = jnp.dot(part, sum_from, preferred_element_type=F32)
            bf = a if bf is None else bf + a
            bb = b if bb is None else bb + b
        return jnp.where(fwd_rows, bf, bb)

    def cummax_dir(u):
        xf = u
        xb = u
        sh = 1
        while sh < L:
            xf = jnp.maximum(xf, jnp.where(lane >= sh, pltpu.roll(xf, sh, axis=1), -jnp.inf))
            xb = jnp.maximum(xb, jnp.where(lane < L - sh, pltpu.roll(xb, L - sh, axis=1), -jnp.inf))
            sh *= 2
        return jnp.where(fwd_rows, xf, xb)

    b_c, u_c, mloc_c, bl_c, g_c, gmax_c = [], [], [], [], [], []
    for c in range(n_chunks):
        f = f_all[:, c * L:(c + 1) * L]
        i = i_all[:, c * L:(c + 1) * L]
        b = cumsum_dir(f)
        u = i - b
        bl = jnp.sum(f, axis=1, keepdims=True)
        g = bl + u
        b_c.append(b)
        u_c.append(u)
        mloc_c.append(b + cummax_dir(u))
        bl_c.append(bl)
        g_c.append(g)
        gmax_c.append(jnp.max(g, axis=1, keepdims=True))

    def scan(order):
        m = jnp.zeros((N_CHAINS, 1), F32)
        prev, new = {}, {}
        for c in order:
            prev[c] = m
            m = jnp.maximum(bl_c[c] + m, gmax_c[c])
            new[c] = m
        return prev, new

    order_f = list(range(n_chunks))
    order_b = [n_ctx - 1 - j for j in range(n_ctx)] + [n_chunks + n_ctx - 1 - j for j in range(n_ctx, n_chunks)]
    prev_f, new_f = scan(order_f)
    prev_b, new_b = scan(order_b)

    for c in range(n_chunks):
        m_prev = jnp.where(fwd_col, prev_f[c], prev_b[c])
        m_new = jnp.where(fwd_col, new_f[c], new_b[c])
        b = b_c[c]
        m_inter = b + m_prev
        m_t = jnp.maximum(m_inter, mloc_c[c])
        pack = jnp.concatenate([b - m_t, jnp.exp(m_inter - m_t), jnp.exp(-m_t),
                                jnp.zeros((LANES - 3 * N_CHAINS, L), F32)], axis=0)
        ct_ref[0, c * L:(c + 1) * L, :] = pack.T
        ur_ref[0, :, c * L:(c + 1) * L] = jnp.concatenate([u_c[c], jnp.exp(g_c[c] - m_new)], axis=0)
        ar_ref[0, c] = jnp.broadcast_to(jnp.exp(bl_c[c] + m_prev - m_new), (N_CHAINS, LANES))


def _mlstm_gates(gates_row, bias_row):
    B, _, T = gates_row.shape
    n_chunks = T // CHUNK
    return pl.pallas_call(
        functools.partial(_mgates_kernel, T=T),
        grid=(B,),
        in_specs=[pl.BlockSpec((1, 2 * N_CHAINS, T), lambda b: (b, 0, 0)),
                  pl.BlockSpec((2 * N_CHAINS, T), lambda b: (0, 0))],
        out_specs=[pl.BlockSpec((1, T, LANES), lambda b: (b, 0, 0)),
                   pl.BlockSpec((1, 2 * N_CHAINS, T), lambda b: (b, 0, 0)),
                   pl.BlockSpec((1, n_chunks, N_CHAINS, LANES), lambda b: (b, 0, 0, 0))],
        out_shape=[jax.ShapeDtypeStruct((B, T, LANES), F32),
                   jax.ShapeDtypeStruct((B, 2 * N_CHAINS, T), F32),
                   jax.ShapeDtypeStruct((B, n_chunks, N_CHAINS, LANES), F32)],
        compiler_params=_params("parallel"),
        name="mlstm_gates",
    )(gates_row, bias_row)


SCAN_BATCH = 2


def _mlstm_kernel(qf_ref, ktf_ref, vf_ref, ctf_ref, urf_ref, arf_ref,
                  qb_ref, ktb_ref, vb_ref, ctb_ref, urb_ref, arb_ref, of_ref, ob_ref, st_ref):
    L = CHUNK

    @pl.when(pl.program_id(1) == 0)
    def _():
        st_ref[...] = jnp.zeros_like(st_ref)

    t_idx = lax.broadcasted_iota(jnp.int32, (L, L), 0)
    s_idx = lax.broadcasted_iota(jnp.int32, (L, L), 1)
    ones = jnp.ones((L, M_HD), BF16)
    dirs = ((qf_ref, ktf_ref, vf_ref, ctf_ref, urf_ref, arf_ref, of_ref),
            (qb_ref, ktb_ref, vb_ref, ctb_ref, urb_ref, arb_ref, ob_ref))
    for bi in range(SCAN_BATCH):
        for d, (q_ref, kt_ref, v_ref, ct_ref, ur_ref, ar_ref, o_ref) in enumerate(dirs):
            mask = (s_idx <= t_idx) if d == 0 else (s_idx >= t_idx)
            ct = ct_ref[bi]
            ur = ur_ref[bi]
            ar = ar_ref[bi, 0]
            outs = []
            for h in range(M_HEADS):
                c = d * M_HEADS + h
                q = q_ref[bi, :, h * M_HD:(h + 1) * M_HD]
                kt = kt_ref[bi, h * M_HD:(h + 1) * M_HD, :]
                v = v_ref[bi, :, h * M_HD:(h + 1) * M_HD]
                v_ext = jnp.concatenate([v, ones], axis=1)
                cq = ct[:, c:c + 1]
                w_inter = ct[:, N_CHAINS + c:N_CHAINS + c + 1]
                e_negm = ct[:, 2 * N_CHAINS + c:2 * N_CHAINS + c + 1]
                wk_row = ur[N_CHAINS + c:N_CHAINS + c + 1, :]
                w_intra = jnp.exp(jnp.where(mask, cq + ur[c:c + 1, :], -jnp.inf))
                s = jnp.dot(q, kt, preferred_element_type=F32) * w_intra
                st_prev = st_ref[bi, d, h]
                lhs = jnp.concatenate([s.astype(BF16), (q.astype(F32) * w_inter).astype(BF16)], axis=1)
                rhs = jnp.concatenate([v_ext, st_prev.astype(BF16)], axis=0)
                tot = jnp.dot(lhs, rhs, preferred_element_type=F32)
                den = jnp.maximum(jnp.abs(tot[:, M_HD:]), e_negm)
                outs.append(tot[:, :M_HD] / den)
                kt_w = (kt.astype(F32) * wk_row).astype(BF16)
                a = jnp.concatenate([ar[c:c + 1, :], ar[c:c + 1, :]], axis=1)
                st_ref[bi, d, h] = a * st_prev + jnp.dot(kt_w, v_ext, preferred_element_type=F32)
            o_ref[0, bi] = jnp.concatenate(outs, axis=1).astype(o_ref.dtype)


def _mlstm_scan(zq, zkt, zv, ct, ur, ar):
    B, T, _ = zq.shape
    n_chunks = T // CHUNK
    n_ctx = CTX_LEN // CHUNK

    def fwd(j):
        return j

    def bwd(j):
        return jnp.where(j < n_ctx, n_ctx - 1 - j, n_chunks + n_ctx - 1 - j)

    nb = SCAN_BATCH

    def specs(order):
        return [pl.BlockSpec((nb, CHUNK, BRANCH_W), lambda b, j: (b, order(j), 0)),
                pl.BlockSpec((nb, BRANCH_W, CHUNK), lambda b, j: (b, 0, order(j))),
                pl.BlockSpec((nb, CHUNK, BRANCH_W), lambda b, j: (b, order(j), 0)),
                pl.BlockSpec((nb, CHUNK, LANES), lambda b, j: (b, order(j), 0)),
                pl.BlockSpec((nb, 2 * N_CHAINS, CHUNK), lambda b, j: (b, 0, order(j))),
                pl.BlockSpec((nb, 1, N_CHAINS, LANES), lambda b, j: (b, order(j), 0, 0))]

    out = pl.pallas_call(
        _mlstm_kernel,
        grid=(B // nb, n_chunks),
        in_specs=specs(fwd) + specs(bwd),
        out_specs=[pl.BlockSpec((1, nb, CHUNK, BRANCH_W), lambda b, j: (0, b, fwd(j), 0)),
                   pl.BlockSpec((1, nb, CHUNK, BRANCH_W), lambda b, j: (0, b, bwd(j), 0))],
        out_shape=[jax.ShapeDtypeStruct((1, B, T, BRANCH_W), F32),
                   jax.ShapeDtypeStruct((1, B, T, BRANCH_W), F32)],
        scratch_shapes=[pltpu.VMEM((nb, 2, M_HEADS, M_HD, 2 * M_HD), F32)],
        compiler_params=_params("parallel", "arbitrary"),
        name="mlstm_scan",
    )(zq, zkt, zv, ct, ur, ar, zq, zkt, zv, ct, ur, ar)
    return out[0][0], out[1][0]


def _merge_kernel(*refs, nt, mod_row):
    groups = [refs[i * nt:(i + 1) * nt] for i in range(6)]
    ya_refs, yw_refs, yg_refs, hf_refs, hb_refs, h_refs = groups
    mod_ref, wg_ref, wbr_ref, wo_ref, gpost_ref, mnorm_ref, o_ref = refs[6 * nt:]
    tm = CTX_LEN
    mult = mod_ref[0, mod_row:mod_row + 1, :]
    shift = mod_ref[0, mod_row + 1:mod_row + 2, :]
    gate = mod_ref[0, mod_row + 2:mod_row + 3, :]

    def rows(tile_refs):
        return jnp.concatenate([r[0] for r in tile_refs], axis=0)

    h = rows(h_refs)
    ya, yw, yg = rows(ya_refs), rows(yw_refs), rows(yg_refs)
    xn = _norm_mod(h, mult, shift)

    def proj(col, width):
        return jnp.dot(xn, wg_ref[:, col:col + width], preferred_element_type=F32)

    hm = rows(hf_refs) + rows(hb_refs)
    parts = []
    for hd in range(M_HEADS):
        x = hm[:, hd * M_HD:(hd + 1) * M_HD]
        ms = jnp.mean(x * x, axis=-1, keepdims=True)
        parts.append(x * lax.rsqrt(ms + EPS) * mnorm_ref[:, hd * M_HD:(hd + 1) * M_HD])
    ym = _sigmoid(proj(GCOL_MO, BRANCH_W)) * jnp.concatenate(parts, axis=1)
    ys = (ya.astype(F32), yw.astype(F32), yg.astype(F32), ym)
    acc = None
    for k in range(N_BRANCH):
        yk = (ys[k] * _silu(proj(GCOL_BRANCH_GATE[k], BRANCH_W))).astype(BF16)
        bp = jnp.dot(yk, wbr_ref[k], preferred_element_type=F32)
        g = _sigmoid(proj(GCOL_MERGE + k * D_MODEL, D_MODEL))
        acc = g * bp if acc is None else acc + g * bp
    y = jnp.dot(acc.astype(BF16), wo_ref[...], preferred_element_type=F32)
    ms = jnp.mean(y * y, axis=-1, keepdims=True)
    yn = y * lax.rsqrt(ms + EPS) * gpost_ref[...]
    o_ref[0] = h + gate * yn


def _merge(ya, yw, yg, hf, hb, h, mod8, wg_bf, wbr_bf, wo_bf, g_post, m_norm, *, context):
    tm = CTX_LEN
    B, rows, _ = h.shape
    nt = 1 if context else 2
    n_steps = rows // (nt * tm)
    stream_off = 0 if context else CTX_LEN // tm
    bw = BRANCH_W

    def tiles(w, off):
        return [pl.BlockSpec((1, tm, w), lambda b, t, i=i: (b, nt * t + i + off, 0)) for i in range(nt)]

    def const(shape):
        nd = len(shape)
        return pl.BlockSpec(shape, lambda b, t: (0,) * nd)

    token_specs = (tiles(bw, 0) + tiles(bw, 0) + tiles(bw, 0) + tiles(bw, stream_off) + tiles(bw, stream_off)
                   + tiles(D_MODEL, 0))
    token_args = [ya] * nt + [yw] * nt + [yg] * nt + [hf] * nt + [hb] * nt + [h] * nt
    return pl.pallas_call(
        functools.partial(_merge_kernel, nt=nt, mod_row=0 if context else 3),
        grid=(B, n_steps),
        in_specs=token_specs + [
            pl.BlockSpec((1, MOD_ROWS, D_MODEL), lambda b, t: (b, 0, 0)),
            const((D_MODEL, N_GATE)), const((N_BRANCH, bw, D_MODEL)), const((D_MODEL, D_MODEL)),
            const((1, D_MODEL)), const((1, bw))],
        out_specs=pl.BlockSpec((1, nt * tm, D_MODEL), lambda b, t: (b, t, 0)),
        out_shape=jax.ShapeDtypeStruct((B, rows, D_MODEL), F32),
        compiler_params=pltpu.CompilerParams(dimension_semantics=("parallel", "parallel"),
                                             vmem_limit_bytes=BIG_VMEM_LIMIT),
        name="merge_ctx" if context else "merge",
    )(*token_args, mod8, wg_bf, wbr_bf, wo_bf, g_post, m_norm)


def _rope_tables(n_lat):
    rows = n_lat // GRID_W
    row = jnp.repeat(jnp.arange(rows), GRID_W).astype(F32)
    col = jnp.tile(jnp.arange(GRID_W), rows).astype(F32)
    n_freq = HEAD_DIM // 4
    inv = ROPE_THETA ** (-jnp.arange(n_freq, dtype=F32) / n_freq)
    ang = jnp.concatenate([row[:, None] * inv, col[:, None] * inv], axis=-1)
    cos, sin = jnp.cos(ang), jnp.sin(ang)
    cos_h = jnp.concatenate([cos, cos], axis=-1)
    sin_h = jnp.concatenate([-sin, sin], axis=-1)
    cos_t = jnp.concatenate([jnp.ones((CTX_LEN, HEAD_DIM), F32), cos_h], axis=0)
    sin_t = jnp.concatenate([jnp.zeros((CTX_LEN, HEAD_DIM), F32), sin_h], axis=0)
    return jnp.tile(cos_t, (1, LANES // HEAD_DIM)), jnp.tile(sin_t, (1, LANES // HEAD_DIM))


def _gather_cols(w, names):
    sl = _ref_slices()
    runs = []
    for n in names:
        start, size = sl[n]
        if runs and runs[-1][1] == start:
            runs[-1][1] = start + size
        else:
            runs.append([start, start + size])
    return jnp.concatenate([w[:, a:b].astype(BF16) for a, b in runs], axis=1)


def kernel(x, c, ctx, c_ctx, w_mod, b_mod, g_pre, g_post, w_in, a_q_norm, a_k_norm, w_sink, sg_ln_g, sg_ln_b,
           sg_w, sg_b, m_conv, m_b_i, m_b_f, m_norm, w_branch, w_out):
    B, n_lat, D = x.shape
    depth = w_mod.shape[0]
    assert depth == 2, "the context stream is only carried from the first layer to the last"
    T = CTX_LEN + n_lat
    n_ctx_chunks = CTX_LEN // CHUNK
    n_lat_chunks = n_lat // CHUNK
    rope = _rope_tables(n_lat)
    gmat = (jnp.arange(LANES)[:, None] // HEAD_DIM == jnp.arange(LANES)[None, :] // HEAD_DIM).astype(BF16)

    hc, hx = ctx, x
    mod_rows = 16
    cc = jnp.zeros((mod_rows, D), F32).at[:B].set(c).at[B].set(c_ctx)
    tm_tok = CTX_LEN
    n_ctx_tiles = CTX_LEN // tm_tok
    n_lat_tiles = n_lat // tm_tok

    for l in range(depth):
        need_ctx = l < depth - 1
        mod = _modulation(cc, w_mod[l].astype(BF16), b_mod[l][None, :])
        shift, scale, gate = mod[:, :D], mod[:, D:2 * D], mod[:, 2 * D:]
        mult = g_pre[l][None, :] * (1.0 + scale)
        zero = jnp.zeros((B, D), F32)
        mod8 = jnp.stack([jnp.broadcast_to(mult[B], (B, D)), jnp.broadcast_to(shift[B], (B, D)),
                          jnp.broadcast_to(gate[B], (B, D)), mult[:B], shift[:B], gate[:B], zero, zero],
                         axis=1)

        w_gate = _gather_cols(w_in[l], _GATE_ORDER)
        p_main, p_gates = _in_projection(hc, hx, mod8, w_in, l)

        qn = jnp.tile(a_q_norm[l], LANES // HEAD_DIM)[None, :]
        kn = jnp.tile(a_k_norm[l], LANES // HEAD_DIM)[None, :]
        ya, yw = _attention(p_main, rope, qn, kn, gmat, w_sink[l],
                            q_tile0=n_ctx_chunks, n_q_tiles=n_lat_chunks, nk=T, banded=True)
        if need_ctx:
            ya_c, yw_c = _attention(p_main, rope, qn, kn, gmat, w_sink[l],
                                    q_tile0=0, n_q_tiles=n_ctx_chunks, nk=CTX_LEN, banded=False)

        sg_args = (p_main, sg_ln_g[l][None, :], sg_ln_b[l][None, :], sg_w[l].astype(BF16), sg_b[l].T)
        yg = _spatial_gating(*sg_args, tile0=n_ctx_tiles, n_tiles=n_lat_tiles, nt=4)
        if need_ctx:
            yg_c = _spatial_gating(*sg_args, tile0=0, n_tiles=n_ctx_tiles, nt=n_ctx_tiles)

        zq, zkt, zv = _mlstm_conv(p_main, m_conv[l])
        gates_row = jnp.swapaxes(p_gates[:, :, :2 * N_CHAINS], 1, 2)
        bias_row = jnp.broadcast_to(
            jnp.concatenate([m_b_i[l].reshape(-1), m_b_f[l].reshape(-1)])[:, None], (2 * N_CHAINS, T))
        ct, ur, ar = _mlstm_gates(gates_row, bias_row)
        hf, hb = _mlstm_scan(zq, zkt, zv, ct, ur, ar)

        weights = (mod8, w_gate, w_branch[l].astype(BF16), w_out[l].astype(BF16),
                   g_post[l][None, :], m_norm[l][None, :])
        hx_new = _merge(ya, yw, yg, hf, hb, hx, *weights, context=False)
        if need_ctx:
            hc = _merge(ya_c, yw_c, yg_c, hf, hb, hc, *weights, context=True)
        hx = hx_new
    return hx
```

```python
import functools

import jax
import jax.numpy as jnp
from jax import lax
from jax.experimental import pallas as pl
from jax.experimental.pallas import tpu as pltpu

F32 = jnp.float32
BF16 = jnp.bfloat16

D_MODEL = 1024
GRID_W = 64
CTX_LEN = 256
N_BRANCH = 4
BRANCH_W = 512
HEAD_DIM = 64
N_HEADS = 8
N_KV = 2
GROUP = N_HEADS // N_KV
WINDOW = 128
CHUNK = 128
B_GROUPS = 4
M_HEADS = 4
M_HD = 128
N_CHAINS = 2 * M_HEADS
ROPE_THETA = 10000.0
EPS = 1e-6
LOG2E = 1.4426950408889634
ATTN_TILES_PER_STEP = 2

LANES = 128
VMEM_LIMIT = 48 * 1024 * 1024
BIG_VMEM_LIMIT = 58 * 1024 * 1024

COL_AQ = 0
COL_WQ = 512
COL_AK = 1024
COL_AV = 1152
COL_WK = 1280
COL_WV = 1408
COL_GU = 1536
COL_GV = 2048
COL_MQKV = 2560
N_PLAIN = COL_MQKV
N_MAIN = 4096
GCOL_BRANCH_GATE = (0, 512, 1024, 2048)
GCOL_MO = 1536
GCOL_MERGE = 2560
N_GATE = GCOL_MERGE + N_BRANCH * D_MODEL

_REF_LAYOUT = (
    ('a_q', 512), ('a_k', 128), ('a_v', 128), ('a_gate', 512),
    ('w_q', 512), ('w_k', 128), ('w_v', 128), ('w_gate', 512),
    ('g_u', 512), ('g_v', 512), ('g_gate', 512),
    ('m_qkv', 1536), ('m_i', 8), ('m_f', 8), ('m_o', 512), ('m_gate', 512),
    ('merge', 4096),
)
_MAIN_ORDER = ('a_q', 'w_q', 'a_k', 'a_v', 'w_k', 'w_v', 'g_u', 'g_v', 'm_qkv')
_GATE_ORDER = ('a_gate', 'w_gate', 'g_gate', 'm_o', 'm_gate', 'merge')


def _ref_slices():
    out, start = {}, 0
    for name, size in _REF_LAYOUT:
        out[name] = (start, size)
        start += size
    return out


def _params(*sem):
    return pltpu.CompilerParams(dimension_semantics=sem, vmem_limit_bytes=VMEM_LIMIT)


def _sigmoid(x):
    return 1.0 / (1.0 + jnp.exp2(x * (-LOG2E)))


def _silu(x):
    return x * _sigmoid(x)


def _gelu_tanh(x):
    c = 0.7978845608028654
    return 0.5 * x * (1.0 + jnp.tanh(c * (x + 0.044715 * (x * x * x))))


def _log_sigmoid(x):
    return -(jnp.maximum(-x, 0.0) + jnp.log1p(jnp.exp(-jnp.abs(x))))


def _mod_kernel(c_ref, w_ref, b_ref, o_ref):
    a = _silu(c_ref[...]).astype(BF16)
    o_ref[...] = jnp.dot(a, w_ref[...], preferred_element_type=F32) + b_ref[...]


def _modulation(cc, w_mod_bf, b_mod):
    rows = cc.shape[0]
    n = w_mod_bf.shape[1]
    tn = 1024
    return pl.pallas_call(
        _mod_kernel,
        grid=(n // tn,),
        in_specs=[pl.BlockSpec((rows, D_MODEL), lambda j: (0, 0)),
                  pl.BlockSpec((D_MODEL, tn), lambda j: (0, j)),
                  pl.BlockSpec((1, tn), lambda j: (0, j))],
        out_specs=pl.BlockSpec((rows, tn), lambda j: (0, j)),
        out_shape=jax.ShapeDtypeStruct((rows, n), F32),
        compiler_params=_params("parallel"),
        name="modulation",
    )(cc, w_mod_bf, b_mod)


MOD_ROWS = 8


def _norm_mod(x, mult, shift):
    ms = jnp.mean(x * x, axis=-1, keepdims=True)
    return (x * lax.rsqrt(ms + EPS) * mult + shift).astype(BF16)


def _conv_silu_slab(x, w_ref, col, no_prev, no_next):
    T = x.shape[0]
    prev = jnp.where(no_prev, 0.0, pltpu.roll(x, 1, axis=0))
    nxt = jnp.where(no_next, 0.0, pltpu.roll(x, T - 1, axis=0))
    return _silu(prev * w_ref[0:1, col:col + LANES] + x * w_ref[1:2, col:col + LANES]
                 + nxt * w_ref[2:3, col:col + LANES])


def _inproj_kernel(hc_ref, hx_ref, mod_ref, w_ref, wl_ref, cw_ref,
                   o_ref, og_ref, zq_ref, zkt_ref, zv_ref, xn_ref, *, n_lat, n_plain):
    j = pl.program_id(1)
    rb = CTX_LEN
    T = rb + n_lat

    @pl.when(j == 0)
    def _():
        xn_ref[0:rb, :] = _norm_mod(hc_ref[0], mod_ref[0, 0:1, :], mod_ref[0, 1:2, :])

        def norm_rows(r, carry):
            off = pl.multiple_of(r * rb, rb)
            x = hx_ref[0, pl.ds(off, rb), :]
            xn_ref[pl.ds(rb + off, rb), :] = _norm_mod(x, mod_ref[0, 3:4, :], mod_ref[0, 4:5, :])
            return carry

        lax.fori_loop(0, n_lat // rb, norm_rows, 0)
        og_ref[0] = jnp.dot(xn_ref[...], wl_ref[...], preferred_element_type=F32)

    @pl.when(j < n_plain)
    def _():
        o_ref[0] = jnp.dot(xn_ref[...], w_ref[...], preferred_element_type=F32).astype(o_ref.dtype)

    for part, z_ref in enumerate((zq_ref, zkt_ref, zv_ref)):
        @pl.when(j == n_plain + part)
        def _(part=part, z_ref=z_ref):
            row = lax.broadcasted_iota(jnp.int32, (T, 1), 0)
            no_prev = jnp.logical_or(row == 0, row == CTX_LEN)
            no_next = jnp.logical_or(row == CTX_LEN - 1, row == T - 1)
            mxu_w = 2 * LANES
            ys = [jnp.dot(xn_ref[...], w_ref[:, lo:lo + mxu_w], preferred_element_type=F32)
                  for lo in range(0, BRANCH_W, mxu_w)]
            for s in range(BRANCH_W // LANES):
                lo = s * LANES
                y = ys[lo // mxu_w][:, lo % mxu_w:lo % mxu_w + LANES]
                z = _conv_silu_slab(y, cw_ref, part * BRANCH_W + lo, no_prev, no_next)
                if part == 0:
                    z_ref[0, :, lo:lo + LANES] = (z * (M_HD ** -0.5)).astype(z_ref.dtype)
                elif part == 1:
                    z_ref[0, lo:lo + LANES, :] = z.T.astype(z_ref.dtype)
                else:
                    z_ref[0, :, lo:lo + LANES] = z.astype(z_ref.dtype)


def _in_projection(hc, hx, mod8, w_bf, wl_bf, conv_w):
    B, n_lat, _ = hx.shape
    T = CTX_LEN + n_lat
    tn = BRANCH_W
    n_plain = N_PLAIN // tn
    n_steps = N_MAIN // tn
    assert n_steps == n_plain + 3

    def resident(shape):
        return pl.BlockSpec(shape, lambda b, j: (b, 0, 0), pipeline_mode=pl.Buffered(1))

    h_specs = [pl.BlockSpec((1, a.shape[1], D_MODEL), lambda b, j: (b, 0, 0)) for a in (hc, hx)]
    row_major = jax.ShapeDtypeStruct((B, T, BRANCH_W), BF16)
    return pl.pallas_call(
        functools.partial(_inproj_kernel, n_lat=n_lat, n_plain=n_plain),
        grid=(B, n_steps),
        in_specs=h_specs + [pl.BlockSpec((1, MOD_ROWS, D_MODEL), lambda b, j: (b, 0, 0)),
                            pl.BlockSpec((D_MODEL, tn), lambda b, j: (0, j)),
                            pl.BlockSpec((D_MODEL, LANES), lambda b, j: (0, 0)),
                            pl.BlockSpec((3, 3 * BRANCH_W), lambda b, j: (0, 0))],
        out_specs=[pl.BlockSpec((1, T, tn), lambda b, j: (b, 0, jnp.minimum(j, n_plain - 1))),
                   resident((1, T, LANES)),
                   resident((1, T, BRANCH_W)), resident((1, BRANCH_W, T)), resident((1, T, BRANCH_W))],
        out_shape=[jax.ShapeDtypeStruct((B, T, N_PLAIN), BF16),
                   jax.ShapeDtypeStruct((B, T, LANES), F32),
                   row_major, jax.ShapeDtypeStruct((B, BRANCH_W, T), BF16), row_major],
        scratch_shapes=[pltpu.VMEM((T, D_MODEL), BF16)],
        compiler_params=pltpu.CompilerParams(dimension_semantics=("parallel", "arbitrary"),
                                             vmem_limit_bytes=BIG_VMEM_LIMIT),
        name="in_projection",
    )(hc, hx, mod8, w_bf, wl_bf, conv_w)


def _rope_slab(x, cos, sin):
    lane = lax.broadcasted_iota(jnp.int32, x.shape, 1)
    first_half = (lane & (HEAD_DIM - 1)) < (HEAD_DIM // 2)
    partner = jnp.where(first_half,
                        pltpu.roll(x, LANES - HEAD_DIM // 2, axis=1),
                        pltpu.roll(x, HEAD_DIM // 2, axis=1))
    return x * cos + partner * sin


def _head_norm_slab(x, gain, gmat):
    ss = jnp.dot((x * x).astype(BF16), gmat, preferred_element_type=F32)
    return x * lax.rsqrt(ss * (1.0 / HEAD_DIM) + EPS) * gain


def _prep_kv(k_ref, v_ref, cosk_ref, sink_ref, kn_ref, gmat_ref, kp_ref, vt_ref, nk, use_norm):
    k = k_ref[0, 0:nk, :].astype(F32)
    if use_norm:
        k = _head_norm_slab(k, kn_ref[...], gmat_ref[...])
    k = _rope_slab(k, cosk_ref[0:nk, :], sink_ref[0:nk, :]).astype(BF16)
    for g in range(N_KV):
        kp_ref[g, 0:nk, :] = k[:, g * HEAD_DIM:(g + 1) * HEAD_DIM]
    for blk in range(nk // CHUNK):
        v = v_ref[0, blk * CHUNK:(blk + 1) * CHUNK, :].astype(F32)
        vt_ref[blk] = v.T.astype(BF16)


def _prep_qt(q_ref, cosq_ref, sinq_ref, qn_ref, gmat_ref, use_norm, r0, tq):
    cos = cosq_ref[r0:r0 + tq, :]
    sin = sinq_ref[r0:r0 + tq, :]
    xts = []
    for s in range(N_HEADS // 2):
        x = q_ref[0, r0:r0 + tq, s * LANES:(s + 1) * LANES].astype(F32)
        if use_norm:
            x = _head_norm_slab(x, qn_ref[...], gmat_ref[...])
        x = _rope_slab(x, cos, sin) * (HEAD_DIM ** -0.5 * LOG2E)
        xts.append(x.T.astype(BF16))
    per = GROUP // 2
    out = []
    for g in range(N_KV):
        parts = []
        for s in range(g * per, (g + 1) * per):
            parts.append(xts[s][:HEAD_DIM, :])
            parts.append(xts[s][HEAD_DIM:, :])
        out.append(jnp.concatenate(parts, axis=1))
    return out


def _heads_from_t(ot, tq):
    slabs = []
    for j in range(GROUP // 2):
        two = jnp.concatenate([ot[:, (2 * j) * tq:(2 * j + 1) * tq],
                               ot[:, (2 * j + 1) * tq:(2 * j + 2) * tq]], axis=0)
        slabs.append(two.T)
    return jnp.concatenate(slabs, axis=1)


def _window_pieces(wsink_ref, q_ref, cosq_ref, sinq_ref, o_ref, kp_ref, vt_ref, step, *,
                   tq, nsub, banded, n_lat_blocks):
    cols = GROUP * tq
    gw = GROUP * HEAD_DIM
    n_ctx_chunks = CTX_LEN // CHUNK
    lane = lax.broadcasted_iota(jnp.int32, (1, cols), 1)
    sink_rows = []
    for g in range(N_KV):
        sink_row = jnp.zeros((1, cols), F32)
        for h in range(GROUP):
            in_head = jnp.logical_and(lane >= h * tq, lane < (h + 1) * tq)
            sink_row = jnp.where(in_head, wsink_ref[g * GROUP + h] * LOG2E, sink_row)
        sink_rows.append(sink_row)
    if banded:
        kk = lax.broadcasted_iota(jnp.int32, (CHUNK, cols), 0)
        qq = lax.broadcasted_iota(jnp.int32, (CHUNK, cols), 1) & (tq - 1)

    for sub in range(nsub):
        i = step * nsub + sub
        qts = _prep_qt(q_ref, cosq_ref, sinq_ref, None, None, False, sub * tq, tq)
        yield
        if banded:
            prev_blk = i + n_ctx_chunks - 1
            own_blk = i + n_ctx_chunks
            next_blk = jnp.minimum(i + n_ctx_chunks + 1, n_ctx_chunks + n_lat_blocks - 1)
            ok_prev = jnp.logical_and(kk >= qq, i > 0)
            ok_next = jnp.logical_and(kk <= qq, i < n_lat_blocks - 1)
            plan = ([(cb, None) for cb in range(n_ctx_chunks)]
                    + [(prev_blk, ok_prev), (own_blk, None), (next_blk, ok_next)])
        else:
            plan = [(cb, None) for cb in range(n_ctx_chunks)]

        for g in range(N_KV):
            qt = qts[g]
            sink_row = sink_rows[g]
            blocks = []
            for blk, ok in plan:
                off = blk * CHUNK
                if not isinstance(off, int):
                    off = pl.multiple_of(off, CHUNK)
                st = jnp.dot(kp_ref[g, pl.ds(off, CHUNK), :], qt, preferred_element_type=F32)
                blocks.append((st if ok is None else jnp.where(ok, st, -jnp.inf), blk))
                yield
            m = sink_row
            for st, _ in blocks:
                m = jnp.maximum(m, jnp.max(st, axis=0, keepdims=True))
            l = jnp.exp2(sink_row - m)
            acc = None
            for st, blk in blocks:
                p = jnp.exp2(st - m)
                l = l + jnp.sum(p, axis=0, keepdims=True)
                pv = jnp.dot(vt_ref[blk, g * HEAD_DIM:(g + 1) * HEAD_DIM, :], p.astype(BF16),
                             preferred_element_type=F32)
                acc = pv if acc is None else acc + pv
                yield
            o_ref[0, sub * tq:(sub + 1) * tq, g * gw:(g + 1) * gw] = (
                _heads_from_t(acc / l, tq).astype(o_ref.dtype))
            yield


def _attn_kernel(wsink_ref, qa_ref, ka_ref, va_ref, qw_ref, kw_ref, vw_ref, cosq_ref, sinq_ref,
                 cosk_ref, sink_ref, qn_ref, kn_ref, gmat_ref, oa_ref, ow_ref,
                 kpa_ref, vta_ref, s_ref, kpw_ref, vtw_ref, *, tq, nsub, nk, ck, banded, n_lat_blocks):
    step = pl.program_id(1)

    @pl.when(step == 0)
    def _():
        _prep_kv(ka_ref, va_ref, cosk_ref, sink_ref, kn_ref, gmat_ref, kpa_ref, vta_ref, nk, True)
        _prep_kv(kw_ref, vw_ref, cosk_ref, sink_ref, None, None, kpw_ref, vtw_ref, nk, False)

    window = _window_pieces(wsink_ref, qw_ref, cosq_ref, sinq_ref, ow_ref, kpw_ref, vtw_ref, step,
                            tq=tq, nsub=nsub, banded=banded, n_lat_blocks=n_lat_blocks)

    qts = [_prep_qt(qa_ref, cosq_ref, sinq_ref, qn_ref, gmat_ref, True, sub * tq, tq) for sub in range(nsub)]
    items = [(sub, g) for sub in range(nsub) for g in range(N_KV)]
    n_chunks = nk // ck
    blocks_per_chunk = ck // CHUNK
    gw = GROUP * HEAD_DIM

    def score_chunk(k, c, m):
        sub, g = items[k]
        st = jnp.dot(kpa_ref[g, c * ck:(c + 1) * ck, :], qts[sub][g], preferred_element_type=F32)
        s_ref[k, c * ck:(c + 1) * ck, :] = st
        cm = jnp.max(st, axis=0, keepdims=True)
        return cm if m is None else jnp.maximum(m, cm)

    def value_chunk(k, c, m, l, acc):
        _, g = items[k]
        p = jnp.exp2(s_ref[k, c * ck:(c + 1) * ck, :] - m)
        cl = jnp.sum(p, axis=0, keepdims=True)
        vt = jnp.concatenate(
            [vta_ref[c * blocks_per_chunk + j, g * HEAD_DIM:(g + 1) * HEAD_DIM, :]
             for j in range(blocks_per_chunk)], axis=1)
        pv = jnp.dot(vt, p.astype(BF16), preferred_element_type=F32)
        return (cl, pv) if l is None else (l + cl, acc + pv)

    m_next = None
    for c in range(n_chunks):
        m_next = score_chunk(0, c, m_next)
        next(window, None)
    for k, (sub, g) in enumerate(items):
        m, m_next = m_next, None
        l = acc = None
        for c in range(n_chunks):
            l, acc = value_chunk(k, c, m, l, acc)
            if k + 1 < len(items):
                m_next = score_chunk(k + 1, c, m_next)
            next(window, None)
        oa_ref[0, sub * tq:(sub + 1) * tq, g * gw:(g + 1) * gw] = _heads_from_t(acc / l, tq).astype(oa_ref.dtype)
    for _ in window:
        pass


def _attention(p_main, rope, qn, kn, gmat, w_sink, *, q_tile0, n_q_tiles, nk, banded):
    B, T, _ = p_main.shape
    cos, sin = rope
    tq = CHUNK
    nsub = ATTN_TILES_PER_STEP
    tb = nsub * tq
    blk0 = q_tile0 // nsub
    ck = 256
    n_lat_blocks = (T - CTX_LEN) // CHUNK
    kernel = functools.partial(_attn_kernel, tq=tq, nsub=nsub, nk=nk, ck=ck, banded=banded,
                               n_lat_blocks=n_lat_blocks)

    def q_spec(col):
        return pl.BlockSpec((1, tb, BRANCH_W), lambda b, i: (b, i + blk0, col // BRANCH_W))

    def kv_spec(col):
        return pl.BlockSpec((1, T, LANES), lambda b, i: (b, 0, col // LANES))

    def const(shape):
        return pl.BlockSpec(shape, lambda b, i: (0, 0))

    out = jax.ShapeDtypeStruct((B, n_q_tiles * tq, BRANCH_W), BF16)
    kv_scratch = [pltpu.VMEM((N_KV, nk, HEAD_DIM), BF16), pltpu.VMEM((nk // CHUNK, LANES, CHUNK), BF16)]
    return pl.pallas_call(
        kernel,
        grid=(B, n_q_tiles // nsub),
        in_specs=[pl.BlockSpec(memory_space=pltpu.SMEM),
                  q_spec(COL_AQ), kv_spec(COL_AK), kv_spec(COL_AV),
                  q_spec(COL_WQ), kv_spec(COL_WK), kv_spec(COL_WV),
                  pl.BlockSpec((tb, LANES), lambda b, i: (i + blk0, 0)),
                  pl.BlockSpec((tb, LANES), lambda b, i: (i + blk0, 0)),
                  const((T, LANES)), const((T, LANES)),
                  const((1, LANES)), const((1, LANES)), const((LANES, LANES))],
        out_specs=[pl.BlockSpec((1, tb, BRANCH_W), lambda b, i: (b, i, 0)),
                   pl.BlockSpec((1, tb, BRANCH_W), lambda b, i: (b, i, 0))],
        out_shape=[out, out],
        scratch_shapes=kv_scratch + [pltpu.VMEM((nsub * N_KV, nk, GROUP * tq), F32)] + kv_scratch,
        compiler_params=_params("parallel", "arbitrary"),
        name="attention",
    )(w_sink, p_main, p_main, p_main, p_main, p_main, p_main, cos, sin, cos, sin, qn, kn, gmat)


def _spatial_gate_tile(gu, gv, lng_ref, lnb_ref, ws_ref, bs_ref):
    u = _gelu_tanh(gu.astype(F32))
    v = _gelu_tanh(gv.astype(F32))
    mu = jnp.mean(v, axis=-1, keepdims=True)
    vc = v - mu
    var = jnp.mean(vc * vc, axis=-1, keepdims=True)
    vn = (vc * lax.rsqrt(var + EPS) * lng_ref[...] + lnb_ref[...]).astype(BF16)
    gw = BRANCH_W // B_GROUPS
    chunks = []
    for c in range(gu.shape[0] // CHUNK):
        cols = []
        for g in range(B_GROUPS):
            mixed = jnp.dot(ws_ref[g], vn[c * CHUNK:(c + 1) * CHUNK, g * gw:(g + 1) * gw],
                            preferred_element_type=F32)
            cols.append(mixed + bs_ref[:, g:g + 1])
        chunks.append(jnp.concatenate(cols, axis=1))
    return u * jnp.concatenate(chunks, axis=0)


def _split3(x):
    x1 = x.astype(BF16)
    r = x - x1.astype(F32)
    x2 = r.astype(BF16)
    x3 = (r - x2.astype(F32)).astype(BF16)
    return x1, x2, x3


def _mgates_kernel(g_ref, bias_ref, ct_ref, ur_ref, ar_ref, *, T):
    L = CHUNK
    n_chunks = T // L
    n_ctx = CTX_LEN // L
    gall = g_ref[0] + bias_ref[...]
    i_all = gall[0:N_CHAINS]
    f_all = _log_sigmoid(gall[N_CHAINS:2 * N_CHAINS])
    fwd_rows = lax.broadcasted_iota(jnp.int32, (N_CHAINS, L), 0) < M_HEADS
    fwd_col = lax.broadcasted_iota(jnp.int32, (N_CHAINS, 1), 0) < M_HEADS
    lane = lax.broadcasted_iota(jnp.int32, (N_CHAINS, L), 1)
    s_idx = lax.broadcasted_iota(jnp.int32, (L, L), 0)
    t_idx = lax.broadcasted_iota(jnp.int32, (L, L), 1)
    sum_upto = jnp.where(s_idx <= t_idx, 1.0, 0.0).astype(BF16)
    sum_from = jnp.where(s_idx >= t_idx, 1.0, 0.0).astype(BF16)

    def cumsum_dir(f):
        bf = None
        bb = None
        for part in _split3(f):
            a = jnp.dot(part, sum_upto, preferred_element_type=F32)
            b = jnp.dot(part, sum_from, preferred_element_type=F32)
            bf = a if bf is None else bf + a
            bb = b if bb is None else bb + b
        return jnp.where(fwd_rows, bf, bb)

    def cummax_dir(u):
        xf = u
        xb = u
        sh = 1
        while sh < L:
            xf = jnp.maximum(xf, jnp.where(lane >= sh, pltpu.roll(xf, sh, axis=1), -jnp.inf))
            xb = jnp.maximum(xb, jnp.where(lane < L - sh, pltpu.roll(xb, L - sh, axis=1), -jnp.inf))
            sh *= 2
        return jnp.where(fwd_rows, xf, xb)

    b_c, u_c, mloc_c, bl_c, g_c, gmax_c = [], [], [], [], [], []
    for c in range(n_chunks):
        f = f_all[:, c * L:(c + 1) * L]
        i = i_all[:, c * L:(c + 1) * L]
        b = cumsum_dir(f)
        u = i - b
        bl = jnp.sum(f, axis=1, keepdims=True)
        g = bl + u
        b_c.append(b)
        u_c.append(u)
        mloc_c.append(b + cummax_dir(u))
        bl_c.append(bl)
        g_c.append(g)
        gmax_c.append(jnp.max(g, axis=1, keepdims=True))

    def scan(order):
        m = jnp.zeros((N_CHAINS, 1), F32)
        prev, new = {}, {}
        for c in order:
            prev[c] = m
            m = jnp.maximum(bl_c[c] + m, gmax_c[c])
            new[c] = m
        return prev, new

    order_f = list(range(n_chunks))
    order_b = [n_ctx - 1 - j for j in range(n_ctx)] + [n_chunks + n_ctx - 1 - j for j in range(n_ctx, n_chunks)]
    prev_f, new_f = scan(order_f)
    prev_b, new_b = scan(order_b)

    for c in range(n_chunks):
        m_prev = jnp.where(fwd_col, prev_f[c], prev_b[c])
        m_new = jnp.where(fwd_col, new_f[c], new_b[c])
        b = b_c[c]
        m_inter = b + m_prev
        m_t = jnp.maximum(m_inter, mloc_c[c])
        pack = jnp.concatenate([b - m_t, jnp.exp(m_inter - m_t), jnp.exp(-m_t),
                                jnp.zeros((LANES - 3 * N_CHAINS, L), F32)], axis=0)
        ct_ref[0, c * L:(c + 1) * L, :] = pack.T
        ur_ref[0, :, c * L:(c + 1) * L] = jnp.concatenate([u_c[c], jnp.exp(g_c[c] - m_new)], axis=0)
        ar_ref[0, c] = jnp.broadcast_to(jnp.exp(bl_c[c] + m_prev - m_new), (N_CHAINS, LANES))


def _mlstm_gates(gates_row, bias_row):
    B, _, T = gates_row.shape
    n_chunks = T // CHUNK
    return pl.pallas_call(
        functools.partial(_mgates_kernel, T=T),
        grid=(B,),
        in_specs=[pl.BlockSpec((1, 2 * N_CHAINS, T), lambda b: (b, 0, 0)),
                  pl.BlockSpec((2 * N_CHAINS, T), lambda b: (0, 0))],
        out_specs=[pl.BlockSpec((1, T, LANES), lambda b: (b, 0, 0)),
                   pl.BlockSpec((1, 2 * N_CHAINS, T), lambda b: (b, 0, 0)),
                   pl.BlockSpec((1, n_chunks, N_CHAINS, LANES), lambda b: (b, 0, 0, 0))],
        out_shape=[jax.ShapeDtypeStruct((B, T, LANES), F32),
                   jax.ShapeDtypeStruct((B, 2 * N_CHAINS, T), F32),
                   jax.ShapeDtypeStruct((B, n_chunks, N_CHAINS, LANES), F32)],
        compiler_params=_params("parallel"),
        name="mlstm_gates",
    )(gates_row, bias_row)


SCAN_BATCH = 2


def _mlstm_kernel(qf_ref, ktf_ref, vf_ref, ctf_ref, urf_ref, arf_ref,
                  qb_ref, ktb_ref, vb_ref, ctb_ref, urb_ref, arb_ref, of_ref, ob_ref, st_ref):
    L = CHUNK

    @pl.when(pl.program_id(1) == 0)
    def _():
        st_ref[...] = jnp.zeros_like(st_ref)

    t_idx = lax.broadcasted_iota(jnp.int32, (L, L), 0)
    s_idx = lax.broadcasted_iota(jnp.int32, (L, L), 1)
    ones = jnp.ones((L, M_HD), BF16)
    dirs = ((qf_ref, ktf_ref, vf_ref, ctf_ref, urf_ref, arf_ref, of_ref),
            (qb_ref, ktb_ref, vb_ref, ctb_ref, urb_ref, arb_ref, ob_ref))
    for bi in range(SCAN_BATCH):
        for d, (q_ref, kt_ref, v_ref, ct_ref, ur_ref, ar_ref, o_ref) in enumerate(dirs):
            mask = (s_idx <= t_idx) if d == 0 else (s_idx >= t_idx)
            ct = ct_ref[bi]
            ur = ur_ref[bi]
            ar = ar_ref[bi, 0]
            outs = []
            for h in range(M_HEADS):
                c = d * M_HEADS + h
                q = q_ref[bi, :, h * M_HD:(h + 1) * M_HD]
                kt = kt_ref[bi, h * M_HD:(h + 1) * M_HD, :]
                v = v_ref[bi, :, h * M_HD:(h + 1) * M_HD]
                v_ext = jnp.concatenate([v, ones], axis=1)
                cq = ct[:, c:c + 1]
                w_inter = ct[:, N_CHAINS + c:N_CHAINS + c + 1]
                e_negm = ct[:, 2 * N_CHAINS + c:2 * N_CHAINS + c + 1]
                wk_row = ur[N_CHAINS + c:N_CHAINS + c + 1, :]
                w_intra = jnp.exp(jnp.where(mask, cq + ur[c:c + 1, :], -jnp.inf))
                s = jnp.dot(q, kt, preferred_element_type=F32) * w_intra
                st_prev = st_ref[bi, d, h]
                lhs = jnp.concatenate([s.astype(BF16), (q.astype(F32) * w_inter).astype(BF16)], axis=1)
                rhs = jnp.concatenate([v_ext, st_prev.astype(BF16)], axis=0)
                tot = jnp.dot(lhs, rhs, preferred_element_type=F32)
                den = jnp.maximum(jnp.abs(tot[:, M_HD:]), e_negm)
                outs.append(tot[:, :M_HD] / den)
                kt_w = (kt.astype(F32) * wk_row).astype(BF16)
                a = jnp.concatenate([ar[c:c + 1, :], ar[c:c + 1, :]], axis=1)
                st_ref[bi, d, h] = a * st_prev + jnp.dot(kt_w, v_ext, preferred_element_type=F32)
            o_ref[0, bi] = jnp.concatenate(outs, axis=1).astype(o_ref.dtype)


def _mlstm_scan(zq, zkt, zv, ct, ur, ar):
    B, T, _ = zq.shape
    n_chunks = T // CHUNK
    n_ctx = CTX_LEN // CHUNK

    def fwd(j):
        return j

    def bwd(j):
        return jnp.where(j < n_ctx, n_ctx - 1 - j, n_chunks + n_ctx - 1 - j)

    nb = SCAN_BATCH

    def specs(order):
        return [pl.BlockSpec((nb, CHUNK, BRANCH_W), lambda b, j: (b, order(j), 0)),
                pl.BlockSpec((nb, BRANCH_W, CHUNK), lambda b, j: (b, 0, order(j))),
                pl.BlockSpec((nb, CHUNK, BRANCH_W), lambda b, j: (b, order(j), 0)),
                pl.BlockSpec((nb, CHUNK, LANES), lambda b, j: (b, order(j), 0)),
                pl.BlockSpec((nb, 2 * N_CHAINS, CHUNK), lambda b, j: (b, 0, order(j))),
                pl.BlockSpec((nb, 1, N_CHAINS, LANES), lambda b, j: (b, order(j), 0, 0))]

    out = pl.pallas_call(
        _mlstm_kernel,
        grid=(B // nb, n_chunks),
        in_specs=specs(fwd) + specs(bwd),
        out_specs=[pl.BlockSpec((1, nb, CHUNK, BRANCH_W), lambda b, j: (0, b, fwd(j), 0)),
                   pl.BlockSpec((1, nb, CHUNK, BRANCH_W), lambda b, j: (0, b, bwd(j), 0))],
        out_shape=[jax.ShapeDtypeStruct((1, B, T, BRANCH_W), F32),
                   jax.ShapeDtypeStruct((1, B, T, BRANCH_W), F32)],
        scratch_shapes=[pltpu.VMEM((nb, 2, M_HEADS, M_HD, 2 * M_HD), F32)],
        compiler_params=_params("parallel", "arbitrary"),
        name="mlstm_scan",
    )(zq, zkt, zv, ct, ur, ar, zq, zkt, zv, ct, ur, ar)
    return out[0][0], out[1][0]


def _merge_kernel(*refs, nt, mod_row):
    groups = [refs[i * nt:(i + 1) * nt] for i in range(7)]
    ya_refs, yw_refs, gu_refs, gv_refs, hf_refs, hb_refs, h_refs = groups
    (mod_ref, wg_ref, wbr_ref, wo_ref, gpost_ref, mnorm_ref,
     lng_ref, lnb_ref, ws_ref, bs_ref, o_ref) = refs[7 * nt:]
    tm = CTX_LEN
    mult = mod_ref[0, mod_row:mod_row + 1, :]
    shift = mod_ref[0, mod_row + 1:mod_row + 2, :]
    gate = mod_ref[0, mod_row + 2:mod_row + 3, :]

    def rows(tile_refs):
        return jnp.concatenate([r[0] for r in tile_refs], axis=0)

    h = rows(h_refs)
    ya, yw = rows(ya_refs), rows(yw_refs)
    yg = jnp.concatenate([_spatial_gate_tile(gu_refs[i][0], gv_refs[i][0], lng_ref, lnb_ref, ws_ref, bs_ref)
                          for i in range(nt)], axis=0)
    xn = _norm_mod(h, mult, shift)

    def proj(col, width):
        return jnp.dot(xn, wg_ref[:, col:col + width], preferred_element_type=F32)

    hm = rows(hf_refs) + rows(hb_refs)
    parts = []
    for hd in range(M_HEADS):
        x = hm[:, hd * M_HD:(hd + 1) * M_HD]
        ms = jnp.mean(x * x, axis=-1, keepdims=True)
        parts.append(x * lax.rsqrt(ms + EPS) * mnorm_ref[:, hd * M_HD:(hd + 1) * M_HD])
    ym = _sigmoid(proj(GCOL_MO, BRANCH_W)) * jnp.concatenate(parts, axis=1)
    ys = (ya.astype(F32), yw.astype(F32), yg, ym)
    acc = None
    for k in range(N_BRANCH):
        yk = (ys[k] * _silu(proj(GCOL_BRANCH_GATE[k], BRANCH_W))).astype(BF16)
        bp = jnp.dot(yk, wbr_ref[k], preferred_element_type=F32)
        g = _sigmoid(proj(GCOL_MERGE + k * D_MODEL, D_MODEL))
        acc = g * bp if acc is None else acc + g * bp
    y = jnp.dot(acc.astype(BF16), wo_ref[...], preferred_element_type=F32)
    ms = jnp.mean(y * y, axis=-1, keepdims=True)
    yn = y * lax.rsqrt(ms + EPS) * gpost_ref[...]
    o_ref[0] = h + gate * yn


def _merge(ya, yw, p_main, hf, hb, h, mod8, wg_bf, wbr_bf, wo_bf, g_post, m_norm, ln_g, ln_b, ws_bf, bs_t,
           *, context):
    tm = CTX_LEN
    B, rows, _ = h.shape
    nt = 1 if context else 2
    n_steps = rows // (nt * tm)
    stream_off = 0 if context else CTX_LEN // tm
    bw = BRANCH_W

    def tiles(w, off, col=0):
        return [pl.BlockSpec((1, tm, w), lambda b, t, i=i: (b, nt * t + i + off, col)) for i in range(nt)]

    def const(shape):
        nd = len(shape)
        return pl.BlockSpec(shape, lambda b, t: (0,) * nd)

    token_specs = (tiles(bw, 0) + tiles(bw, 0)
                   + tiles(bw, stream_off, COL_GU // bw) + tiles(bw, stream_off, COL_GV // bw)
                   + tiles(bw, stream_off) + tiles(bw, stream_off) + tiles(D_MODEL, 0))
    token_args = [ya] * nt + [yw] * nt + [p_main] * (2 * nt) + [hf] * nt + [hb] * nt + [h] * nt
    return pl.pallas_call(
        functools.partial(_merge_kernel, nt=nt, mod_row=0 if context else 3),
        grid=(B, n_steps),
        in_specs=token_specs + [
            pl.BlockSpec((1, MOD_ROWS, D_MODEL), lambda b, t: (b, 0, 0)),
            const((D_MODEL, N_GATE)), const((N_BRANCH, bw, D_MODEL)), const((D_MODEL, D_MODEL)),
            const((1, D_MODEL)), const((1, bw)),
            const((1, bw)), const((1, bw)), const((B_GROUPS, CHUNK, CHUNK)), const((CHUNK, B_GROUPS))],
        out_specs=pl.BlockSpec((1, nt * tm, D_MODEL), lambda b, t: (b, t, 0)),
        out_shape=jax.ShapeDtypeStruct((B, rows, D_MODEL), F32),
        compiler_params=pltpu.CompilerParams(dimension_semantics=("parallel", "parallel"),
                                             vmem_limit_bytes=BIG_VMEM_LIMIT),
        name="merge_ctx" if context else "merge",
    )(*token_args, mod8, wg_bf, wbr_bf, wo_bf, g_post, m_norm, ln_g, ln_b, ws_bf, bs_t)


def _rope_tables(n_lat):
    rows = n_lat // GRID_W
    row = jnp.repeat(jnp.arange(rows), GRID_W).astype(F32)
    col = jnp.tile(jnp.arange(GRID_W), rows).astype(F32)
    n_freq = HEAD_DIM // 4
    inv = ROPE_THETA ** (-jnp.arange(n_freq, dtype=F32) / n_freq)
    ang = jnp.concatenate([row[:, None] * inv, col[:, None] * inv], axis=-1)
    cos, sin = jnp.cos(ang), jnp.sin(ang)
    cos_h = jnp.concatenate([cos, cos], axis=-1)
    sin_h = jnp.concatenate([-sin, sin], axis=-1)
    cos_t = jnp.concatenate([jnp.ones((CTX_LEN, HEAD_DIM), F32), cos_h], axis=0)
    sin_t = jnp.concatenate([jnp.zeros((CTX_LEN, HEAD_DIM), F32), sin_h], axis=0)
    return jnp.tile(cos_t, (1, LANES // HEAD_DIM)), jnp.tile(sin_t, (1, LANES // HEAD_DIM))


def _gather_cols(w, names):
    sl = _ref_slices()
    runs = []
    for n in names:
        start, size = sl[n]
        if runs and runs[-1][1] == start:
            runs[-1][1] = start + size
        else:
            runs.append([start, start + size])
    return jnp.concatenate([w[:, a:b].astype(BF16) for a, b in runs], axis=1)


def kernel(x, c, ctx, c_ctx, w_mod, b_mod, g_pre, g_post, w_in, a_q_norm, a_k_norm, w_sink, sg_ln_g, sg_ln_b,
           sg_w, sg_b, m_conv, m_b_i, m_b_f, m_norm, w_branch, w_out):
    B, n_lat, D = x.shape
    depth = w_mod.shape[0]
    assert depth == 2, "the context stream is only carried from the first layer to the last"
    T = CTX_LEN + n_lat
    n_ctx_chunks = CTX_LEN // CHUNK
    n_lat_chunks = n_lat // CHUNK
    rope = _rope_tables(n_lat)
    gmat = (jnp.arange(LANES)[:, None] // HEAD_DIM == jnp.arange(LANES)[None, :] // HEAD_DIM).astype(BF16)

    hc, hx = ctx, x
    mod_rows = 16
    cc = jnp.zeros((mod_rows, D), F32).at[:B].set(c).at[B].set(c_ctx)
    tm_tok = CTX_LEN
    n_ctx_tiles = CTX_LEN // tm_tok
    n_lat_tiles = n_lat // tm_tok

    for l in range(depth):
        need_ctx = l < depth - 1
        mod = _modulation(cc, w_mod[l].astype(BF16), b_mod[l][None, :])
        shift, scale, gate = mod[:, :D], mod[:, D:2 * D], mod[:, 2 * D:]
        mult = g_pre[l][None, :] * (1.0 + scale)
        zero = jnp.zeros((B, D), F32)
        mod8 = jnp.stack([jnp.broadcast_to(mult[B], (B, D)), jnp.broadcast_to(shift[B], (B, D)),
                          jnp.broadcast_to(gate[B], (B, D)), mult[:B], shift[:B], gate[:B], zero, zero],
                         axis=1)

        w_gate = _gather_cols(w_in[l], _GATE_ORDER)
        w_main = _gather_cols(w_in[l], _MAIN_ORDER)
        w_lstm = jnp.pad(_gather_cols(w_in[l], ('m_i', 'm_f')), ((0, 0), (0, LANES - 2 * N_CHAINS)))
        p_main, p_gates, zq, zkt, zv = _in_projection(hc, hx, mod8, w_main, w_lstm, m_conv[l])

        qn = jnp.tile(a_q_norm[l], LANES // HEAD_DIM)[None, :]
        kn = jnp.tile(a_k_norm[l], LANES // HEAD_DIM)[None, :]
        ya, yw = _attention(p_main, rope, qn, kn, gmat, w_sink[l],
                            q_tile0=n_ctx_chunks, n_q_tiles=n_lat_chunks, nk=T, banded=True)
        if need_ctx:
            ya_c, yw_c = _attention(p_main, rope, qn, kn, gmat, w_sink[l],
                                    q_tile0=0, n_q_tiles=n_ctx_chunks, nk=CTX_LEN, banded=False)

        gates_row = jnp.swapaxes(p_gates[:, :, :2 * N_CHAINS], 1, 2)
        bias_row = jnp.broadcast_to(
            jnp.concatenate([m_b_i[l].reshape(-1), m_b_f[l].reshape(-1)])[:, None], (2 * N_CHAINS, T))
        ct, ur, ar = _mlstm_gates(gates_row, bias_row)
        hf, hb = _mlstm_scan(zq, zkt, zv, ct, ur, ar)

        weights = (mod8, w_gate, w_branch[l].astype(BF16), w_out[l].astype(BF16),
                   g_post[l][None, :], m_norm[l][None, :],
                   sg_ln_g[l][None, :], sg_ln_b[l][None, :], sg_w[l].astype(BF16), sg_b[l].T)
        hx_new = _merge(ya, yw, p_main, hf, hb, hx, *weights, context=False)
        if need_ctx:
            hc = _merge(ya_c, yw_c, p_main, hf, hb, hc, *weights, context=True)
        hx = hx_new
    return hx
```

```python
import functools

import jax
import jax.numpy as jnp
from jax import lax
from jax.experimental import pallas as pl
from jax.experimental.pallas import tpu as pltpu

F32 = jnp.float32
BF16 = jnp.bfloat16

D_MODEL = 1024
GRID_W = 64
CTX_LEN = 256
N_BRANCH = 4
BRANCH_W = 512
HEAD_DIM = 64
N_HEADS = 8
N_KV = 2
GROUP = N_HEADS // N_KV
WINDOW = 128
CHUNK = 128
B_GROUPS = 4
M_HEADS = 4
M_HD = 128
N_CHAINS = 2 * M_HEADS
ROPE_THETA = 10000.0
EPS = 1e-6
LOG2E = 1.4426950408889634
ATTN_TILES_PER_STEP = 2

LANES = 128
VMEM_LIMIT = 48 * 1024 * 1024
BIG_VMEM_LIMIT = 58 * 1024 * 1024

COL_AQ = 0
COL_WQ = 512
COL_AK = 1024
COL_AV = 1152
COL_WK = 1280
COL_WV = 1408
COL_GU = 1536
COL_GV = 2048
COL_MQKV = 2560
N_PLAIN = COL_MQKV
N_MAIN = 4096
_REF_LAYOUT = (
    ('a_q', 512), ('a_k', 128), ('a_v', 128), ('a_gate', 512),
    ('w_q', 512), ('w_k', 128), ('w_v', 128), ('w_gate', 512),
    ('g_u', 512), ('g_v', 512), ('g_gate', 512),
    ('m_qkv', 1536), ('m_i', 8), ('m_f', 8), ('m_o', 512), ('m_gate', 512),
    ('merge', 4096),
)
_MAIN_ORDER = ('a_q', 'w_q', 'a_k', 'a_v', 'w_k', 'w_v', 'g_u', 'g_v', 'm_qkv')
_W2_ORDER = tuple(name for name, _ in _REF_LAYOUT if name not in ('m_i', 'm_f'))


def _ref_slices():
    out, start = {}, 0
    for name, size in _REF_LAYOUT:
        out[name] = (start, size)
        start += size
    return out


def _w2_cols():
    out, start = {}, 0
    for name in _W2_ORDER:
        out[name] = start
        start += _ref_slices()[name][1]
    out['end'] = start
    return out


GCOL_BRANCH_GATE = tuple(_w2_cols()[n] for n in ('a_gate', 'w_gate', 'g_gate', 'm_gate'))
GCOL_MO = _w2_cols()['m_o']
GCOL_MERGE = _w2_cols()['merge']
N_W2 = _w2_cols()['end']
assert all(c % LANES == 0 for c in GCOL_BRANCH_GATE + (GCOL_MO, GCOL_MERGE, N_W2))


def _params(*sem):
    return pltpu.CompilerParams(dimension_semantics=sem, vmem_limit_bytes=VMEM_LIMIT)


def _sigmoid(x):
    return 1.0 / (1.0 + jnp.exp2(x * (-LOG2E)))


def _silu(x):
    return x * _sigmoid(x)


def _gelu_tanh(x):
    c = 0.7978845608028654
    return 0.5 * x * (1.0 + jnp.tanh(c * (x + 0.044715 * (x * x * x))))


def _log_sigmoid(x):
    return -(jnp.maximum(-x, 0.0) + jnp.log1p(jnp.exp(-jnp.abs(x))))


def _mod_kernel(c_ref, w_ref, b_ref, o_ref):
    a = _silu(c_ref[...]).astype(BF16)
    o_ref[...] = jnp.dot(a, w_ref[...], preferred_element_type=F32) + b_ref[...]


def _modulation(cc, w_mod_bf, b_mod):
    rows = cc.shape[0]
    n = w_mod_bf.shape[1]
    tn = 1024
    return pl.pallas_call(
        _mod_kernel,
        grid=(n // tn,),
        in_specs=[pl.BlockSpec((rows, D_MODEL), lambda j: (0, 0)),
                  pl.BlockSpec((D_MODEL, tn), lambda j: (0, j)),
                  pl.BlockSpec((1, tn), lambda j: (0, j))],
        out_specs=pl.BlockSpec((rows, tn), lambda j: (0, j)),
        out_shape=jax.ShapeDtypeStruct((rows, n), F32),
        compiler_params=_params("parallel"),
        name="modulation",
    )(cc, w_mod_bf, b_mod)


MOD_ROWS = 8


def _norm_mod(x, mult, shift):
    ms = jnp.mean(x * x, axis=-1, keepdims=True)
    return (x * lax.rsqrt(ms + EPS) * mult + shift).astype(BF16)


def _conv_silu_slab(x, w_ref, col, no_prev, no_next):
    T = x.shape[0]
    prev = jnp.where(no_prev, 0.0, pltpu.roll(x, 1, axis=0))
    nxt = jnp.where(no_next, 0.0, pltpu.roll(x, T - 1, axis=0))
    return _silu(prev * w_ref[0:1, col:col + LANES] + x * w_ref[1:2, col:col + LANES]
                 + nxt * w_ref[2:3, col:col + LANES])


W_TILE = 2 * LANES


def _main_tiles():
    sl = _ref_slices()
    tiles = []
    for name in _MAIN_ORDER:
        start, size = sl[name]
        if size < W_TILE:
            if start % W_TILE == 0:
                tiles.append(start // W_TILE)
            continue
        assert start % W_TILE == 0 and size % W_TILE == 0
        tiles.extend(range(start // W_TILE, (start + size) // W_TILE))
    return tiles


def _inproj_kernel(tbl_ref, hc_ref, hx_ref, mod_ref, w0_ref, w1_ref, wl_ref, cw_ref,
                   o_ref, og_ref, zq_ref, zkt_ref, zv_ref, xn_ref, *, n_lat, n_plain):
    del tbl_ref
    w_refs = (w0_ref, w1_ref)
    j = pl.program_id(1)
    rb = CTX_LEN
    T = rb + n_lat

    @pl.when(j == 0)
    def _():
        xn_ref[0:rb, :] = _norm_mod(hc_ref[0], mod_ref[0, 0:1, :], mod_ref[0, 1:2, :])

        def norm_rows(r, carry):
            off = pl.multiple_of(r * rb, rb)
            x = hx_ref[0, pl.ds(off, rb), :]
            xn_ref[pl.ds(rb + off, rb), :] = _norm_mod(x, mod_ref[0, 3:4, :], mod_ref[0, 4:5, :])
            return carry

        lax.fori_loop(0, n_lat // rb, norm_rows, 0)
        og_ref[0] = jnp.dot(xn_ref[...], wl_ref[...], preferred_element_type=F32)

    @pl.when(j < n_plain)
    def _():
        w = jnp.concatenate([r[...] for r in w_refs], axis=1)
        o_ref[0] = jnp.dot(xn_ref[...], w, preferred_element_type=F32).astype(o_ref.dtype)

    for part, z_ref in enumerate((zq_ref, zkt_ref, zv_ref)):
        @pl.when(j == n_plain + part)
        def _(part=part, z_ref=z_ref):
            row = lax.broadcasted_iota(jnp.int32, (T, 1), 0)
            no_prev = jnp.logical_or(row == 0, row == CTX_LEN)
            no_next = jnp.logical_or(row == CTX_LEN - 1, row == T - 1)
            ys = [jnp.dot(xn_ref[...], r[...], preferred_element_type=F32) for r in w_refs]
            for s in range(BRANCH_W // LANES):
                lo = s * LANES
                y = ys[lo // W_TILE][:, lo % W_TILE:lo % W_TILE + LANES]
                z = _conv_silu_slab(y, cw_ref, part * BRANCH_W + lo, no_prev, no_next)
                if part == 0:
                    z_ref[0, :, lo:lo + LANES] = (z * (M_HD ** -0.5)).astype(z_ref.dtype)
                elif part == 1:
                    z_ref[0, lo:lo + LANES, :] = z.T.astype(z_ref.dtype)
                else:
                    z_ref[0, :, lo:lo + LANES] = z.astype(z_ref.dtype)


def _in_projection(hc, hx, mod8, w2_bf, wl_bf, conv_w):
    B, n_lat, _ = hx.shape
    T = CTX_LEN + n_lat
    tn = BRANCH_W
    per_step = tn // W_TILE
    assert per_step == 2
    tiles = _main_tiles()
    n_plain = N_PLAIN // tn
    n_steps = len(tiles) // per_step
    assert len(tiles) * W_TILE == N_MAIN and n_steps == n_plain + 3
    table = jnp.asarray(tiles, jnp.int32)

    def resident(shape):
        return pl.BlockSpec(shape, lambda b, j, tbl: (b, 0, 0), pipeline_mode=pl.Buffered(1))

    h_specs = [pl.BlockSpec((1, a.shape[1], D_MODEL), lambda b, j, tbl: (b, 0, 0)) for a in (hc, hx)]
    w_specs = [pl.BlockSpec((D_MODEL, W_TILE), lambda b, j, tbl, i=i: (0, tbl[per_step * j + i]))
               for i in range(per_step)]
    row_major = jax.ShapeDtypeStruct((B, T, BRANCH_W), BF16)
    grid_spec = pltpu.PrefetchScalarGridSpec(
        num_scalar_prefetch=1,
        grid=(B, n_steps),
        in_specs=h_specs + [pl.BlockSpec((1, MOD_ROWS, D_MODEL), lambda b, j, tbl: (b, 0, 0))] + w_specs
        + [pl.BlockSpec((D_MODEL, LANES), lambda b, j, tbl: (0, 0)),
           pl.BlockSpec((3, 3 * BRANCH_W), lambda b, j, tbl: (0, 0))],
        out_specs=[pl.BlockSpec((1, T, tn), lambda b, j, tbl: (b, 0, jnp.minimum(j, n_plain - 1))),
                   resident((1, T, LANES)),
                   resident((1, T, BRANCH_W)), resident((1, BRANCH_W, T)), resident((1, T, BRANCH_W))],
        scratch_shapes=[pltpu.VMEM((T, D_MODEL), BF16)])
    return pl.pallas_call(
        functools.partial(_inproj_kernel, n_lat=n_lat, n_plain=n_plain),
        grid_spec=grid_spec,
        out_shape=[jax.ShapeDtypeStruct((B, T, N_PLAIN), BF16),
                   jax.ShapeDtypeStruct((B, T, LANES), F32),
                   row_major, jax.ShapeDtypeStruct((B, BRANCH_W, T), BF16), row_major],
        compiler_params=pltpu.CompilerParams(dimension_semantics=("parallel", "arbitrary"),
                                             vmem_limit_bytes=BIG_VMEM_LIMIT),
        name="in_projection",
    )(table, hc, hx, mod8, w2_bf, w2_bf, wl_bf, conv_w)


def _rope_slab(x, cos, sin):
    lane = lax.broadcasted_iota(jnp.int32, x.shape, 1)
    first_half = (lane & (HEAD_DIM - 1)) < (HEAD_DIM // 2)
    partner = jnp.where(first_half,
                        pltpu.roll(x, LANES - HEAD_DIM // 2, axis=1),
                        pltpu.roll(x, HEAD_DIM // 2, axis=1))
    return x * cos + partner * sin


def _head_norm_slab(x, gain, gmat):
    ss = jnp.dot((x * x).astype(BF16), gmat, preferred_element_type=F32)
    return x * lax.rsqrt(ss * (1.0 / HEAD_DIM) + EPS) * gain


def _prep_kv(k_ref, v_ref, cosk_ref, sink_ref, kn_ref, gmat_ref, kp_ref, vt_ref, nk, use_norm):
    k = k_ref[0, 0:nk, :].astype(F32)
    if use_norm:
        k = _head_norm_slab(k, kn_ref[...], gmat_ref[...])
    k = _rope_slab(k, cosk_ref[0:nk, :], sink_ref[0:nk, :]).astype(BF16)
    for g in range(N_KV):
        kp_ref[g, 0:nk, :] = k[:, g * HEAD_DIM:(g + 1) * HEAD_DIM]
    for blk in range(nk // CHUNK):
        v = v_ref[0, blk * CHUNK:(blk + 1) * CHUNK, :].astype(F32)
        vt_ref[blk] = v.T.astype(BF16)


def _prep_qt(q_ref, cosq_ref, sinq_ref, qn_ref, gmat_ref, use_norm, r0, tq):
    cos = cosq_ref[r0:r0 + tq, :]
    sin = sinq_ref[r0:r0 + tq, :]
    xts = []
    for s in range(N_HEADS // 2):
        x = q_ref[0, r0:r0 + tq, s * LANES:(s + 1) * LANES].astype(F32)
        if use_norm:
            x = _head_norm_slab(x, qn_ref[...], gmat_ref[...])
        x = _rope_slab(x, cos, sin) * (HEAD_DIM ** -0.5 * LOG2E)
        xts.append(x.T.astype(BF16))
    per = GROUP // 2
    out = []
    for g in range(N_KV):
        parts = []
        for s in range(g * per, (g + 1) * per):
            parts.append(xts[s][:HEAD_DIM, :])
            parts.append(xts[s][HEAD_DIM:, :])
        out.append(jnp.concatenate(parts, axis=1))
    return out


def _heads_from_t(ot, tq):
    slabs = []
    for j in range(GROUP // 2):
        two = jnp.concatenate([ot[:, (2 * j) * tq:(2 * j + 1) * tq],
                               ot[:, (2 * j + 1) * tq:(2 * j + 2) * tq]], axis=0)
        slabs.append(two.T)
    return jnp.concatenate(slabs, axis=1)


def _window_pieces(wsink_ref, q_ref, cosq_ref, sinq_ref, o_ref, kp_ref, vt_ref, step, *,
                   tq, nsub, banded, n_lat_blocks):
    cols = GROUP * tq
    gw = GROUP * HEAD_DIM
    n_ctx_chunks = CTX_LEN // CHUNK
    lane = lax.broadcasted_iota(jnp.int32, (1, cols), 1)
    sink_rows = []
    for g in range(N_KV):
        sink_row = jnp.zeros((1, cols), F32)
        for h in range(GROUP):
            in_head = jnp.logical_and(lane >= h * tq, lane < (h + 1) * tq)
            sink_row = jnp.where(in_head, wsink_ref[g * GROUP + h] * LOG2E, sink_row)
        sink_rows.append(sink_row)
    if banded:
        diff = (lax.broadcasted_iota(jnp.int32, (CHUNK, cols), 0)
                - (lax.broadcasted_iota(jnp.int32, (CHUNK, cols), 1) & (tq - 1)))

    for sub in range(nsub):
        i = step * nsub + sub
        qts = _prep_qt(q_ref, cosq_ref, sinq_ref, None, None, False, sub * tq, tq)
        yield
        if banded:
            prev_blk = i + n_ctx_chunks - 1
            own_blk = i + n_ctx_chunks
            next_blk = jnp.minimum(i + n_ctx_chunks + 1, n_ctx_chunks + n_lat_blocks - 1)
            ok_prev = diff >= jnp.where(i > 0, 0, CHUNK)
            ok_next = diff <= jnp.where(i < n_lat_blocks - 1, 0, -CHUNK)
            plan = ([(cb, None) for cb in range(n_ctx_chunks)]
                    + [(prev_blk, ok_prev), (own_blk, None), (next_blk, ok_next)])
        else:
            plan = [(cb, None) for cb in range(n_ctx_chunks)]

        for g in range(N_KV):
            qt = qts[g]
            sink_row = sink_rows[g]
            blocks = []
            for blk, ok in plan:
                off = blk * CHUNK
                if not isinstance(off, int):
                    off = pl.multiple_of(off, CHUNK)
                st = jnp.dot(kp_ref[g, pl.ds(off, CHUNK), :], qt, preferred_element_type=F32)
                blocks.append((st if ok is None else jnp.where(ok, st, -jnp.inf), blk))
                yield
            m = sink_row
            for st, _ in blocks:
                m = jnp.maximum(m, jnp.max(st, axis=0, keepdims=True))
            l = jnp.exp2(sink_row - m)
            acc = None
            for st, blk in blocks:
                p = jnp.exp2(st - m)
                l = l + jnp.sum(p, axis=0, keepdims=True)
                pv = jnp.dot(vt_ref[blk, g * HEAD_DIM:(g + 1) * HEAD_DIM, :], p.astype(BF16),
                             preferred_element_type=F32)
                acc = pv if acc is None else acc + pv
                yield
            o_ref[0, sub * tq:(sub + 1) * tq, g * gw:(g + 1) * gw] = (
                _heads_from_t(acc / l, tq).astype(o_ref.dtype))
            yield


def _attn_kernel(wsink_ref, qa_ref, ka_ref, va_ref, qw_ref, kw_ref, vw_ref, cosq_ref, sinq_ref,
                 cosk_ref, sink_ref, qn_ref, kn_ref, gmat_ref, oa_ref, ow_ref,
                 kpa_ref, vta_ref, s_ref, kpw_ref, vtw_ref, *, tq, nsub, nk, ck, banded, n_lat_blocks):
    step = pl.program_id(1)

    @pl.when(step == 0)
    def _():
        _prep_kv(ka_ref, va_ref, cosk_ref, sink_ref, kn_ref, gmat_ref, kpa_ref, vta_ref, nk, True)
        _prep_kv(kw_ref, vw_ref, cosk_ref, sink_ref, None, None, kpw_ref, vtw_ref, nk, False)

    window = _window_pieces(wsink_ref, qw_ref, cosq_ref, sinq_ref, ow_ref, kpw_ref, vtw_ref, step,
                            tq=tq, nsub=nsub, banded=banded, n_lat_blocks=n_lat_blocks)

    qts = [_prep_qt(qa_ref, cosq_ref, sinq_ref, qn_ref, gmat_ref, True, sub * tq, tq) for sub in range(nsub)]
    items = [(sub, g) for sub in range(nsub) for g in range(N_KV)]
    n_chunks = nk // ck
    blocks_per_chunk = ck // CHUNK
    gw = GROUP * HEAD_DIM

    def score_chunk(k, c, m):
        sub, g = items[k]
        st = jnp.dot(kpa_ref[g, c * ck:(c + 1) * ck, :], qts[sub][g], preferred_element_type=F32)
        s_ref[k, c * ck:(c + 1) * ck, :] = st
        cm = jnp.max(st, axis=0, keepdims=True)
        return cm if m is None else jnp.maximum(m, cm)

    def value_chunk(k, c, m, l, acc):
        _, g = items[k]
        p = jnp.exp2(s_ref[k, c * ck:(c + 1) * ck, :] - m)
        cl = jnp.sum(p, axis=0, keepdims=True)
        vt = jnp.concatenate(
            [vta_ref[c * blocks_per_chunk + j, g * HEAD_DIM:(g + 1) * HEAD_DIM, :]
             for j in range(blocks_per_chunk)], axis=1)
        pv = jnp.dot(vt, p.astype(BF16), preferred_element_type=F32)
        return (cl, pv) if l is None else (l + cl, acc + pv)

    m_next = None
    for c in range(n_chunks):
        m_next = score_chunk(0, c, m_next)
        next(window, None)
    for k, (sub, g) in enumerate(items):
        m, m_next = m_next, None
        l = acc = None
        for c in range(n_chunks):
            l, acc = value_chunk(k, c, m, l, acc)
            if k + 1 < len(items):
                m_next = score_chunk(k + 1, c, m_next)
            next(window, None)
        oa_ref[0, sub * tq:(sub + 1) * tq, g * gw:(g + 1) * gw] = _heads_from_t(acc / l, tq).astype(oa_ref.dtype)
    for _ in window:
        pass


def _attention(p_main, rope, qn, kn, gmat, w_sink, *, q_tile0, n_q_tiles, nk, banded):
    B, T, _ = p_main.shape
    cos, sin = rope
    tq = CHUNK
    nsub = ATTN_TILES_PER_STEP
    tb = nsub * tq
    blk0 = q_tile0 // nsub
    ck = 256
    n_lat_blocks = (T - CTX_LEN) // CHUNK
    kernel = functools.partial(_attn_kernel, tq=tq, nsub=nsub, nk=nk, ck=ck, banded=banded,
                               n_lat_blocks=n_lat_blocks)

    def q_spec(col):
        return pl.BlockSpec((1, tb, BRANCH_W), lambda b, i: (b, i + blk0, col // BRANCH_W))

    def kv_spec(col):
        return pl.BlockSpec((1, T, LANES), lambda b, i: (b, 0, col // LANES))

    def const(shape):
        return pl.BlockSpec(shape, lambda b, i: (0, 0))

    out = jax.ShapeDtypeStruct((B, n_q_tiles * tq, BRANCH_W), BF16)
    kv_scratch = [pltpu.VMEM((N_KV, nk, HEAD_DIM), BF16), pltpu.VMEM((nk // CHUNK, LANES, CHUNK), BF16)]
    return pl.pallas_call(
        kernel,
        grid=(B, n_q_tiles // nsub),
        in_specs=[pl.BlockSpec(memory_space=pltpu.SMEM),
                  q_spec(COL_AQ), kv_spec(COL_AK), kv_spec(COL_AV),
                  q_spec(COL_WQ), kv_spec(COL_WK), kv_spec(COL_WV),
                  pl.BlockSpec((tb, LANES), lambda b, i: (i + blk0, 0)),
                  pl.BlockSpec((tb, LANES), lambda b, i: (i + blk0, 0)),
                  const((T, LANES)), const((T, LANES)),
                  const((1, LANES)), const((1, LANES)), const((LANES, LANES))],
        out_specs=[pl.BlockSpec((1, tb, BRANCH_W), lambda b, i: (b, i, 0)),
                   pl.BlockSpec((1, tb, BRANCH_W), lambda b, i: (b, i, 0))],
        out_shape=[out, out],
        scratch_shapes=kv_scratch + [pltpu.VMEM((nsub * N_KV, nk, GROUP * tq), F32)] + kv_scratch,
        compiler_params=_params("parallel", "arbitrary"),
        name="attention",
    )(w_sink, p_main, p_main, p_main, p_main, p_main, p_main, cos, sin, cos, sin, qn, kn, gmat)


def _spatial_gate_tile(gu, gv, lng_ref, lnb_ref, ws_ref, bs_ref):
    u = _gelu_tanh(gu.astype(F32))
    v = _gelu_tanh(gv.astype(F32))
    mu = jnp.mean(v, axis=-1, keepdims=True)
    vc = v - mu
    var = jnp.mean(vc * vc, axis=-1, keepdims=True)
    vn = (vc * lax.rsqrt(var + EPS) * lng_ref[...] + lnb_ref[...]).astype(BF16)
    gw = BRANCH_W // B_GROUPS
    chunks = []
    for c in range(gu.shape[0] // CHUNK):
        cols = []
        for g in range(B_GROUPS):
            mixed = jnp.dot(ws_ref[g], vn[c * CHUNK:(c + 1) * CHUNK, g * gw:(g + 1) * gw],
                            preferred_element_type=F32)
            cols.append(mixed + bs_ref[:, g:g + 1])
        chunks.append(jnp.concatenate(cols, axis=1))
    return u * jnp.concatenate(chunks, axis=0)


def _split3(x):
    x1 = x.astype(BF16)
    r = x - x1.astype(F32)
    x2 = r.astype(BF16)
    x3 = (r - x2.astype(F32)).astype(BF16)
    return x1, x2, x3


def _mgates_kernel(g_ref, bias_ref, ct_ref, ur_ref, ar_ref, *, T):
    L = CHUNK
    n_chunks = T // L
    n_ctx = CTX_LEN // L
    gall = g_ref[0] + bias_ref[...]
    i_all = gall[0:N_CHAINS]
    f_all = _log_sigmoid(gall[N_CHAINS:2 * N_CHAINS])
    fwd_rows = lax.broadcasted_iota(jnp.int32, (N_CHAINS, L), 0) < M_HEADS
    fwd_col = lax.broadcasted_iota(jnp.int32, (N_CHAINS, 1), 0) < M_HEADS
    lane = lax.broadcasted_iota(jnp.int32, (N_CHAINS, L), 1)
    s_idx = lax.broadcasted_iota(jnp.int32, (L, L), 0)
    t_idx = lax.broadcasted_iota(jnp.int32, (L, L), 1)
    sum_upto = jnp.where(s_idx <= t_idx, 1.0, 0.0).astype(BF16)
    sum_from = jnp.where(s_idx >= t_idx, 1.0, 0.0).astype(BF16)

    def cumsum_dir(f):
        bf = None
        bb = None
        for part in _split3(f):
            a = jnp.dot(part, sum_upto, preferred_element_type=F32)
            b = jnp.dot(part, sum_from, preferred_element_type=F32)
            bf = a if bf is None else bf + a
            bb = b if bb is None else bb + b
        return jnp.where(fwd_rows, bf, bb)

    def cummax_dir(u):
        xf = u
        xb = u
        sh = 1
        while sh < L:
            xf = jnp.maximum(xf, jnp.where(lane >= sh, pltpu.roll(xf, sh, axis=1), -jnp.inf))
            xb = jnp.maximum(xb, jnp.where(lane < L - sh, pltpu.roll(xb, L - sh, axis=1), -jnp.inf))
            sh *= 2
        return jnp.where(fwd_rows, xf, xb)

    b_c, u_c, mloc_c, bl_c, g_c, gmax_c = [], [], [], [], [], []
    for c in range(n_chunks):
        f = f_all[:, c * L:(c + 1) * L]
        i = i_all[:, c * L:(c + 1) * L]
        b = cumsum_dir(f)
        u = i - b
        bl = jnp.sum(f, axis=1, keepdims=True)
        g = bl + u
        b_c.append(b)
        u_c.append(u)
        mloc_c.append(b + cummax_dir(u))
        bl_c.append(bl)
        g_c.append(g)
        gmax_c.append(jnp.max(g, axis=1, keepdims=True))

    def scan(order):
        m = jnp.zeros((N_CHAINS, 1), F32)
        prev, new = {}, {}
        for c in order:
            prev[c] = m
            m = jnp.maximum(bl_c[c] + m, gmax_c[c])
            new[c] = m
        return prev, new

    order_f = list(range(n_chunks))
    order_b = [n_ctx - 1 - j for j in range(n_ctx)] + [n_chunks + n_ctx - 1 - j for j in range(n_ctx, n_chunks)]
    prev_f, new_f = scan(order_f)
    prev_b, new_b = scan(order_b)

    for c in range(n_chunks):
        m_prev = jnp.where(fwd_col, prev_f[c], prev_b[c])
        m_new = jnp.where(fwd_col, new_f[c], new_b[c])
        b = b_c[c]
        m_inter = b + m_prev
        m_t = jnp.maximum(m_inter, mloc_c[c])
        pack = jnp.concatenate([b - m_t, jnp.exp(m_inter - m_t), jnp.exp(-m_t),
                                jnp.zeros((LANES - 3 * N_CHAINS, L), F32)], axis=0)
        ct_ref[0, c * L:(c + 1) * L, :] = pack.T
        ur_ref[0, :, c * L:(c + 1) * L] = jnp.concatenate([u_c[c], jnp.exp(g_c[c] - m_new)], axis=0)
        ar_ref[0, c] = jnp.broadcast_to(jnp.exp(bl_c[c] + m_prev - m_new), (N_CHAINS, LANES))


def _mlstm_gates(gates_row, bias_row):
    B, _, T = gates_row.shape
    n_chunks = T // CHUNK
    return pl.pallas_call(
        functools.partial(_mgates_kernel, T=T),
        grid=(B,),
        in_specs=[pl.BlockSpec((1, 2 * N_CHAINS, T), lambda b: (b, 0, 0)),
                  pl.BlockSpec((2 * N_CHAINS, T), lambda b: (0, 0))],
        out_specs=[pl.BlockSpec((1, T, LANES), lambda b: (b, 0, 0)),
                   pl.BlockSpec((1, 2 * N_CHAINS, T), lambda b: (b, 0, 0)),
                   pl.BlockSpec((1, n_chunks, N_CHAINS, LANES), lambda b: (b, 0, 0, 0))],
        out_shape=[jax.ShapeDtypeStruct((B, T, LANES), F32),
                   jax.ShapeDtypeStruct((B, 2 * N_CHAINS, T), F32),
                   jax.ShapeDtypeStruct((B, n_chunks, N_CHAINS, LANES), F32)],
        compiler_params=_params("parallel"),
        name="mlstm_gates",
    )(gates_row, bias_row)


SCAN_BATCH = 4


def _mlstm_kernel(qf_ref, ktf_ref, vf_ref, ctf_ref, urf_ref, arf_ref,
                  qb_ref, ktb_ref, vb_ref, ctb_ref, urb_ref, arb_ref, of_ref, ob_ref, st_ref):
    L = CHUNK

    @pl.when(pl.program_id(1) == 0)
    def _():
        st_ref[...] = jnp.zeros_like(st_ref)

    t_idx = lax.broadcasted_iota(jnp.int32, (L, L), 0)
    s_idx = lax.broadcasted_iota(jnp.int32, (L, L), 1)
    ones = jnp.ones((L, M_HD), BF16)
    dirs = ((qf_ref, ktf_ref, vf_ref, ctf_ref, urf_ref, arf_ref, of_ref),
            (qb_ref, ktb_ref, vb_ref, ctb_ref, urb_ref, arb_ref, ob_ref))
    for bi in range(SCAN_BATCH):
        for d, (q_ref, kt_ref, v_ref, ct_ref, ur_ref, ar_ref, o_ref) in enumerate(dirs):
            mask = (s_idx <= t_idx) if d == 0 else (s_idx >= t_idx)
            ct = ct_ref[bi]
            ur = ur_ref[bi]
            ar = ar_ref[bi, 0]
            outs = []
            for h in range(M_HEADS):
                c = d * M_HEADS + h
                q = q_ref[bi, :, h * M_HD:(h + 1) * M_HD]
                kt = kt_ref[bi, h * M_HD:(h + 1) * M_HD, :]
                v = v_ref[bi, :, h * M_HD:(h + 1) * M_HD]
                v_ext = jnp.concatenate([v, ones], axis=1)
                cq = ct[:, c:c + 1]
                w_inter = ct[:, N_CHAINS + c:N_CHAINS + c + 1]
                e_negm = ct[:, 2 * N_CHAINS + c:2 * N_CHAINS + c + 1]
                wk_row = ur[N_CHAINS + c:N_CHAINS + c + 1, :]
                w_intra = jnp.exp(jnp.where(mask, cq + ur[c:c + 1, :], -jnp.inf))
                s = jnp.dot(q, kt, preferred_element_type=F32) * w_intra
                st_prev = st_ref[bi, d, h]
                lhs = jnp.concatenate([s.astype(BF16), (q.astype(F32) * w_inter).astype(BF16)], axis=1)
                rhs = jnp.concatenate([v_ext, st_prev.astype(BF16)], axis=0)
                tot = jnp.dot(lhs, rhs, preferred_element_type=F32)
                den = jnp.maximum(jnp.abs(tot[:, M_HD:]), e_negm)
                outs.append(tot[:, :M_HD] / den)
                kt_w = (kt.astype(F32) * wk_row).astype(BF16)
                a = jnp.concatenate([ar[c:c + 1, :], ar[c:c + 1, :]], axis=1)
                st_ref[bi, d, h] = a * st_prev + jnp.dot(kt_w, v_ext, preferred_element_type=F32)
            o_ref[0, bi] = jnp.concatenate(outs, axis=1).astype(o_ref.dtype)


def _mlstm_scan(zq, zkt, zv, ct, ur, ar):
    B, T, _ = zq.shape
    n_chunks = T // CHUNK
    n_ctx = CTX_LEN // CHUNK

    def fwd(j):
        return j

    def bwd(j):
        return jnp.where(j < n_ctx, n_ctx - 1 - j, n_chunks + n_ctx - 1 - j)

    nb = SCAN_BATCH

    def specs(order):
        return [pl.BlockSpec((nb, CHUNK, BRANCH_W), lambda b, j: (b, order(j), 0)),
                pl.BlockSpec((nb, BRANCH_W, CHUNK), lambda b, j: (b, 0, order(j))),
                pl.BlockSpec((nb, CHUNK, BRANCH_W), lambda b, j: (b, order(j), 0)),
                pl.BlockSpec((nb, CHUNK, LANES), lambda b, j: (b, order(j), 0)),
                pl.BlockSpec((nb, 2 * N_CHAINS, CHUNK), lambda b, j: (b, 0, order(j))),
                pl.BlockSpec((nb, 1, N_CHAINS, LANES), lambda b, j: (b, order(j), 0, 0))]

    out = pl.pallas_call(
        _mlstm_kernel,
        grid=(B // nb, n_chunks),
        in_specs=specs(fwd) + specs(bwd),
        out_specs=[pl.BlockSpec((1, nb, CHUNK, BRANCH_W), lambda b, j: (0, b, fwd(j), 0)),
                   pl.BlockSpec((1, nb, CHUNK, BRANCH_W), lambda b, j: (0, b, bwd(j), 0))],
        out_shape=[jax.ShapeDtypeStruct((1, B, T, BRANCH_W), F32),
                   jax.ShapeDtypeStruct((1, B, T, BRANCH_W), F32)],
        scratch_shapes=[pltpu.VMEM((nb, 2, M_HEADS, M_HD, 2 * M_HD), F32)],
        compiler_params=_params("parallel", "arbitrary"),
        name="mlstm_scan",
    )(zq, zkt, zv, ct, ur, ar, zq, zkt, zv, ct, ur, ar)
    return out[0][0], out[1][0]


def _merge_kernel(*refs, nt, mod_row):
    groups = [refs[i * nt:(i + 1) * nt] for i in range(7)]
    ya_refs, yw_refs, gu_refs, gv_refs, hf_refs, hb_refs, h_refs = groups
    (mod_ref, wg_ref, wbr_ref, wo_ref, gpost_ref, mnorm_ref,
     lng_ref, lnb_ref, ws_ref, bs_ref, o_ref) = refs[7 * nt:]
    tm = CTX_LEN
    mult = mod_ref[0, mod_row:mod_row + 1, :]
    shift = mod_ref[0, mod_row + 1:mod_row + 2, :]
    gate = mod_ref[0, mod_row + 2:mod_row + 3, :]

    def rows(tile_refs):
        return jnp.concatenate([r[0] for r in tile_refs], axis=0)

    h = rows(h_refs)
    ya, yw = rows(ya_refs), rows(yw_refs)
    yg = jnp.concatenate([_spatial_gate_tile(gu_refs[i][0], gv_refs[i][0], lng_ref, lnb_ref, ws_ref, bs_ref)
                          for i in range(nt)], axis=0)
    xn = _norm_mod(h, mult, shift)

    def proj(col, width):
        return jnp.dot(xn, wg_ref[:, col:col + width], preferred_element_type=F32)

    hm = rows(hf_refs) + rows(hb_refs)
    parts = []
    for hd in range(M_HEADS):
        x = hm[:, hd * M_HD:(hd + 1) * M_HD]
        ms = jnp.mean(x * x, axis=-1, keepdims=True)
        parts.append(x * lax.rsqrt(ms + EPS) * mnorm_ref[:, hd * M_HD:(hd + 1) * M_HD])
    ym = _sigmoid(proj(GCOL_MO, BRANCH_W)) * jnp.concatenate(parts, axis=1)
    ys = (ya.astype(F32), yw.astype(F32), yg, ym)
    acc = None
    for k in range(N_BRANCH):
        yk = (ys[k] * _silu(proj(GCOL_BRANCH_GATE[k], BRANCH_W))).astype(BF16)
        bp = jnp.dot(yk, wbr_ref[k], preferred_element_type=F32)
        g = _sigmoid(proj(GCOL_MERGE + k * D_MODEL, D_MODEL))
        acc = g * bp if acc is None else acc + g * bp
    y = jnp.dot(acc.astype(BF16), wo_ref[...], preferred_element_type=F32)
    ms = jnp.mean(y * y, axis=-1, keepdims=True)
    yn = y * lax.rsqrt(ms + EPS) * gpost_ref[...]
    o_ref[0] = h + gate * yn


def _merge(ya, yw, p_main, hf, hb, h, mod8, wg_bf, wbr_bf, wo_bf, g_post, m_norm, ln_g, ln_b, ws_bf, bs_t,
           *, context):
    tm = CTX_LEN
    B, rows, _ = h.shape
    nt = 1 if context else 2
    n_steps = rows // (nt * tm)
    stream_off = 0 if context else CTX_LEN // tm
    bw = BRANCH_W

    def tiles(w, off, col=0):
        return [pl.BlockSpec((1, tm, w), lambda b, t, i=i: (b, nt * t + i + off, col)) for i in range(nt)]

    def const(shape):
        nd = len(shape)
        return pl.BlockSpec(shape, lambda b, t: (0,) * nd)

    token_specs = (tiles(bw, 0) + tiles(bw, 0)
                   + tiles(bw, stream_off, COL_GU // bw) + tiles(bw, stream_off, COL_GV // bw)
                   + tiles(bw, stream_off) + tiles(bw, stream_off) + tiles(D_MODEL, 0))
    token_args = [ya] * nt + [yw] * nt + [p_main] * (2 * nt) + [hf] * nt + [hb] * nt + [h] * nt
    return pl.pallas_call(
        functools.partial(_merge_kernel, nt=nt, mod_row=0 if context else 3),
        grid=(B, n_steps),
        in_specs=token_specs + [
            pl.BlockSpec((1, MOD_ROWS, D_MODEL), lambda b, t: (b, 0, 0)),
            const((D_MODEL, N_W2)), const((N_BRANCH, bw, D_MODEL)), const((D_MODEL, D_MODEL)),
            const((1, D_MODEL)), const((1, bw)),
            const((1, bw)), const((1, bw)), const((B_GROUPS, CHUNK, CHUNK)), const((CHUNK, B_GROUPS))],
        out_specs=pl.BlockSpec((1, nt * tm, D_MODEL), lambda b, t: (b, t, 0)),
        out_shape=jax.ShapeDtypeStruct((B, rows, D_MODEL), F32),
        compiler_params=pltpu.CompilerParams(dimension_semantics=("parallel", "parallel"),
                                             vmem_limit_bytes=BIG_VMEM_LIMIT),
        name="merge_ctx" if context else "merge",
    )(*token_args, mod8, wg_bf, wbr_bf, wo_bf, g_post, m_norm, ln_g, ln_b, ws_bf, bs_t)


def _rope_tables(n_lat):
    rows = n_lat // GRID_W
    row = jnp.repeat(jnp.arange(rows), GRID_W).astype(F32)
    col = jnp.tile(jnp.arange(GRID_W), rows).astype(F32)
    n_freq = HEAD_DIM // 4
    inv = ROPE_THETA ** (-jnp.arange(n_freq, dtype=F32) / n_freq)
    ang = jnp.concatenate([row[:, None] * inv, col[:, None] * inv], axis=-1)
    cos, sin = jnp.cos(ang), jnp.sin(ang)
    cos_h = jnp.concatenate([cos, cos], axis=-1)
    sin_h = jnp.concatenate([-sin, sin], axis=-1)
    cos_t = jnp.concatenate([jnp.ones((CTX_LEN, HEAD_DIM), F32), cos_h], axis=0)
    sin_t = jnp.concatenate([jnp.zeros((CTX_LEN, HEAD_DIM), F32), sin_h], axis=0)
    return jnp.tile(cos_t, (1, LANES // HEAD_DIM)), jnp.tile(sin_t, (1, LANES // HEAD_DIM))


def _gather_cols(w, names):
    sl = _ref_slices()
    runs = []
    for n in names:
        start, size = sl[n]
        if runs and runs[-1][1] == start:
            runs[-1][1] = start + size
        else:
            runs.append([start, start + size])
    return jnp.concatenate([w[:, a:b].astype(BF16) for a, b in runs], axis=1)


def kernel(x, c, ctx, c_ctx, w_mod, b_mod, g_pre, g_post, w_in, a_q_norm, a_k_norm, w_sink, sg_ln_g, sg_ln_b,
           sg_w, sg_b, m_conv, m_b_i, m_b_f, m_norm, w_branch, w_out):
    B, n_lat, D = x.shape
    depth = w_mod.shape[0]
    assert depth == 2, "the context stream is only carried from the first layer to the last"
    T = CTX_LEN + n_lat
    n_ctx_chunks = CTX_LEN // CHUNK
    n_lat_chunks = n_lat // CHUNK
    rope = _rope_tables(n_lat)
    gmat = (jnp.arange(LANES)[:, None] // HEAD_DIM == jnp.arange(LANES)[None, :] // HEAD_DIM).astype(BF16)

    hc, hx = ctx, x
    mod_rows = 16
    cc = jnp.zeros((mod_rows, D), F32).at[:B].set(c).at[B].set(c_ctx)
    tm_tok = CTX_LEN
    n_ctx_tiles = CTX_LEN // tm_tok
    n_lat_tiles = n_lat // tm_tok

    for l in range(depth):
        need_ctx = l < depth - 1
        mod = _modulation(cc, w_mod[l].astype(BF16), b_mod[l][None, :])
        shift, scale, gate = mod[:, :D], mod[:, D:2 * D], mod[:, 2 * D:]
        mult = g_pre[l][None, :] * (1.0 + scale)
        zero = jnp.zeros((B, D), F32)
        mod8 = jnp.stack([jnp.broadcast_to(mult[B], (B, D)), jnp.broadcast_to(shift[B], (B, D)),
                          jnp.broadcast_to(gate[B], (B, D)), mult[:B], shift[:B], gate[:B], zero, zero],
                         axis=1)

        w_gate = _gather_cols(w_in[l], _W2_ORDER)
        w_lstm = jnp.pad(_gather_cols(w_in[l], ('m_i', 'm_f')), ((0, 0), (0, LANES - 2 * N_CHAINS)))
        p_main, p_gates, zq, zkt, zv = _in_projection(hc, hx, mod8, w_gate, w_lstm, m_conv[l])

        qn = jnp.tile(a_q_norm[l], LANES // HEAD_DIM)[None, :]
        kn = jnp.tile(a_k_norm[l], LANES // HEAD_DIM)[None, :]
        ya, yw = _attention(p_main, rope, qn, kn, gmat, w_sink[l],
                            q_tile0=n_ctx_chunks, n_q_tiles=n_lat_chunks, nk=T, banded=True)
        if need_ctx:
            ya_c, yw_c = _attention(p_main, rope, qn, kn, gmat, w_sink[l],
                                    q_tile0=0, n_q_tiles=n_ctx_chunks, nk=CTX_LEN, banded=False)

        gates_row = jnp.swapaxes(p_gates[:, :, :2 * N_CHAINS], 1, 2)
        bias_row = jnp.broadcast_to(
            jnp.concatenate([m_b_i[l].reshape(-1), m_b_f[l].reshape(-1)])[:, None], (2 * N_CHAINS, T))
        ct, ur, ar = _mlstm_gates(gates_row, bias_row)
        hf, hb = _mlstm_scan(zq, zkt, zv, ct, ur, ar)

        weights = (mod8, w_gate, w_branch[l].astype(BF16), w_out[l].astype(BF16),
                   g_post[l][None, :], m_norm[l][None, :],
                   sg_ln_g[l][None, :], sg_ln_b[l][None, :], sg_w[l].astype(BF16), sg_b[l].T)
        hx_new = _merge(ya, yw, p_main, hf, hb, hx, *weights, context=False)
        if need_ctx:
            hc = _merge(ya_c, yw_c, p_main, hf, hb, hc, *weights, context=True)
        hx = hx_new
    return hx
```

```python
import functools

import jax
import jax.numpy as jnp
from jax import lax
from jax.experimental import pallas as pl
from jax.experimental.pallas import tpu as pltpu

F32 = jnp.float32
BF16 = jnp.bfloat16

D_MODEL = 1024
GRID_W = 64
CTX_LEN = 256
N_BRANCH = 4
BRANCH_W = 512
HEAD_DIM = 64
N_HEADS = 8
N_KV = 2
GROUP = N_HEADS // N_KV
WINDOW = 128
CHUNK = 128
B_GROUPS = 4
M_HEADS = 4
M_HD = 128
N_CHAINS = 2 * M_HEADS
ROPE_THETA = 10000.0
EPS = 1e-6
LOG2E = 1.4426950408889634
ATTN_TILES_PER_STEP = 2

LANES = 128
VMEM_LIMIT = 48 * 1024 * 1024
BIG_VMEM_LIMIT = 58 * 1024 * 1024

COL_AQ = 0
COL_WQ = 512
COL_AK = 1024
COL_AV = 1152
COL_WK = 1280
COL_WV = 1408
COL_GU = 1536
COL_GV = 2048
COL_MQKV = 2560
N_PLAIN = COL_MQKV
N_MAIN = 4096
_REF_LAYOUT = (
    ('a_q', 512), ('a_k', 128), ('a_v', 128), ('a_gate', 512),
    ('w_q', 512), ('w_k', 128), ('w_v', 128), ('w_gate', 512),
    ('g_u', 512), ('g_v', 512), ('g_gate', 512),
    ('m_qkv', 1536), ('m_i', 8), ('m_f', 8), ('m_o', 512), ('m_gate', 512),
    ('merge', 4096),
)
_MAIN_ORDER = ('a_q', 'w_q', 'a_k', 'a_v', 'w_k', 'w_v', 'g_u', 'g_v', 'm_qkv')
_HEAD_ORDER = tuple(name for name, _ in _REF_LAYOUT[:[n for n, _ in _REF_LAYOUT].index('m_i')])
_TAIL_ORDER = tuple(name for name, _ in _REF_LAYOUT[[n for n, _ in _REF_LAYOUT].index('m_f') + 1:])


def _ref_slices():
    out, start = {}, 0
    for name, size in _REF_LAYOUT:
        out[name] = (start, size)
        start += size
    return out


def _piece_cols(order):
    out, start = {}, 0
    for name in order:
        out[name] = start
        start += _ref_slices()[name][1]
    out['end'] = start
    return out


def _gate_col(name):
    if name in _HEAD_ORDER:
        return 0, _piece_cols(_HEAD_ORDER)[name]
    return 1, _piece_cols(_TAIL_ORDER)[name]


GCOL_BRANCH_GATE = tuple(_gate_col(n) for n in ('a_gate', 'w_gate', 'g_gate', 'm_gate'))
GCOL_MO = _gate_col('m_o')
GCOL_MERGE = _gate_col('merge')
N_HEAD = _piece_cols(_HEAD_ORDER)['end']
N_TAIL = _piece_cols(_TAIL_ORDER)['end']
assert all(c % LANES == 0 for _, c in GCOL_BRANCH_GATE + (GCOL_MO, GCOL_MERGE))


def _params(*sem):
    return pltpu.CompilerParams(dimension_semantics=sem, vmem_limit_bytes=VMEM_LIMIT)


def _sigmoid(x):
    return 1.0 / (1.0 + jnp.exp2(x * (-LOG2E)))


def _silu(x):
    return x * _sigmoid(x)


def _gelu_tanh(x):
    c = 0.7978845608028654
    return 0.5 * x * (1.0 + jnp.tanh(c * (x + 0.044715 * (x * x * x))))


def _log_sigmoid(x):
    return -(jnp.maximum(-x, 0.0) + jnp.log1p(jnp.exp(-jnp.abs(x))))


def _mod_kernel(c_ref, w_ref, b_ref, o_ref):
    a = _silu(c_ref[...]).astype(BF16)
    o_ref[...] = jnp.dot(a, w_ref[0].astype(BF16), preferred_element_type=F32) + b_ref[...]


def _modulation(cc, w_mod, b_mod, layer):
    rows = cc.shape[0]
    n = w_mod.shape[2]
    tn = 1024
    return pl.pallas_call(
        _mod_kernel,
        grid=(n // tn,),
        in_specs=[pl.BlockSpec((rows, D_MODEL), lambda j: (0, 0)),
                  pl.BlockSpec((1, D_MODEL, tn), lambda j: (layer, 0, j)),
                  pl.BlockSpec((1, tn), lambda j: (0, j))],
        out_specs=pl.BlockSpec((rows, tn), lambda j: (0, j)),
        out_shape=jax.ShapeDtypeStruct((rows, n), F32),
        compiler_params=_params("parallel"),
        name="modulation",
    )(cc, w_mod, b_mod)


MOD_ROWS = 8


def _norm_mod(x, mult, shift):
    ms = jnp.mean(x * x, axis=-1, keepdims=True)
    return (x * lax.rsqrt(ms + EPS) * mult + shift).astype(BF16)


def _conv_silu_slab(x, w_ref, col, no_prev, no_next):
    T = x.shape[0]
    prev = jnp.where(no_prev, 0.0, pltpu.roll(x, 1, axis=0))
    nxt = jnp.where(no_next, 0.0, pltpu.roll(x, T - 1, axis=0))
    return _silu(prev * w_ref[0:1, col:col + LANES] + x * w_ref[1:2, col:col + LANES]
                 + nxt * w_ref[2:3, col:col + LANES])


W_TILE = 2 * LANES


def _main_tiles():
    sl = _ref_slices()
    tiles = []
    for name in _MAIN_ORDER:
        start, size = sl[name]
        if size < W_TILE:
            if start % W_TILE == 0:
                tiles.append(start // W_TILE)
            continue
        assert start % W_TILE == 0 and size % W_TILE == 0
        tiles.extend(range(start // W_TILE, (start + size) // W_TILE))
    return tiles


def _inproj_kernel(tbl_ref, hc_ref, hx_ref, mod_ref, w0_ref, w1_ref, wl_ref, cw_ref,
                   o_ref, og_ref, zq_ref, zkt_ref, zv_ref, xn_ref, *, n_lat, n_plain):
    del tbl_ref
    w_refs = (w0_ref, w1_ref)
    j = pl.program_id(1)
    rb = CTX_LEN
    T = rb + n_lat

    @pl.when(j == 0)
    def _():
        xn_ref[0:rb, :] = _norm_mod(hc_ref[0], mod_ref[0, 0:1, :], mod_ref[0, 1:2, :])

        def norm_rows(r, carry):
            off = pl.multiple_of(r * rb, rb)
            x = hx_ref[0, pl.ds(off, rb), :]
            xn_ref[pl.ds(rb + off, rb), :] = _norm_mod(x, mod_ref[0, 3:4, :], mod_ref[0, 4:5, :])
            return carry

        lax.fori_loop(0, n_lat // rb, norm_rows, 0)
        og_ref[0] = jnp.dot(xn_ref[...], wl_ref[...], preferred_element_type=F32)

    @pl.when(j < n_plain)
    def _():
        w = jnp.concatenate([r[...] for r in w_refs], axis=1)
        o_ref[0] = jnp.dot(xn_ref[...], w, preferred_element_type=F32).astype(o_ref.dtype)

    for part, z_ref in enumerate((zq_ref, zkt_ref, zv_ref)):
        @pl.when(j == n_plain + part)
        def _(part=part, z_ref=z_ref):
            row = lax.broadcasted_iota(jnp.int32, (T, 1), 0)
            no_prev = jnp.logical_or(row == 0, row == CTX_LEN)
            no_next = jnp.logical_or(row == CTX_LEN - 1, row == T - 1)
            ys = [jnp.dot(xn_ref[...], r[...], preferred_element_type=F32) for r in w_refs]
            for s in range(BRANCH_W // LANES):
                lo = s * LANES
                y = ys[lo // W_TILE][:, lo % W_TILE:lo % W_TILE + LANES]
                z = _conv_silu_slab(y, cw_ref, part * BRANCH_W + lo, no_prev, no_next)
                if part == 0:
                    z_ref[0, :, lo:lo + LANES] = (z * (M_HD ** -0.5)).astype(z_ref.dtype)
                elif part == 1:
                    z_ref[0, lo:lo + LANES, :] = z.T.astype(z_ref.dtype)
                else:
                    z_ref[0, :, lo:lo + LANES] = z.astype(z_ref.dtype)


def _in_projection(hc, hx, mod8, w2_bf, wl_bf, conv_w):
    B, n_lat, _ = hx.shape
    T = CTX_LEN + n_lat
    tn = BRANCH_W
    per_step = tn // W_TILE
    assert per_step == 2
    tiles = _main_tiles()
    n_plain = N_PLAIN // tn
    n_steps = len(tiles) // per_step
    assert len(tiles) * W_TILE == N_MAIN and n_steps == n_plain + 3
    table = jnp.asarray(tiles, jnp.int32)

    def resident(shape):
        return pl.BlockSpec(shape, lambda b, j, tbl: (b, 0, 0), pipeline_mode=pl.Buffered(1))

    h_specs = [pl.BlockSpec((1, a.shape[1], D_MODEL), lambda b, j, tbl: (b, 0, 0)) for a in (hc, hx)]
    w_specs = [pl.BlockSpec((D_MODEL, W_TILE), lambda b, j, tbl, i=i: (0, tbl[per_step * j + i]))
               for i in range(per_step)]
    row_major = jax.ShapeDtypeStruct((B, T, BRANCH_W), BF16)
    grid_spec = pltpu.PrefetchScalarGridSpec(
        num_scalar_prefetch=1,
        grid=(B, n_steps),
        in_specs=h_specs + [pl.BlockSpec((1, MOD_ROWS, D_MODEL), lambda b, j, tbl: (b, 0, 0))] + w_specs
        + [pl.BlockSpec((D_MODEL, LANES), lambda b, j, tbl: (0, 0)),
           pl.BlockSpec((3, 3 * BRANCH_W), lambda b, j, tbl: (0, 0))],
        out_specs=[pl.BlockSpec((1, T, tn), lambda b, j, tbl: (b, 0, jnp.minimum(j, n_plain - 1))),
                   resident((1, T, LANES)),
                   resident((1, T, BRANCH_W)), resident((1, BRANCH_W, T)), resident((1, T, BRANCH_W))],
        scratch_shapes=[pltpu.VMEM((T, D_MODEL), BF16)])
    return pl.pallas_call(
        functools.partial(_inproj_kernel, n_lat=n_lat, n_plain=n_plain),
        grid_spec=grid_spec,
        out_shape=[jax.ShapeDtypeStruct((B, T, N_PLAIN), BF16),
                   jax.ShapeDtypeStruct((B, T, LANES), F32),
                   row_major, jax.ShapeDtypeStruct((B, BRANCH_W, T), BF16), row_major],
        compiler_params=pltpu.CompilerParams(dimension_semantics=("parallel", "arbitrary"),
                                             vmem_limit_bytes=BIG_VMEM_LIMIT),
        name="in_projection",
    )(table, hc, hx, mod8, w2_bf, w2_bf, wl_bf, conv_w)


def _rope_slab(x, cos, sin):
    lane = lax.broadcasted_iota(jnp.int32, x.shape, 1)
    first_half = (lane & (HEAD_DIM - 1)) < (HEAD_DIM // 2)
    partner = jnp.where(first_half,
                        pltpu.roll(x, LANES - HEAD_DIM // 2, axis=1),
                        pltpu.roll(x, HEAD_DIM // 2, axis=1))
    return x * cos + partner * sin


def _head_norm_slab(x, gain, gmat):
    ss = jnp.dot((x * x).astype(BF16), gmat, preferred_element_type=F32)
    return x * lax.rsqrt(ss * (1.0 / HEAD_DIM) + EPS) * gain


def _prep_kv(k_ref, v_ref, cosk_ref, sink_ref, kn_ref, gmat_ref, kp_ref, vt_ref, nk, use_norm):
    k = k_ref[0, 0:nk, :].astype(F32)
    if use_norm:
        k = _head_norm_slab(k, kn_ref[...], gmat_ref[...])
    k = _rope_slab(k, cosk_ref[0:nk, :], sink_ref[0:nk, :]).astype(BF16)
    for g in range(N_KV):
        kp_ref[g, 0:nk, :] = k[:, g * HEAD_DIM:(g + 1) * HEAD_DIM]
    for blk in range(nk // CHUNK):
        v = v_ref[0, blk * CHUNK:(blk + 1) * CHUNK, :].astype(F32)
        vt_ref[blk] = v.T.astype(BF16)


def _prep_qt(q_ref, cosq_ref, sinq_ref, qn_ref, gmat_ref, use_norm, r0, tq):
    cos = cosq_ref[r0:r0 + tq, :]
    sin = sinq_ref[r0:r0 + tq, :]
    xts = []
    for s in range(N_HEADS // 2):
        x = q_ref[0, r0:r0 + tq, s * LANES:(s + 1) * LANES].astype(F32)
        if use_norm:
            x = _head_norm_slab(x, qn_ref[...], gmat_ref[...])
        x = _rope_slab(x, cos, sin) * (HEAD_DIM ** -0.5 * LOG2E)
        xts.append(x.T.astype(BF16))
    per = GROUP // 2
    out = []
    for g in range(N_KV):
        parts = []
        for s in range(g * per, (g + 1) * per):
            parts.append(xts[s][:HEAD_DIM, :])
            parts.append(xts[s][HEAD_DIM:, :])
        out.append(jnp.concatenate(parts, axis=1))
    return out


def _heads_from_t(ot, tq):
    slabs = []
    for j in range(GROUP // 2):
        two = jnp.concatenate([ot[:, (2 * j) * tq:(2 * j + 1) * tq],
                               ot[:, (2 * j + 1) * tq:(2 * j + 2) * tq]], axis=0)
        slabs.append(two.T)
    return jnp.concatenate(slabs, axis=1)


def _window_pieces(wsink_ref, q_ref, cosq_ref, sinq_ref, o_ref, kp_ref, vt_ref, step, *,
                   tq, nsub, banded, n_lat_blocks):
    cols = GROUP * tq
    gw = GROUP * HEAD_DIM
    n_ctx_chunks = CTX_LEN // CHUNK
    lane = lax.broadcasted_iota(jnp.int32, (1, cols), 1)
    sink_rows = []
    for g in range(N_KV):
        sink_row = jnp.zeros((1, cols), F32)
        for h in range(GROUP):
            in_head = jnp.logical_and(lane >= h * tq, lane < (h + 1) * tq)
            sink_row = jnp.where(in_head, wsink_ref[g * GROUP + h] * LOG2E, sink_row)
        sink_rows.append(sink_row)
    if banded:
        diff = (lax.broadcasted_iota(jnp.int32, (CHUNK, cols), 0)
                - (lax.broadcasted_iota(jnp.int32, (CHUNK, cols), 1) & (tq - 1)))

    for sub in range(nsub):
        i = step * nsub + sub
        qts = _prep_qt(q_ref, cosq_ref, sinq_ref, None, None, False, sub * tq, tq)
        yield
        if banded:
            prev_blk = i + n_ctx_chunks - 1
            own_blk = i + n_ctx_chunks
            next_blk = jnp.minimum(i + n_ctx_chunks + 1, n_ctx_chunks + n_lat_blocks - 1)
            ok_prev = diff >= jnp.where(i > 0, 0, CHUNK)
            ok_next = diff <= jnp.where(i < n_lat_blocks - 1, 0, -CHUNK)
            plan = ([(cb, None) for cb in range(n_ctx_chunks)]
                    + [(prev_blk, ok_prev), (own_blk, None), (next_blk, ok_next)])
        else:
            plan = [(cb, None) for cb in range(n_ctx_chunks)]

        for g in range(N_KV):
            qt = qts[g]
            sink_row = sink_rows[g]
            blocks = []
            for blk, ok in plan:
                off = blk * CHUNK
                if not isinstance(off, int):
                    off = pl.multiple_of(off, CHUNK)
                st = jnp.dot(kp_ref[g, pl.ds(off, CHUNK), :], qt, preferred_element_type=F32)
                blocks.append((st if ok is None else jnp.where(ok, st, -jnp.inf), blk))
                yield
            m = sink_row
            for st, _ in blocks:
                m = jnp.maximum(m, jnp.max(st, axis=0, keepdims=True))
            l = jnp.exp2(sink_row - m)
            acc = None
            for st, blk in blocks:
                p = jnp.exp2(st - m)
                l = l + jnp.sum(p, axis=0, keepdims=True)
                pv = jnp.dot(vt_ref[blk, g * HEAD_DIM:(g + 1) * HEAD_DIM, :], p.astype(BF16),
                             preferred_element_type=F32)
                acc = pv if acc is None else acc + pv
                yield
            o_ref[0, sub * tq:(sub + 1) * tq, g * gw:(g + 1) * gw] = (
                _heads_from_t(acc / l, tq).astype(o_ref.dtype))
            yield


def _attn_kernel(wsink_ref, qa_ref, ka_ref, va_ref, qw_ref, kw_ref, vw_ref, cosq_ref, sinq_ref,
                 cosk_ref, sink_ref, qn_ref, kn_ref, gmat_ref, oa_ref, ow_ref,
                 kpa_ref, vta_ref, s_ref, kpw_ref, vtw_ref, *, tq, nsub, nk, ck, banded, n_lat_blocks):
    step = pl.program_id(1)

    @pl.when(step == 0)
    def _():
        _prep_kv(ka_ref, va_ref, cosk_ref, sink_ref, kn_ref, gmat_ref, kpa_ref, vta_ref, nk, True)
        _prep_kv(kw_ref, vw_ref, cosk_ref, sink_ref, None, None, kpw_ref, vtw_ref, nk, False)

    window = _window_pieces(wsink_ref, qw_ref, cosq_ref, sinq_ref, ow_ref, kpw_ref, vtw_ref, step,
                            tq=tq, nsub=nsub, banded=banded, n_lat_blocks=n_lat_blocks)

    qts = [_prep_qt(qa_ref, cosq_ref, sinq_ref, qn_ref, gmat_ref, True, sub * tq, tq) for sub in range(nsub)]
    items = [(sub, g) for sub in range(nsub) for g in range(N_KV)]
    n_chunks = nk // ck
    blocks_per_chunk = ck // CHUNK
    gw = GROUP * HEAD_DIM

    def score_chunk(k, c, m):
        sub, g = items[k]
        st = jnp.dot(kpa_ref[g, c * ck:(c + 1) * ck, :], qts[sub][g], preferred_element_type=F32)
        s_ref[k, c * ck:(c + 1) * ck, :] = st
        cm = jnp.max(st, axis=0, keepdims=True)
        return cm if m is None else jnp.maximum(m, cm)

    def value_chunk(k, c, m, l, acc):
        _, g = items[k]
        p = jnp.exp2(s_ref[k, c * ck:(c + 1) * ck, :] - m)
        cl = jnp.sum(p, axis=0, keepdims=True)
        vt = jnp.concatenate(
            [vta_ref[c * blocks_per_chunk + j, g * HEAD_DIM:(g + 1) * HEAD_DIM, :]
             for j in range(blocks_per_chunk)], axis=1)
        pv = jnp.dot(vt, p.astype(BF16), preferred_element_type=F32)
        return (cl, pv) if l is None else (l + cl, acc + pv)

    m_next = None
    for c in range(n_chunks):
        m_next = score_chunk(0, c, m_next)
        next(window, None)
    for k, (sub, g) in enumerate(items):
        m, m_next = m_next, None
        l = acc = None
        for c in range(n_chunks):
            l, acc = value_chunk(k, c, m, l, acc)
            if k + 1 < len(items):
                m_next = score_chunk(k + 1, c, m_next)
            next(window, None)
        oa_ref[0, sub * tq:(sub + 1) * tq, g * gw:(g + 1) * gw] = _heads_from_t(acc / l, tq).astype(oa_ref.dtype)
    for _ in window:
        pass


def _attention(p_main, rope, qn, kn, gmat, w_sink, *, q_tile0, n_q_tiles, nk, banded):
    B, T, _ = p_main.shape
    cos, sin = rope
    tq = CHUNK
    nsub = ATTN_TILES_PER_STEP
    tb = nsub * tq
    blk0 = q_tile0 // nsub
    ck = 256
    n_lat_blocks = (T - CTX_LEN) // CHUNK
    kernel = functools.partial(_attn_kernel, tq=tq, nsub=nsub, nk=nk, ck=ck, banded=banded,
                               n_lat_blocks=n_lat_blocks)

    def q_spec(col):
        return pl.BlockSpec((1, tb, BRANCH_W), lambda b, i: (b, i + blk0, col // BRANCH_W))

    def kv_spec(col):
        return pl.BlockSpec((1, T, LANES), lambda b, i: (b, 0, col // LANES))

    def const(shape):
        return pl.BlockSpec(shape, lambda b, i: (0, 0))

    out = jax.ShapeDtypeStruct((B, n_q_tiles * tq, BRANCH_W), BF16)
    kv_scratch = [pltpu.VMEM((N_KV, nk, HEAD_DIM), BF16), pltpu.VMEM((nk // CHUNK, LANES, CHUNK), BF16)]
    return pl.pallas_call(
        kernel,
        grid=(B, n_q_tiles // nsub),
        in_specs=[pl.BlockSpec(memory_space=pltpu.SMEM),
                  q_spec(COL_AQ), kv_spec(COL_AK), kv_spec(COL_AV),
                  q_spec(COL_WQ), kv_spec(COL_WK), kv_spec(COL_WV),
                  pl.BlockSpec((tb, LANES), lambda b, i: (i + blk0, 0)),
                  pl.BlockSpec((tb, LANES), lambda b, i: (i + blk0, 0)),
                  const((T, LANES)), const((T, LANES)),
                  const((1, LANES)), const((1, LANES)), const((LANES, LANES))],
        out_specs=[pl.BlockSpec((1, tb, BRANCH_W), lambda b, i: (b, i, 0)),
                   pl.BlockSpec((1, tb, BRANCH_W), lambda b, i: (b, i, 0))],
        out_shape=[out, out],
        scratch_shapes=kv_scratch + [pltpu.VMEM((nsub * N_KV, nk, GROUP * tq), F32)] + kv_scratch,
        compiler_params=_params("parallel", "arbitrary"),
        name="attention",
    )(w_sink, p_main, p_main, p_main, p_main, p_main, p_main, cos, sin, cos, sin, qn, kn, gmat)


def _spatial_gate_tile(gu, gv, lng_ref, lnb_ref, ws_ref, bs_ref):
    u = _gelu_tanh(gu.astype(F32))
    v = _gelu_tanh(gv.astype(F32))
    mu = jnp.mean(v, axis=-1, keepdims=True)
    vc = v - mu
    var = jnp.mean(vc * vc, axis=-1, keepdims=True)
    vn = (vc * lax.rsqrt(var + EPS) * lng_ref[...] + lnb_ref[...]).astype(BF16)
    gw = BRANCH_W // B_GROUPS
    chunks = []
    for c in range(gu.shape[0] // CHUNK):
        cols = []
        for g in range(B_GROUPS):
            mixed = jnp.dot(ws_ref[g], vn[c * CHUNK:(c + 1) * CHUNK, g * gw:(g + 1) * gw],
                            preferred_element_type=F32)
            cols.append(mixed + bs_ref[:, g:g + 1])
        chunks.append(jnp.concatenate(cols, axis=1))
    return u * jnp.concatenate(chunks, axis=0)


def _split3(x):
    x1 = x.astype(BF16)
    r = x - x1.astype(F32)
    x2 = r.astype(BF16)
    x3 = (r - x2.astype(F32)).astype(BF16)
    return x1, x2, x3


def _mgates_kernel(g_ref, bias_ref, ct_ref, ur_ref, ar_ref, *, T):
    L = CHUNK
    n_chunks = T // L
    n_ctx = CTX_LEN // L
    gall = g_ref[0] + bias_ref[...]
    i_all = gall[0:N_CHAINS]
    f_all = _log_sigmoid(gall[N_CHAINS:2 * N_CHAINS])
    fwd_rows = lax.broadcasted_iota(jnp.int32, (N_CHAINS, L), 0) < M_HEADS
    fwd_col = lax.broadcasted_iota(jnp.int32, (N_CHAINS, 1), 0) < M_HEADS
    lane = lax.broadcasted_iota(jnp.int32, (N_CHAINS, L), 1)
    s_idx = lax.broadcasted_iota(jnp.int32, (L, L), 0)
    t_idx = lax.broadcasted_iota(jnp.int32, (L, L), 1)
    sum_upto = jnp.where(s_idx <= t_idx, 1.0, 0.0).astype(BF16)
    sum_from = jnp.where(s_idx >= t_idx, 1.0, 0.0).astype(BF16)

    def cumsum_dir(f):
        bf = None
        bb = None
        for part in _split3(f):
            a = jnp.dot(part, sum_upto, preferred_element_type=F32)
            b = jnp.dot(part, sum_from, preferred_element_type=F32)
            bf = a if bf is None else bf + a
            bb = b if bb is None else bb + b
        return jnp.where(fwd_rows, bf, bb)

    def cummax_dir(u):
        xf = u
        xb = u
        sh = 1
        while sh < L:
            xf = jnp.maximum(xf, jnp.where(lane >= sh, pltpu.roll(xf, sh, axis=1), -jnp.inf))
            xb = jnp.maximum(xb, jnp.where(lane < L - sh, pltpu.roll(xb, L - sh, axis=1), -jnp.inf))
            sh *= 2
        return jnp.where(fwd_rows, xf, xb)

    b_c, u_c, mloc_c, bl_c, g_c, gmax_c = [], [], [], [], [], []
    for c in range(n_chunks):
        f = f_all[:, c * L:(c + 1) * L]
        i = i_all[:, c * L:(c + 1) * L]
        b = cumsum_dir(f)
        u = i - b
        bl = jnp.sum(f, axis=1, keepdims=True)
        g = bl + u
        b_c.append(b)
        u_c.append(u)
        mloc_c.append(b + cummax_dir(u))
        bl_c.append(bl)
        g_c.append(g)
        gmax_c.append(jnp.max(g, axis=1, keepdims=True))

    def scan(order):
        m = jnp.zeros((N_CHAINS, 1), F32)
        prev, new = {}, {}
        for c in order:
            prev[c] = m
            m = jnp.maximum(bl_c[c] + m, gmax_c[c])
            new[c] = m
        return prev, new

    order_f = list(range(n_chunks))
    order_b = [n_ctx - 1 - j for j in range(n_ctx)] + [n_chunks + n_ctx - 1 - j for j in range(n_ctx, n_chunks)]
    prev_f, new_f = scan(order_f)
    prev_b, new_b = scan(order_b)

    for c in range(n_chunks):
        m_prev = jnp.where(fwd_col, prev_f[c], prev_b[c])
        m_new = jnp.where(fwd_col, new_f[c], new_b[c])
        b = b_c[c]
        m_inter = b + m_prev
        m_t = jnp.maximum(m_inter, mloc_c[c])
        pack = jnp.concatenate([b - m_t, jnp.exp(m_inter - m_t), jnp.exp(-m_t),
                                jnp.zeros((LANES - 3 * N_CHAINS, L), F32)], axis=0)
        ct_ref[0, c * L:(c + 1) * L, :] = pack.T
        ur_ref[0, :, c * L:(c + 1) * L] = jnp.concatenate([u_c[c], jnp.exp(g_c[c] - m_new)], axis=0)
        ar_ref[0, c] = jnp.broadcast_to(jnp.exp(bl_c[c] + m_prev - m_new), (N_CHAINS, LANES))


def _mlstm_gates(gates_row, bias_row):
    B, _, T = gates_row.shape
    n_chunks = T // CHUNK
    return pl.pallas_call(
        functools.partial(_mgates_kernel, T=T),
        grid=(B,),
        in_specs=[pl.BlockSpec((1, 2 * N_CHAINS, T), lambda b: (b, 0, 0)),
                  pl.BlockSpec((2 * N_CHAINS, T), lambda b: (0, 0))],
        out_specs=[pl.BlockSpec((1, T, LANES), lambda b: (b, 0, 0)),
                   pl.BlockSpec((1, 2 * N_CHAINS, T), lambda b: (b, 0, 0)),
                   pl.BlockSpec((1, n_chunks, N_CHAINS, LANES), lambda b: (b, 0, 0, 0))],
        out_shape=[jax.ShapeDtypeStruct((B, T, LANES), F32),
                   jax.ShapeDtypeStruct((B, 2 * N_CHAINS, T), F32),
                   jax.ShapeDtypeStruct((B, n_chunks, N_CHAINS, LANES), F32)],
        compiler_params=_params("parallel"),
        name="mlstm_gates",
    )(gates_row, bias_row)


SCAN_BATCH = 4


def _mlstm_kernel(qf_ref, ktf_ref, vf_ref, ctf_ref, urf_ref, arf_ref,
                  qb_ref, ktb_ref, vb_ref, ctb_ref, urb_ref, arb_ref, of_ref, ob_ref, st_ref):
    L = CHUNK

    @pl.when(pl.program_id(1) == 0)
    def _():
        st_ref[...] = jnp.zeros_like(st_ref)

    t_idx = lax.broadcasted_iota(jnp.int32, (L, L), 0)
    s_idx = lax.broadcasted_iota(jnp.int32, (L, L), 1)
    ones = jnp.ones((L, M_HD), BF16)
    dirs = ((qf_ref, ktf_ref, vf_ref, ctf_ref, urf_ref, arf_ref, of_ref),
            (qb_ref, ktb_ref, vb_ref, ctb_ref, urb_ref, arb_ref, ob_ref))
    for bi in range(SCAN_BATCH):
        for d, (q_ref, kt_ref, v_ref, ct_ref, ur_ref, ar_ref, o_ref) in enumerate(dirs):
            mask = (s_idx <= t_idx) if d == 0 else (s_idx >= t_idx)
            ct = ct_ref[bi]
            ur = ur_ref[bi]
            ar = ar_ref[bi, 0]
            outs = []
            for h in range(M_HEADS):
                c = d * M_HEADS + h
                q = q_ref[bi, :, h * M_HD:(h + 1) * M_HD]
                kt = kt_ref[bi, h * M_HD:(h + 1) * M_HD, :]
                v = v_ref[bi, :, h * M_HD:(h + 1) * M_HD]
                v_ext = jnp.concatenate([v, ones], axis=1)
                cq = ct[:, c:c + 1]
                w_inter = ct[:, N_CHAINS + c:N_CHAINS + c + 1]
                e_negm = ct[:, 2 * N_CHAINS + c:2 * N_CHAINS + c + 1]
                wk_row = ur[N_CHAINS + c:N_CHAINS + c + 1, :]
                w_intra = jnp.exp(jnp.where(mask, cq + ur[c:c + 1, :], -jnp.inf))
                s = jnp.dot(q, kt, preferred_element_type=F32) * w_intra
                st_prev = st_ref[bi, d, h]
                lhs = jnp.concatenate([s.astype(BF16), (q.astype(F32) * w_inter).astype(BF16)], axis=1)
                rhs = jnp.concatenate([v_ext, st_prev.astype(BF16)], axis=0)
                tot = jnp.dot(lhs, rhs, preferred_element_type=F32)
                den = jnp.maximum(jnp.abs(tot[:, M_HD:]), e_negm)
                outs.append(tot[:, :M_HD] / den)
                kt_w = (kt.astype(F32) * wk_row).astype(BF16)
                a = jnp.concatenate([ar[c:c + 1, :], ar[c:c + 1, :]], axis=1)
                st_ref[bi, d, h] = a * st_prev + jnp.dot(kt_w, v_ext, preferred_element_type=F32)
            o_ref[0, bi] = jnp.concatenate(outs, axis=1).astype(o_ref.dtype)


def _mlstm_scan(zq, zkt, zv, ct, ur, ar):
    B, T, _ = zq.shape
    n_chunks = T // CHUNK
    n_ctx = CTX_LEN // CHUNK

    def fwd(j):
        return j

    def bwd(j):
        return jnp.where(j < n_ctx, n_ctx - 1 - j, n_chunks + n_ctx - 1 - j)

    nb = SCAN_BATCH

    def specs(order):
        return [pl.BlockSpec((nb, CHUNK, BRANCH_W), lambda b, j: (b, order(j), 0)),
                pl.BlockSpec((nb, BRANCH_W, CHUNK), lambda b, j: (b, 0, order(j))),
                pl.BlockSpec((nb, CHUNK, BRANCH_W), lambda b, j: (b, order(j), 0)),
                pl.BlockSpec((nb, CHUNK, LANES), lambda b, j: (b, order(j), 0)),
                pl.BlockSpec((nb, 2 * N_CHAINS, CHUNK), lambda b, j: (b, 0, order(j))),
                pl.BlockSpec((nb, 1, N_CHAINS, LANES), lambda b, j: (b, order(j), 0, 0))]

    out = pl.pallas_call(
        _mlstm_kernel,
        grid=(B // nb, n_chunks),
        in_specs=specs(fwd) + specs(bwd),
        out_specs=[pl.BlockSpec((1, nb, CHUNK, BRANCH_W), lambda b, j: (0, b, fwd(j), 0)),
                   pl.BlockSpec((1, nb, CHUNK, BRANCH_W), lambda b, j: (0, b, bwd(j), 0))],
        out_shape=[jax.ShapeDtypeStruct((1, B, T, BRANCH_W), F32),
                   jax.ShapeDtypeStruct((1, B, T, BRANCH_W), F32)],
        scratch_shapes=[pltpu.VMEM((nb, 2, M_HEADS, M_HD, 2 * M_HD), F32)],
        compiler_params=_params("parallel", "arbitrary"),
        name="mlstm_scan",
    )(zq, zkt, zv, ct, ur, ar, zq, zkt, zv, ct, ur, ar)
    return out[0][0], out[1][0]


def _merge_kernel(*refs, nt, mod_row):
    groups = [refs[i * nt:(i + 1) * nt] for i in range(7)]
    ya_refs, yw_refs, gu_refs, gv_refs, hf_refs, hb_refs, h_refs = groups
    (mod_ref, wh_ref, wt_ref, wbr_ref, wo_ref, gpost_ref, mnorm_ref,
     lng_ref, lnb_ref, ws_ref, bs_ref, o_ref) = refs[7 * nt:]
    tm = CTX_LEN
    mult = mod_ref[0, mod_row:mod_row + 1, :]
    shift = mod_ref[0, mod_row + 1:mod_row + 2, :]
    gate = mod_ref[0, mod_row + 2:mod_row + 3, :]

    def rows(tile_refs):
        return jnp.concatenate([r[0] for r in tile_refs], axis=0)

    h = rows(h_refs)
    ya, yw = rows(ya_refs), rows(yw_refs)
    yg = jnp.concatenate([_spatial_gate_tile(gu_refs[i][0], gv_refs[i][0], lng_ref, lnb_ref, ws_ref, bs_ref)
                          for i in range(nt)], axis=0)
    xn = _norm_mod(h, mult, shift)

    def proj(where, width, offset=0):
        piece, col = where
        w_ref = (wh_ref, wt_ref)[piece]
        return jnp.dot(xn, w_ref[:, col + offset:col + offset + width], preferred_element_type=F32)

    hm = rows(hf_refs) + rows(hb_refs)
    parts = []
    for hd in range(M_HEADS):
        x = hm[:, hd * M_HD:(hd + 1) * M_HD]
        ms = jnp.mean(x * x, axis=-1, keepdims=True)
        parts.append(x * lax.rsqrt(ms + EPS) * mnorm_ref[:, hd * M_HD:(hd + 1) * M_HD])
    ym = _sigmoid(proj(GCOL_MO, BRANCH_W)) * jnp.concatenate(parts, axis=1)
    ys = (ya.astype(F32), yw.astype(F32), yg, ym)
    acc = None
    for k in range(N_BRANCH):
        yk = (ys[k] * _silu(proj(GCOL_BRANCH_GATE[k], BRANCH_W))).astype(BF16)
        bp = jnp.dot(yk, wbr_ref[k], preferred_element_type=F32)
        g = _sigmoid(proj(GCOL_MERGE, D_MODEL, k * D_MODEL))
        acc = g * bp if acc is None else acc + g * bp
    y = jnp.dot(acc.astype(BF16), wo_ref[...], preferred_element_type=F32)
    ms = jnp.mean(y * y, axis=-1, keepdims=True)
    yn = y * lax.rsqrt(ms + EPS) * gpost_ref[...]
    o_ref[0] = h + gate * yn


def _merge(ya, yw, p_main, hf, hb, h, mod8, wh_bf, wt_bf, wbr_bf, wo_bf, g_post, m_norm, ln_g, ln_b, ws_bf,
           bs_t, *, context):
    tm = CTX_LEN
    B, rows, _ = h.shape
    nt = 1 if context else 2
    n_steps = rows // (nt * tm)
    stream_off = 0 if context else CTX_LEN // tm
    bw = BRANCH_W

    def tiles(w, off, col=0):
        return [pl.BlockSpec((1, tm, w), lambda b, t, i=i: (b, nt * t + i + off, col)) for i in range(nt)]

    def const(shape):
        nd = len(shape)
        return pl.BlockSpec(shape, lambda b, t: (0,) * nd)

    token_specs = (tiles(bw, 0) + tiles(bw, 0)
                   + tiles(bw, stream_off, COL_GU // bw) + tiles(bw, stream_off, COL_GV // bw)
                   + tiles(bw, stream_off) + tiles(bw, stream_off) + tiles(D_MODEL, 0))
    token_args = [ya] * nt + [yw] * nt + [p_main] * (2 * nt) + [hf] * nt + [hb] * nt + [h] * nt
    return pl.pallas_call(
        functools.partial(_merge_kernel, nt=nt, mod_row=0 if context else 3),
        grid=(B, n_steps),
        in_specs=token_specs + [
            pl.BlockSpec((1, MOD_ROWS, D_MODEL), lambda b, t: (b, 0, 0)),
            const((D_MODEL, N_HEAD)), const((D_MODEL, N_TAIL)),
            const((N_BRANCH, bw, D_MODEL)), const((D_MODEL, D_MODEL)),
            const((1, D_MODEL)), const((1, bw)),
            const((1, bw)), const((1, bw)), const((B_GROUPS, CHUNK, CHUNK)), const((CHUNK, B_GROUPS))],
        out_specs=pl.BlockSpec((1, nt * tm, D_MODEL), lambda b, t: (b, t, 0)),
        out_shape=jax.ShapeDtypeStruct((B, rows, D_MODEL), F32),
        compiler_params=pltpu.CompilerParams(dimension_semantics=("parallel", "parallel"),
                                             vmem_limit_bytes=BIG_VMEM_LIMIT),
        name="merge_ctx" if context else "merge",
    )(*token_args, mod8, wh_bf, wt_bf, wbr_bf, wo_bf, g_post, m_norm, ln_g, ln_b, ws_bf, bs_t)


def _rope_tables(n_lat):
    rows = n_lat // GRID_W
    row = jnp.repeat(jnp.arange(rows), GRID_W).astype(F32)
    col = jnp.tile(jnp.arange(GRID_W), rows).astype(F32)
    n_freq = HEAD_DIM // 4
    inv = ROPE_THETA ** (-jnp.arange(n_freq, dtype=F32) / n_freq)
    ang = jnp.concatenate([row[:, None] * inv, col[:, None] * inv], axis=-1)
    cos, sin = jnp.cos(ang), jnp.sin(ang)
    cos_h = jnp.concatenate([cos, cos], axis=-1)
    sin_h = jnp.concatenate([-sin, sin], axis=-1)
    cos_t = jnp.concatenate([jnp.ones((CTX_LEN, HEAD_DIM), F32), cos_h], axis=0)
    sin_t = jnp.concatenate([jnp.zeros((CTX_LEN, HEAD_DIM), F32), sin_h], axis=0)
    return jnp.tile(cos_t, (1, LANES // HEAD_DIM)), jnp.tile(sin_t, (1, LANES // HEAD_DIM))


def _gather_cols(w, names):
    sl = _ref_slices()
    runs = []
    for n in names:
        start, size = sl[n]
        if runs and runs[-1][1] == start:
            runs[-1][1] = start + size
        else:
            runs.append([start, start + size])
    return jnp.concatenate([w[:, a:b].astype(BF16) for a, b in runs], axis=1)


def kernel(x, c, ctx, c_ctx, w_mod, b_mod, g_pre, g_post, w_in, a_q_norm, a_k_norm, w_sink, sg_ln_g, sg_ln_b,
           sg_w, sg_b, m_conv, m_b_i, m_b_f, m_norm, w_branch, w_out):
    B, n_lat, D = x.shape
    depth = w_mod.shape[0]
    assert depth == 2, "the context stream is only carried from the first layer to the last"
    T = CTX_LEN + n_lat
    n_ctx_chunks = CTX_LEN // CHUNK
    n_lat_chunks = n_lat // CHUNK
    rope = _rope_tables(n_lat)
    gmat = (jnp.arange(LANES)[:, None] // HEAD_DIM == jnp.arange(LANES)[None, :] // HEAD_DIM).astype(BF16)

    hc, hx = ctx, x
    mod_rows = 16
    cc = jnp.zeros((mod_rows, D), F32).at[:B].set(c).at[B].set(c_ctx)
    tm_tok = CTX_LEN
    n_ctx_tiles = CTX_LEN // tm_tok
    n_lat_tiles = n_lat // tm_tok

    for l in range(depth):
        need_ctx = l < depth - 1
        mod = _modulation(cc, w_mod, b_mod[l][None, :], l)
        shift, scale, gate = mod[:, :D], mod[:, D:2 * D], mod[:, 2 * D:]
        mult = g_pre[l][None, :] * (1.0 + scale)
        zero = jnp.zeros((B, D), F32)
        mod8 = jnp.stack([jnp.broadcast_to(mult[B], (B, D)), jnp.broadcast_to(shift[B], (B, D)),
                          jnp.broadcast_to(gate[B], (B, D)), mult[:B], shift[:B], gate[:B], zero, zero],
                         axis=1)

        w_head = _gather_cols(w_in[l], _HEAD_ORDER)
        w_tail = _gather_cols(w_in[l], _TAIL_ORDER)
        w_lstm = jnp.pad(_gather_cols(w_in[l], ('m_i', 'm_f')), ((0, 0), (0, LANES - 2 * N_CHAINS)))
        p_main, p_gates, zq, zkt, zv = _in_projection(hc, hx, mod8, w_head, w_lstm, m_conv[l])

        qn = jnp.tile(a_q_norm[l], LANES // HEAD_DIM)[None, :]
        kn = jnp.tile(a_k_norm[l], LANES // HEAD_DIM)[None, :]
        ya, yw = _attention(p_main, rope, qn, kn, gmat, w_sink[l],
                            q_tile0=n_ctx_chunks, n_q_tiles=n_lat_chunks, nk=T, banded=True)
        if need_ctx:
            ya_c, yw_c = _attention(p_main, rope, qn, kn, gmat, w_sink[l],
                                    q_tile0=0, n_q_tiles=n_ctx_chunks, nk=CTX_LEN, banded=False)

        gates_row = jnp.swapaxes(p_gates[:, :, :2 * N_CHAINS], 1, 2)
        bias_row = jnp.broadcast_to(
            jnp.concatenate([m_b_i[l].reshape(-1), m_b_f[l].reshape(-1)])[:, None], (2 * N_CHAINS, T))
        ct, ur, ar = _mlstm_gates(gates_row, bias_row)
        hf, hb = _mlstm_scan(zq, zkt, zv, ct, ur, ar)

        weights = (mod8, w_head, w_tail, w_branch[l].astype(BF16), w_out[l].astype(BF16),
                   g_post[l][None, :], m_norm[l][None, :],
                   sg_ln_g[l][None, :], sg_ln_b[l][None, :], sg_w[l].astype(BF16), sg_b[l].T)
        hx_new = _merge(ya, yw, p_main, hf, hb, hx, *weights, context=False)
        if need_ctx:
            hc = _merge(ya_c, yw_c, p_main, hf, hb, hc, *weights, context=True)
        hx = hx_new
    return hx
```

```python
import functools

import jax
import jax.numpy as jnp
from jax import lax
from jax.experimental import pallas as pl
from jax.experimental.pallas import tpu as pltpu

F32 = jnp.float32
BF16 = jnp.bfloat16

D_MODEL = 1024
GRID_W = 64
CTX_LEN = 256
N_BRANCH = 4
BRANCH_W = 512
HEAD_DIM = 64
N_HEADS = 8
N_KV = 2
GROUP = N_HEADS // N_KV
WINDOW = 128
CHUNK = 128
B_GROUPS = 4
M_HEADS = 4
M_HD = 128
N_CHAINS = 2 * M_HEADS
ROPE_THETA = 10000.0
EPS = 1e-6
LOG2E = 1.4426950408889634
ATTN_TILES_PER_STEP = 2

LANES = 128
VMEM_LIMIT = 48 * 1024 * 1024
BIG_VMEM_LIMIT = 58 * 1024 * 1024

COL_AQ = 0
COL_WQ = 512
COL_AK = 1024
COL_AV = 1152
COL_WK = 1280
COL_WV = 1408
COL_GU = 1536
COL_GV = 2048
COL_MQKV = 2560
N_PLAIN = COL_MQKV
N_MAIN = 4096
_REF_LAYOUT = (
    ('a_q', 512), ('a_k', 128), ('a_v', 128), ('a_gate', 512),
    ('w_q', 512), ('w_k', 128), ('w_v', 128), ('w_gate', 512),
    ('g_u', 512), ('g_v', 512), ('g_gate', 512),
    ('m_qkv', 1536), ('m_i', 8), ('m_f', 8), ('m_o', 512), ('m_gate', 512),
    ('merge', 4096),
)
_MAIN_ORDER = ('a_q', 'w_q', 'a_k', 'a_v', 'w_k', 'w_v', 'g_u', 'g_v', 'm_qkv')
_HEAD_ORDER = tuple(name for name, _ in _REF_LAYOUT[:[n for n, _ in _REF_LAYOUT].index('m_i')])
_TAIL_ORDER = tuple(name for name, _ in _REF_LAYOUT[[n for n, _ in _REF_LAYOUT].index('m_f') + 1:])


def _ref_slices():
    out, start = {}, 0
    for name, size in _REF_LAYOUT:
        out[name] = (start, size)
        start += size
    return out


def _piece_cols(order):
    out, start = {}, 0
    for name in order:
        out[name] = start
        start += _ref_slices()[name][1]
    out['end'] = start
    return out


def _gate_col(name):
    if name in _HEAD_ORDER:
        return 0, _piece_cols(_HEAD_ORDER)[name]
    return 1, _piece_cols(_TAIL_ORDER)[name]


GCOL_BRANCH_GATE = tuple(_gate_col(n) for n in ('a_gate', 'w_gate', 'g_gate', 'm_gate'))
GCOL_MO = _gate_col('m_o')
GCOL_MERGE = _gate_col('merge')
N_HEAD = _piece_cols(_HEAD_ORDER)['end']
N_TAIL = _piece_cols(_TAIL_ORDER)['end']
assert all(c % LANES == 0 for _, c in GCOL_BRANCH_GATE + (GCOL_MO, GCOL_MERGE))


def _params(*sem):
    return pltpu.CompilerParams(dimension_semantics=sem, vmem_limit_bytes=VMEM_LIMIT)


def _sigmoid(x):
    return 1.0 / (1.0 + jnp.exp2(x * (-LOG2E)))


def _silu(x):
    return x * _sigmoid(x)


def _gelu_tanh(x):
    c = 0.7978845608028654
    return 0.5 * x * (1.0 + jnp.tanh(c * (x + 0.044715 * (x * x * x))))


def _log_sigmoid(x):
    return -(jnp.maximum(-x, 0.0) + jnp.log1p(jnp.exp(-jnp.abs(x))))


def _mod_kernel(c_ref, w_ref, b_ref, o_ref):
    a = _silu(c_ref[...]).astype(BF16)
    o_ref[...] = jnp.dot(a, w_ref[0].astype(BF16), preferred_element_type=F32) + b_ref[...]


def _modulation(cc, w_mod, b_mod, layer):
    rows = cc.shape[0]
    n = w_mod.shape[2]
    tn = 1024
    return pl.pallas_call(
        _mod_kernel,
        grid=(n // tn,),
        in_specs=[pl.BlockSpec((rows, D_MODEL), lambda j: (0, 0)),
                  pl.BlockSpec((1, D_MODEL, tn), lambda j: (layer, 0, j)),
                  pl.BlockSpec((1, tn), lambda j: (0, j))],
        out_specs=pl.BlockSpec((rows, tn), lambda j: (0, j)),
        out_shape=jax.ShapeDtypeStruct((rows, n), F32),
        compiler_params=_params("parallel"),
        name="modulation",
    )(cc, w_mod, b_mod)


MOD_ROWS = 8


def _norm_mod(x, mult, shift):
    ms = jnp.mean(x * x, axis=-1, keepdims=True)
    return (x * lax.rsqrt(ms + EPS) * mult + shift).astype(BF16)


def _conv_silu_slab(x, w_ref, col, no_prev, no_next):
    T = x.shape[0]
    prev = jnp.where(no_prev, 0.0, pltpu.roll(x, 1, axis=0))
    nxt = jnp.where(no_next, 0.0, pltpu.roll(x, T - 1, axis=0))
    return _silu(prev * w_ref[0:1, col:col + LANES] + x * w_ref[1:2, col:col + LANES]
                 + nxt * w_ref[2:3, col:col + LANES])


W_TILE = 2 * LANES


def _main_tiles():
    sl = _ref_slices()
    tiles = []
    for name in _MAIN_ORDER:
        start, size = sl[name]
        if size < W_TILE:
            if start % W_TILE == 0:
                tiles.append(start // W_TILE)
            continue
        assert start % W_TILE == 0 and size % W_TILE == 0
        tiles.extend(range(start // W_TILE, (start + size) // W_TILE))
    return tiles


def _inproj_kernel(tbl_ref, hc_ref, hx_ref, mod_ref, w0_ref, w1_ref, wl_ref, cw_ref,
                   o_ref, og_ref, zq_ref, zkt_ref, zv_ref, xn_ref, *, n_lat, n_plain):
    del tbl_ref
    w_refs = (w0_ref, w1_ref)
    j = pl.program_id(1)
    rb = CTX_LEN
    T = rb + n_lat

    @pl.when(j == 0)
    def _():
        xn_ref[0:rb, :] = _norm_mod(hc_ref[0], mod_ref[0, 0:1, :], mod_ref[0, 1:2, :])

        def norm_rows(r, carry):
            off = pl.multiple_of(r * rb, rb)
            x = hx_ref[0, pl.ds(off, rb), :]
            xn_ref[pl.ds(rb + off, rb), :] = _norm_mod(x, mod_ref[0, 3:4, :], mod_ref[0, 4:5, :])
            return carry

        lax.fori_loop(0, n_lat // rb, norm_rows, 0)
        og_ref[0] = jnp.dot(xn_ref[...], wl_ref[...], preferred_element_type=F32)

    @pl.when(j < n_plain)
    def _():
        w = jnp.concatenate([r[...] for r in w_refs], axis=1)
        o_ref[0] = jnp.dot(xn_ref[...], w, preferred_element_type=F32).astype(o_ref.dtype)

    for part, z_ref in enumerate((zq_ref, zkt_ref, zv_ref)):
        @pl.when(j == n_plain + part)
        def _(part=part, z_ref=z_ref):
            row = lax.broadcasted_iota(jnp.int32, (T, 1), 0)
            no_prev = jnp.logical_or(row == 0, row == CTX_LEN)
            no_next = jnp.logical_or(row == CTX_LEN - 1, row == T - 1)
            ys = [jnp.dot(xn_ref[...], r[...], preferred_element_type=F32) for r in w_refs]
            for s in range(BRANCH_W // LANES):
                lo = s * LANES
                y = ys[lo // W_TILE][:, lo % W_TILE:lo % W_TILE + LANES]
                z = _conv_silu_slab(y, cw_ref, part * BRANCH_W + lo, no_prev, no_next)
                if part == 0:
                    z_ref[0, :, lo:lo + LANES] = (z * (M_HD ** -0.5)).astype(z_ref.dtype)
                elif part == 1:
                    z_ref[0, lo:lo + LANES, :] = z.T.astype(z_ref.dtype)
                else:
                    z_ref[0, :, lo:lo + LANES] = z.astype(z_ref.dtype)


def _in_projection(hc, hx, mod8, w2_bf, wl_bf, conv_w):
    B, n_lat, _ = hx.shape
    T = CTX_LEN + n_lat
    tn = BRANCH_W
    per_step = tn // W_TILE
    assert per_step == 2
    tiles = _main_tiles()
    n_plain = N_PLAIN // tn
    n_steps = len(tiles) // per_step
    assert len(tiles) * W_TILE == N_MAIN and n_steps == n_plain + 3
    table = jnp.asarray(tiles, jnp.int32)

    def resident(shape):
        return pl.BlockSpec(shape, lambda b, j, tbl: (b, 0, 0), pipeline_mode=pl.Buffered(1))

    h_specs = [pl.BlockSpec((1, a.shape[1], D_MODEL), lambda b, j, tbl: (b, 0, 0)) for a in (hc, hx)]
    w_specs = [pl.BlockSpec((D_MODEL, W_TILE), lambda b, j, tbl, i=i: (0, tbl[per_step * j + i]))
               for i in range(per_step)]
    row_major = jax.ShapeDtypeStruct((B, T, BRANCH_W), BF16)
    grid_spec = pltpu.PrefetchScalarGridSpec(
        num_scalar_prefetch=1,
        grid=(B, n_steps),
        in_specs=h_specs + [pl.BlockSpec((1, MOD_ROWS, D_MODEL), lambda b, j, tbl: (b, 0, 0))] + w_specs
        + [pl.BlockSpec((D_MODEL, LANES), lambda b, j, tbl: (0, 0)),
           pl.BlockSpec((3, 3 * BRANCH_W), lambda b, j, tbl: (0, 0))],
        out_specs=[pl.BlockSpec((1, T, tn), lambda b, j, tbl: (b, 0, jnp.minimum(j, n_plain - 1))),
                   resident((1, T, LANES)),
                   resident((1, T, BRANCH_W)), resident((1, BRANCH_W, T)), resident((1, T, BRANCH_W))],
        scratch_shapes=[pltpu.VMEM((T, D_MODEL), BF16)])
    return pl.pallas_call(
        functools.partial(_inproj_kernel, n_lat=n_lat, n_plain=n_plain),
        grid_spec=grid_spec,
        out_shape=[jax.ShapeDtypeStruct((B, T, N_PLAIN), BF16),
                   jax.ShapeDtypeStruct((B, T, LANES), F32),
                   row_major, jax.ShapeDtypeStruct((B, BRANCH_W, T), BF16), row_major],
        compiler_params=pltpu.CompilerParams(dimension_semantics=("parallel", "arbitrary"),
                                             vmem_limit_bytes=BIG_VMEM_LIMIT),
        name="in_projection",
    )(table, hc, hx, mod8, w2_bf, w2_bf, wl_bf, conv_w)


def _rope_slab(x, cos, sin):
    lane = lax.broadcasted_iota(jnp.int32, x.shape, 1)
    first_half = (lane & (HEAD_DIM - 1)) < (HEAD_DIM // 2)
    partner = jnp.where(first_half,
                        pltpu.roll(x, LANES - HEAD_DIM // 2, axis=1),
                        pltpu.roll(x, HEAD_DIM // 2, axis=1))
    return x * cos + partner * sin


def _head_norm_slab(x, gain, gmat):
    ss = jnp.dot((x * x).astype(BF16), gmat, preferred_element_type=F32)
    return x * lax.rsqrt(ss * (1.0 / HEAD_DIM) + EPS) * gain


def _prep_kv(k_ref, v_ref, cosk_ref, sink_ref, kn_ref, gmat_ref, kp_ref, vt_ref, nk, use_norm):
    k = k_ref[0, 0:nk, :].astype(F32)
    if use_norm:
        k = _head_norm_slab(k, kn_ref[...], gmat_ref[...])
    k = _rope_slab(k, cosk_ref[0:nk, :], sink_ref[0:nk, :]).astype(BF16)
    for g in range(N_KV):
        kp_ref[g, 0:nk, :] = k[:, g * HEAD_DIM:(g + 1) * HEAD_DIM]
    for blk in range(nk // CHUNK):
        v = v_ref[0, blk * CHUNK:(blk + 1) * CHUNK, :].astype(F32)
        vt_ref[blk] = v.T.astype(BF16)


def _prep_qt(q_ref, cosq_ref, sinq_ref, qn_ref, gmat_ref, use_norm, r0, tq):
    cos = cosq_ref[r0:r0 + tq, :]
    sin = sinq_ref[r0:r0 + tq, :]
    xts = []
    for s in range(N_HEADS // 2):
        x = q_ref[0, r0:r0 + tq, s * LANES:(s + 1) * LANES].astype(F32)
        if use_norm:
            x = _head_norm_slab(x, qn_ref[...], gmat_ref[...])
        x = _rope_slab(x, cos, sin) * (HEAD_DIM ** -0.5 * LOG2E)
        xts.append(x.T.astype(BF16))
    per = GROUP // 2
    out = []
    for g in range(N_KV):
        parts = []
        for s in range(g * per, (g + 1) * per):
            parts.append(xts[s][:HEAD_DIM, :])
            parts.append(xts[s][HEAD_DIM:, :])
        out.append(jnp.concatenate(parts, axis=1))
    return out


def _heads_from_t(ot, tq):
    slabs = []
    for j in range(GROUP // 2):
        two = jnp.concatenate([ot[:, (2 * j) * tq:(2 * j + 1) * tq],
                               ot[:, (2 * j + 1) * tq:(2 * j + 2) * tq]], axis=0)
        slabs.append(two.T)
    return jnp.concatenate(slabs, axis=1)


def _window_pieces(wsink_ref, q_ref, cosq_ref, sinq_ref, o_ref, kp_ref, vt_ref, step, *,
                   tq, nsub, banded, n_lat_blocks):
    cols = GROUP * tq
    gw = GROUP * HEAD_DIM
    n_ctx_chunks = CTX_LEN // CHUNK
    lane = lax.broadcasted_iota(jnp.int32, (1, cols), 1)
    sink_rows = []
    for g in range(N_KV):
        sink_row = jnp.zeros((1, cols), F32)
        for h in range(GROUP):
            in_head = jnp.logical_and(lane >= h * tq, lane < (h + 1) * tq)
            sink_row = jnp.where(in_head, wsink_ref[g * GROUP + h] * LOG2E, sink_row)
        sink_rows.append(sink_row)
    if banded:
        diff = (lax.broadcasted_iota(jnp.int32, (CHUNK, cols), 0)
                - (lax.broadcasted_iota(jnp.int32, (CHUNK, cols), 1) & (tq - 1)))

    for sub in range(nsub):
        i = step * nsub + sub
        qts = _prep_qt(q_ref, cosq_ref, sinq_ref, None, None, False, sub * tq, tq)
        yield
        if banded:
            prev_blk = i + n_ctx_chunks - 1
            own_blk = i + n_ctx_chunks
            next_blk = jnp.minimum(i + n_ctx_chunks + 1, n_ctx_chunks + n_lat_blocks - 1)
            ok_prev = diff >= jnp.where(i > 0, 0, CHUNK)
            ok_next = diff <= jnp.where(i < n_lat_blocks - 1, 0, -CHUNK)
            plan = ([(cb, None) for cb in range(n_ctx_chunks)]
                    + [(prev_blk, ok_prev), (own_blk, None), (next_blk, ok_next)])
        else:
            plan = [(cb, None) for cb in range(n_ctx_chunks)]

        for g in range(N_KV):
            qt = qts[g]
            sink_row = sink_rows[g]
            blocks = []
            for blk, ok in plan:
                off = blk * CHUNK
                if not isinstance(off, int):
                    off = pl.multiple_of(off, CHUNK)
                st = jnp.dot(kp_ref[g, pl.ds(off, CHUNK), :], qt, preferred_element_type=F32)
                blocks.append((st if ok is None else jnp.where(ok, st, -jnp.inf), blk))
                yield
            m = sink_row
            for st, _ in blocks:
                m = jnp.maximum(m, jnp.max(st, axis=0, keepdims=True))
            l = jnp.exp2(sink_row - m)
            acc = None
            for st, blk in blocks:
                p = jnp.exp2(st - m)
                l = l + jnp.sum(p, axis=0, keepdims=True)
                pv = jnp.dot(vt_ref[blk, g * HEAD_DIM:(g + 1) * HEAD_DIM, :], p.astype(BF16),
                             preferred_element_type=F32)
                acc = pv if acc is None else acc + pv
                yield
            o_ref[0, sub * tq:(sub + 1) * tq, g * gw:(g + 1) * gw] = (
                _heads_from_t(acc / l, tq).astype(o_ref.dtype))
            yield


def _attn_kernel(wsink_ref, qa_ref, ka_ref, va_ref, qw_ref, kw_ref, vw_ref, cosq_ref, sinq_ref,
                 cosk_ref, sink_ref, qn_ref, kn_ref, gmat_ref, oa_ref, ow_ref,
                 kpa_ref, vta_ref, s_ref, kpw_ref, vtw_ref, *, tq, nsub, nk, ck, banded, n_lat_blocks):
    step = pl.program_id(1)

    @pl.when(step == 0)
    def _():
        _prep_kv(ka_ref, va_ref, cosk_ref, sink_ref, kn_ref, gmat_ref, kpa_ref, vta_ref, nk, True)
        _prep_kv(kw_ref, vw_ref, cosk_ref, sink_ref, None, None, kpw_ref, vtw_ref, nk, False)

    window = _window_pieces(wsink_ref, qw_ref, cosq_ref, sinq_ref, ow_ref, kpw_ref, vtw_ref, step,
                            tq=tq, nsub=nsub, banded=banded, n_lat_blocks=n_lat_blocks)

    qts = [_prep_qt(qa_ref, cosq_ref, sinq_ref, qn_ref, gmat_ref, True, sub * tq, tq) for sub in range(nsub)]
    items = [(sub, g) for sub in range(nsub) for g in range(N_KV)]
    n_chunks = nk // ck
    blocks_per_chunk = ck // CHUNK
    gw = GROUP * HEAD_DIM

    def score_chunk(k, c, m):
        sub, g = items[k]
        st = jnp.dot(kpa_ref[g, c * ck:(c + 1) * ck, :], qts[sub][g], preferred_element_type=F32)
        s_ref[k, c * ck:(c + 1) * ck, :] = st
        cm = jnp.max(st, axis=0, keepdims=True)
        return cm if m is None else jnp.maximum(m, cm)

    def value_chunk(k, c, m, l, acc):
        _, g = items[k]
        p = jnp.exp2(s_ref[k, c * ck:(c + 1) * ck, :] - m)
        cl = jnp.sum(p, axis=0, keepdims=True)
        vt = jnp.concatenate(
            [vta_ref[c * blocks_per_chunk + j, g * HEAD_DIM:(g + 1) * HEAD_DIM, :]
             for j in range(blocks_per_chunk)], axis=1)
        pv = jnp.dot(vt, p.astype(BF16), preferred_element_type=F32)
        return (cl, pv) if l is None else (l + cl, acc + pv)

    m_next = None
    for c in range(n_chunks):
        m_next = score_chunk(0, c, m_next)
        next(window, None)
    for k, (sub, g) in enumerate(items):
        m, m_next = m_next, None
        l = acc = None
        for c in range(n_chunks):
            l, acc = value_chunk(k, c, m, l, acc)
            if k + 1 < len(items):
                m_next = score_chunk(k + 1, c, m_next)
            next(window, None)
        oa_ref[0, sub * tq:(sub + 1) * tq, g * gw:(g + 1) * gw] = _heads_from_t(acc / l, tq).astype(oa_ref.dtype)
    for _ in window:
        pass


def _attention(p_main, rope, qn, kn, gmat, w_sink, *, q_tile0, n_q_tiles, nk, banded):
    B, T, _ = p_main.shape
    cos, sin = rope
    tq = CHUNK
    nsub = ATTN_TILES_PER_STEP
    tb = nsub * tq
    blk0 = q_tile0 // nsub
    ck = 256
    assert WINDOW == CHUNK
    n_lat_blocks = (T - CTX_LEN) // CHUNK
    kernel = functools.partial(_attn_kernel, tq=tq, nsub=nsub, nk=nk, ck=ck, banded=banded,
                               n_lat_blocks=n_lat_blocks)

    def q_spec(col):
        return pl.BlockSpec((1, tb, BRANCH_W), lambda b, i: (b, i + blk0, col // BRANCH_W))

    def kv_spec(col):
        return pl.BlockSpec((1, T, LANES), lambda b, i: (b, 0, col // LANES))

    def const(shape):
        return pl.BlockSpec(shape, lambda b, i: (0, 0))

    out = jax.ShapeDtypeStruct((B, n_q_tiles * tq, BRANCH_W), BF16)
    kv_scratch = [pltpu.VMEM((N_KV, nk, HEAD_DIM), BF16), pltpu.VMEM((nk // CHUNK, LANES, CHUNK), BF16)]
    return pl.pallas_call(
        kernel,
        grid=(B, n_q_tiles // nsub),
        in_specs=[pl.BlockSpec(memory_space=pltpu.SMEM),
                  q_spec(COL_AQ), kv_spec(COL_AK), kv_spec(COL_AV),
                  q_spec(COL_WQ), kv_spec(COL_WK), kv_spec(COL_WV),
                  pl.BlockSpec((tb, LANES), lambda b, i: (i + blk0, 0)),
                  pl.BlockSpec((tb, LANES), lambda b, i: (i + blk0, 0)),
                  const((T, LANES)), const((T, LANES)),
                  const((1, LANES)), const((1, LANES)), const((LANES, LANES))],
        out_specs=[pl.BlockSpec((1, tb, BRANCH_W), lambda b, i: (b, i, 0)),
                   pl.BlockSpec((1, tb, BRANCH_W), lambda b, i: (b, i, 0))],
        out_shape=[out, out],
        scratch_shapes=kv_scratch + [pltpu.VMEM((nsub * N_KV, nk, GROUP * tq), F32)] + kv_scratch,
        compiler_params=_params("parallel", "arbitrary"),
        name="attention",
    )(w_sink, p_main, p_main, p_main, p_main, p_main, p_main, cos, sin, cos, sin, qn, kn, gmat)


def _spatial_gate_tile(gu, gv, lng_ref, lnb_ref, ws_ref, bs_ref):
    u = _gelu_tanh(gu.astype(F32))
    v = _gelu_tanh(gv.astype(F32))
    mu = jnp.mean(v, axis=-1, keepdims=True)
    vc = v - mu
    var = jnp.mean(vc * vc, axis=-1, keepdims=True)
    vn = (vc * lax.rsqrt(var + EPS) * lng_ref[...] + lnb_ref[...]).astype(BF16)
    gw = BRANCH_W // B_GROUPS
    chunks = []
    for c in range(gu.shape[0] // CHUNK):
        cols = []
        for g in range(B_GROUPS):
            mixed = jnp.dot(ws_ref[g], vn[c * CHUNK:(c + 1) * CHUNK, g * gw:(g + 1) * gw],
                            preferred_element_type=F32)
            cols.append(mixed + bs_ref[:, g:g + 1])
        chunks.append(jnp.concatenate(cols, axis=1))
    return u * jnp.concatenate(chunks, axis=0)


def _split3(x):
    x1 = x.astype(BF16)
    r = x - x1.astype(F32)
    x2 = r.astype(BF16)
    x3 = (r - x2.astype(F32)).astype(BF16)
    return x1, x2, x3


def _mgates_kernel(g_ref, bias_ref, ct_ref, ur_ref, ar_ref, *, T):
    L = CHUNK
    n_chunks = T // L
    n_ctx = CTX_LEN // L
    gall = g_ref[0] + bias_ref[...]
    i_all = gall[0:N_CHAINS]
    f_all = _log_sigmoid(gall[N_CHAINS:2 * N_CHAINS])
    fwd_rows = lax.broadcasted_iota(jnp.int32, (N_CHAINS, L), 0) < M_HEADS
    fwd_col = lax.broadcasted_iota(jnp.int32, (N_CHAINS, 1), 0) < M_HEADS
    lane = lax.broadcasted_iota(jnp.int32, (N_CHAINS, L), 1)
    s_idx = lax.broadcasted_iota(jnp.int32, (L, L), 0)
    t_idx = lax.broadcasted_iota(jnp.int32, (L, L), 1)
    sum_upto = jnp.where(s_idx <= t_idx, 1.0, 0.0).astype(BF16)
    sum_from = jnp.where(s_idx >= t_idx, 1.0, 0.0).astype(BF16)

    def cumsum_dir(f):
        bf = None
        bb = None
        for part in _split3(f):
            a = jnp.dot(part, sum_upto, preferred_element_type=F32)
            b = jnp.dot(part, sum_from, preferred_element_type=F32)
            bf = a if bf is None else bf + a
            bb = b if bb is None else bb + b
        return jnp.where(fwd_rows, bf, bb)

    def cummax_dir(u):
        xf = u
        xb = u
        sh = 1
        while sh < L:
            xf = jnp.maximum(xf, jnp.where(lane >= sh, pltpu.roll(xf, sh, axis=1), -jnp.inf))
            xb = jnp.maximum(xb, jnp.where(lane < L - sh, pltpu.roll(xb, L - sh, axis=1), -jnp.inf))
            sh *= 2
        return jnp.where(fwd_rows, xf, xb)

    b_c, u_c, mloc_c, bl_c, g_c, gmax_c = [], [], [], [], [], []
    for c in range(n_chunks):
        f = f_all[:, c * L:(c + 1) * L]
        i = i_all[:, c * L:(c + 1) * L]
        b = cumsum_dir(f)
        u = i - b
        bl = jnp.sum(f, axis=1, keepdims=True)
        g = bl + u
        b_c.append(b)
        u_c.append(u)
        mloc_c.append(b + cummax_dir(u))
        bl_c.append(bl)
        g_c.append(g)
        gmax_c.append(jnp.max(g, axis=1, keepdims=True))

    def scan(order):
        m = jnp.zeros((N_CHAINS, 1), F32)
        prev, new = {}, {}
        for c in order:
            prev[c] = m
            m = jnp.maximum(bl_c[c] + m, gmax_c[c])
            new[c] = m
        return prev, new

    order_f = list(range(n_chunks))
    order_b = [n_ctx - 1 - j for j in range(n_ctx)] + [n_chunks + n_ctx - 1 - j for j in range(n_ctx, n_chunks)]
    prev_f, new_f = scan(order_f)
    prev_b, new_b = scan(order_b)

    for c in range(n_chunks):
        m_prev = jnp.where(fwd_col, prev_f[c], prev_b[c])
        m_new = jnp.where(fwd_col, new_f[c], new_b[c])
        b = b_c[c]
        m_inter = b + m_prev
        m_t = jnp.maximum(m_inter, mloc_c[c])
        pack = jnp.concatenate([b - m_t, jnp.exp(m_inter - m_t), jnp.exp(-m_t),
                                jnp.zeros((LANES - 3 * N_CHAINS, L), F32)], axis=0)
        ct_ref[0, c * L:(c + 1) * L, :] = pack.T
        ur_ref[0, :, c * L:(c + 1) * L] = jnp.concatenate([u_c[c], jnp.exp(g_c[c] - m_new)], axis=0)
        ar_ref[0, c] = jnp.broadcast_to(jnp.exp(bl_c[c] + m_prev - m_new), (N_CHAINS, LANES))


def _mlstm_gates(gates_row, bias_row):
    B, _, T = gates_row.shape
    n_chunks = T // CHUNK
    return pl.pallas_call(
        functools.partial(_mgates_kernel, T=T),
        grid=(B,),
        in_specs=[pl.BlockSpec((1, 2 * N_CHAINS, T), lambda b: (b, 0, 0)),
                  pl.BlockSpec((2 * N_CHAINS, T), lambda b: (0, 0))],
        out_specs=[pl.BlockSpec((1, T, LANES), lambda b: (b, 0, 0)),
                   pl.BlockSpec((1, 2 * N_CHAINS, T), lambda b: (b, 0, 0)),
                   pl.BlockSpec((1, n_chunks, N_CHAINS, LANES), lambda b: (b, 0, 0, 0))],
        out_shape=[jax.ShapeDtypeStruct((B, T, LANES), F32),
                   jax.ShapeDtypeStruct((B, 2 * N_CHAINS, T), F32),
                   jax.ShapeDtypeStruct((B, n_chunks, N_CHAINS, LANES), F32)],
        compiler_params=_params("parallel"),
        name="mlstm_gates",
    )(gates_row, bias_row)


SCAN_BATCH = 8


def _mlstm_kernel(qf_ref, ktf_ref, vf_ref, ctf_ref, urf_ref, arf_ref,
                  qb_ref, ktb_ref, vb_ref, ctb_ref, urb_ref, arb_ref, of_ref, ob_ref, st_ref):
    L = CHUNK

    @pl.when(pl.program_id(1) == 0)
    def _():
        st_ref[...] = jnp.zeros_like(st_ref)

    t_idx = lax.broadcasted_iota(jnp.int32, (L, L), 0)
    s_idx = lax.broadcasted_iota(jnp.int32, (L, L), 1)
    ones = jnp.ones((L, M_HD), BF16)
    dirs = ((qf_ref, ktf_ref, vf_ref, ctf_ref, urf_ref, arf_ref, of_ref),
            (qb_ref, ktb_ref, vb_ref, ctb_ref, urb_ref, arb_ref, ob_ref))
    for bi in range(SCAN_BATCH):
        for d, (q_ref, kt_ref, v_ref, ct_ref, ur_ref, ar_ref, o_ref) in enumerate(dirs):
            mask = (s_idx <= t_idx) if d == 0 else (s_idx >= t_idx)
            ct = ct_ref[bi]
            ur = ur_ref[bi]
            ar = ar_ref[bi, 0]
            outs = []
            for h in range(M_HEADS):
                c = d * M_HEADS + h
                q = q_ref[bi, :, h * M_HD:(h + 1) * M_HD]
                kt = kt_ref[bi, h * M_HD:(h + 1) * M_HD, :]
                v = v_ref[bi, :, h * M_HD:(h + 1) * M_HD]
                v_ext = jnp.concatenate([v, ones], axis=1)
                cq = ct[:, c:c + 1]
                w_inter = ct[:, N_CHAINS + c:N_CHAINS + c + 1]
                e_negm = ct[:, 2 * N_CHAINS + c:2 * N_CHAINS + c + 1]
                wk_row = ur[N_CHAINS + c:N_CHAINS + c + 1, :]
                w_intra = jnp.exp(jnp.where(mask, cq + ur[c:c + 1, :], -jnp.inf))
                s = jnp.dot(q, kt, preferred_element_type=F32) * w_intra
                st_prev = st_ref[bi, d, h]
                lhs = jnp.concatenate([s.astype(BF16), (q.astype(F32) * w_inter).astype(BF16)], axis=1)
                rhs = jnp.concatenate([v_ext, st_prev.astype(BF16)], axis=0)
                tot = jnp.dot(lhs, rhs, preferred_element_type=F32)
                den = jnp.maximum(jnp.abs(tot[:, M_HD:]), e_negm)
                outs.append(tot[:, :M_HD] / den)
                kt_w = (kt.astype(F32) * wk_row).astype(BF16)
                a = jnp.concatenate([ar[c:c + 1, :], ar[c:c + 1, :]], axis=1)
                st_ref[bi, d, h] = a * st_prev + jnp.dot(kt_w, v_ext, preferred_element_type=F32)
            o_ref[0, bi] = jnp.concatenate(outs, axis=1).astype(o_ref.dtype)


def _mlstm_scan(zq, zkt, zv, ct, ur, ar):
    B, T, _ = zq.shape
    n_chunks = T // CHUNK
    n_ctx = CTX_LEN // CHUNK

    def fwd(j):
        return j

    def bwd(j):
        return jnp.where(j < n_ctx, n_ctx - 1 - j, n_chunks + n_ctx - 1 - j)

    nb = SCAN_BATCH
    assert B % nb == 0

    def specs(order):
        return [pl.BlockSpec((nb, CHUNK, BRANCH_W), lambda b, j: (b, order(j), 0)),
                pl.BlockSpec((nb, BRANCH_W, CHUNK), lambda b, j: (b, 0, order(j))),
                pl.BlockSpec((nb, CHUNK, BRANCH_W), lambda b, j: (b, order(j), 0)),
                pl.BlockSpec((nb, CHUNK, LANES), lambda b, j: (b, order(j), 0)),
                pl.BlockSpec((nb, 2 * N_CHAINS, CHUNK), lambda b, j: (b, 0, order(j))),
                pl.BlockSpec((nb, 1, N_CHAINS, LANES), lambda b, j: (b, order(j), 0, 0))]

    out = pl.pallas_call(
        _mlstm_kernel,
        grid=(B // nb, n_chunks),
        in_specs=specs(fwd) + specs(bwd),
        out_specs=[pl.BlockSpec((1, nb, CHUNK, BRANCH_W), lambda b, j: (0, b, fwd(j), 0)),
                   pl.BlockSpec((1, nb, CHUNK, BRANCH_W), lambda b, j: (0, b, bwd(j), 0))],
        out_shape=[jax.ShapeDtypeStruct((1, B, T, BRANCH_W), F32),
                   jax.ShapeDtypeStruct((1, B, T, BRANCH_W), F32)],
        scratch_shapes=[pltpu.VMEM((nb, 2, M_HEADS, M_HD, 2 * M_HD), F32)],
        compiler_params=_params("parallel", "arbitrary"),
        name="mlstm_scan",
    )(zq, zkt, zv, ct, ur, ar, zq, zkt, zv, ct, ur, ar)
    return out[0][0], out[1][0]


def _merge_kernel(*refs, nt, mod_row):
    groups = [refs[i * nt:(i + 1) * nt] for i in range(7)]
    ya_refs, yw_refs, gu_refs, gv_refs, hf_refs, hb_refs, h_refs = groups
    (mod_ref, wh_ref, wt_ref, wbr_ref, wo_ref, gpost_ref, mnorm_ref,
     lng_ref, lnb_ref, ws_ref, bs_ref, o_ref) = refs[7 * nt:]
    tm = CTX_LEN
    mult = mod_ref[0, mod_row:mod_row + 1, :]
    shift = mod_ref[0, mod_row + 1:mod_row + 2, :]
    gate = mod_ref[0, mod_row + 2:mod_row + 3, :]

    def rows(tile_refs):
        return jnp.concatenate([r[0] for r in tile_refs], axis=0)

    h = rows(h_refs)
    ya, yw = rows(ya_refs), rows(yw_refs)
    yg = jnp.concatenate([_spatial_gate_tile(gu_refs[i][0], gv_refs[i][0], lng_ref, lnb_ref, ws_ref, bs_ref)
                          for i in range(nt)], axis=0)
    xn = _norm_mod(h, mult, shift)

    def proj(where, width, offset=0):
        piece, col = where
        w_ref = (wh_ref, wt_ref)[piece]
        return jnp.dot(xn, w_ref[:, col + offset:col + offset + width], preferred_element_type=F32)

    hm = rows(hf_refs) + rows(hb_refs)
    parts = []
    for hd in range(M_HEADS):
        x = hm[:, hd * M_HD:(hd + 1) * M_HD]
        ms = jnp.mean(x * x, axis=-1, keepdims=True)
        parts.append(x * lax.rsqrt(ms + EPS) * mnorm_ref[:, hd * M_HD:(hd + 1) * M_HD])
    ym = _sigmoid(proj(GCOL_MO, BRANCH_W)) * jnp.concatenate(parts, axis=1)
    ys = (ya.astype(F32), yw.astype(F32), yg, ym)
    acc = None
    for k in range(N_BRANCH):
        yk = (ys[k] * _silu(proj(GCOL_BRANCH_GATE[k], BRANCH_W))).astype(BF16)
        bp = jnp.dot(yk, wbr_ref[k], preferred_element_type=F32)
        g = _sigmoid(proj(GCOL_MERGE, D_MODEL, k * D_MODEL))
        acc = g * bp if acc is None else acc + g * bp
    y = jnp.dot(acc.astype(BF16), wo_ref[...], preferred_element_type=F32)
    ms = jnp.mean(y * y, axis=-1, keepdims=True)
    yn = y * lax.rsqrt(ms + EPS) * gpost_ref[...]
    o_ref[0] = h + gate * yn


def _merge(ya, yw, p_main, hf, hb, h, mod8, wh_bf, wt_bf, wbr_bf, wo_bf, g_post, m_norm, ln_g, ln_b, ws_bf,
           bs_t, *, context):
    tm = CTX_LEN
    B, rows, _ = h.shape
    nt = 1 if context else 2
    n_steps = rows // (nt * tm)
    stream_off = 0 if context else CTX_LEN // tm
    bw = BRANCH_W

    def tiles(w, off, col=0):
        return [pl.BlockSpec((1, tm, w), lambda b, t, i=i: (b, nt * t + i + off, col)) for i in range(nt)]

    def const(shape):
        nd = len(shape)
        return pl.BlockSpec(shape, lambda b, t: (0,) * nd)

    token_specs = (tiles(bw, 0) + tiles(bw, 0)
                   + tiles(bw, stream_off, COL_GU // bw) + tiles(bw, stream_off, COL_GV // bw)
                   + tiles(bw, stream_off) + tiles(bw, stream_off) + tiles(D_MODEL, 0))
    token_args = [ya] * nt + [yw] * nt + [p_main] * (2 * nt) + [hf] * nt + [hb] * nt + [h] * nt
    return pl.pallas_call(
        functools.partial(_merge_kernel, nt=nt, mod_row=0 if context else 3),
        grid=(B, n_steps),
        in_specs=token_specs + [
            pl.BlockSpec((1, MOD_ROWS, D_MODEL), lambda b, t: (b, 0, 0)),
            const((D_MODEL, N_HEAD)), const((D_MODEL, N_TAIL)),
            const((N_BRANCH, bw, D_MODEL)), const((D_MODEL, D_MODEL)),
            const((1, D_MODEL)), const((1, bw)),
            const((1, bw)), const((1, bw)), const((B_GROUPS, CHUNK, CHUNK)), const((CHUNK, B_GROUPS))],
        out_specs=pl.BlockSpec((1, nt * tm, D_MODEL), lambda b, t: (b, t, 0)),
        out_shape=jax.ShapeDtypeStruct((B, rows, D_MODEL), F32),
        compiler_params=pltpu.CompilerParams(dimension_semantics=("parallel", "parallel"),
                                             vmem_limit_bytes=BIG_VMEM_LIMIT),
        name="merge_ctx" if context else "merge",
    )(*token_args, mod8, wh_bf, wt_bf, wbr_bf, wo_bf, g_post, m_norm, ln_g, ln_b, ws_bf, bs_t)


def _rope_tables(n_lat):
    rows = n_lat // GRID_W
    row = jnp.repeat(jnp.arange(rows), GRID_W).astype(F32)
    col = jnp.tile(jnp.arange(GRID_W), rows).astype(F32)
    n_freq = HEAD_DIM // 4
    inv = ROPE_THETA ** (-jnp.arange(n_freq, dtype=F32) / n_freq)
    ang = jnp.concatenate([row[:, None] * inv, col[:, None] * inv], axis=-1)
    cos, sin = jnp.cos(ang), jnp.sin(ang)
    cos_h = jnp.concatenate([cos, cos], axis=-1)
    sin_h = jnp.concatenate([-sin, sin], axis=-1)
    cos_t = jnp.concatenate([jnp.ones((CTX_LEN, HEAD_DIM), F32), cos_h], axis=0)
    sin_t = jnp.concatenate([jnp.zeros((CTX_LEN, HEAD_DIM), F32), sin_h], axis=0)
    return jnp.tile(cos_t, (1, LANES // HEAD_DIM)), jnp.tile(sin_t, (1, LANES // HEAD_DIM))


def _gather_cols(w, names):
    sl = _ref_slices()
    runs = []
    for n in names:
        start, size = sl[n]
        if runs and runs[-1][1] == start:
            runs[-1][1] = start + size
        else:
            runs.append([start, start + size])
    return jnp.concatenate([w[:, a:b].astype(BF16) for a, b in runs], axis=1)


def kernel(x, c, ctx, c_ctx, w_mod, b_mod, g_pre, g_post, w_in, a_q_norm, a_k_norm, w_sink, sg_ln_g, sg_ln_b,
           sg_w, sg_b, m_conv, m_b_i, m_b_f, m_norm, w_branch, w_out):
    B, n_lat, D = x.shape
    depth = w_mod.shape[0]
    assert depth == 2, "the context stream is only carried from the first layer to the last"
    T = CTX_LEN + n_lat
    n_ctx_chunks = CTX_LEN // CHUNK
    n_lat_chunks = n_lat // CHUNK
    rope = _rope_tables(n_lat)
    gmat = (jnp.arange(LANES)[:, None] // HEAD_DIM == jnp.arange(LANES)[None, :] // HEAD_DIM).astype(BF16)

    hc, hx = ctx, x
    mod_rows = 16
    cc = jnp.zeros((mod_rows, D), F32).at[:B].set(c).at[B].set(c_ctx)

    for l in range(depth):
        need_ctx = l < depth - 1
        mod = _modulation(cc, w_mod, b_mod[l][None, :], l)
        shift, scale, gate = mod[:, :D], mod[:, D:2 * D], mod[:, 2 * D:]
        mult = g_pre[l][None, :] * (1.0 + scale)
        zero = jnp.zeros((B, D), F32)
        mod8 = jnp.stack([jnp.broadcast_to(mult[B], (B, D)), jnp.broadcast_to(shift[B], (B, D)),
                          jnp.broadcast_to(gate[B], (B, D)), mult[:B], shift[:B], gate[:B], zero, zero],
                         axis=1)

        w_head = _gather_cols(w_in[l], _HEAD_ORDER)
        w_tail = _gather_cols(w_in[l], _TAIL_ORDER)
        w_lstm = jnp.pad(_gather_cols(w_in[l], ('m_i', 'm_f')), ((0, 0), (0, LANES - 2 * N_CHAINS)))
        p_main, p_gates, zq, zkt, zv = _in_projection(hc, hx, mod8, w_head, w_lstm, m_conv[l])

        qn = jnp.tile(a_q_norm[l], LANES // HEAD_DIM)[None, :]
        kn = jnp.tile(a_k_norm[l], LANES // HEAD_DIM)[None, :]
        ya, yw = _attention(p_main, rope, qn, kn, gmat, w_sink[l],
                            q_tile0=n_ctx_chunks, n_q_tiles=n_lat_chunks, nk=T, banded=True)
        if need_ctx:
            ya_c, yw_c = _attention(p_main, rope, qn, kn, gmat, w_sink[l],
                                    q_tile0=0, n_q_tiles=n_ctx_chunks, nk=CTX_LEN, banded=False)

        gates_row = jnp.swapaxes(p_gates[:, :, :2 * N_CHAINS], 1, 2)
        bias_row = jnp.broadcast_to(
            jnp.concatenate([m_b_i[l].reshape(-1), m_b_f[l].reshape(-1)])[:, None], (2 * N_CHAINS, T))
        ct, ur, ar = _mlstm_gates(gates_row, bias_row)
        hf, hb = _mlstm_scan(zq, zkt, zv, ct, ur, ar)

        weights = (mod8, w_head, w_tail, w_branch[l].astype(BF16), w_out[l].astype(BF16),
                   g_post[l][None, :], m_norm[l][None, :],
                   sg_ln_g[l][None, :], sg_ln_b[l][None, :], sg_w[l].astype(BF16), sg_b[l].T)
        hx_new = _merge(ya, yw, p_main, hf, hb, hx, *weights, context=False)
        if need_ctx:
            hc = _merge(ya_c, yw_c, p_main, hf, hb, hc, *weights, context=True)
        hx = hx_new
    return hx
```

```python
import functools

import jax
import jax.numpy as jnp
from jax import lax
from jax.experimental import pallas as pl
from jax.experimental.pallas import tpu as pltpu

F32 = jnp.float32
BF16 = jnp.bfloat16

D_MODEL = 1024
GRID_W = 64
CTX_LEN = 256
N_BRANCH = 4
BRANCH_W = 512
HEAD_DIM = 64
N_HEADS = 8
N_KV = 2
GROUP = N_HEADS // N_KV
WINDOW = 128
CHUNK = 128
B_GROUPS = 4
M_HEADS = 4
M_HD = 128
N_CHAINS = 2 * M_HEADS
ROPE_THETA = 10000.0
EPS = 1e-6
LOG2E = 1.4426950408889634
ATTN_TILES_PER_STEP = 2

LANES = 128
VMEM_LIMIT = 48 * 1024 * 1024
BIG_VMEM_LIMIT = 58 * 1024 * 1024

COL_AQ = 0
COL_WQ = 512
COL_AK = 1024
COL_AV = 1152
COL_WK = 1280
COL_WV = 1408
COL_GU = 1536
COL_GV = 2048
COL_MQKV = 2560
N_PLAIN = COL_MQKV
N_MAIN = 4096
_REF_LAYOUT = (
    ('a_q', 512), ('a_k', 128), ('a_v', 128), ('a_gate', 512),
    ('w_q', 512), ('w_k', 128), ('w_v', 128), ('w_gate', 512),
    ('g_u', 512), ('g_v', 512), ('g_gate', 512),
    ('m_qkv', 1536), ('m_i', 8), ('m_f', 8), ('m_o', 512), ('m_gate', 512),
    ('merge', 4096),
)
_MAIN_ORDER = ('a_q', 'w_q', 'a_k', 'a_v', 'w_k', 'w_v', 'g_u', 'g_v', 'm_qkv')
_HEAD_ORDER = tuple(name for name, _ in _REF_LAYOUT[:[n for n, _ in _REF_LAYOUT].index('m_i')])
_TAIL_ORDER = tuple(name for name, _ in _REF_LAYOUT[[n for n, _ in _REF_LAYOUT].index('m_f') + 1:])


def _ref_slices():
    out, start = {}, 0
    for name, size in _REF_LAYOUT:
        out[name] = (start, size)
        start += size
    return out


def _piece_cols(order):
    out, start = {}, 0
    for name in order:
        out[name] = start
        start += _ref_slices()[name][1]
    out['end'] = start
    return out


def _gate_col(name):
    if name in _HEAD_ORDER:
        return 0, _piece_cols(_HEAD_ORDER)[name]
    return 1, _piece_cols(_TAIL_ORDER)[name]


GCOL_BRANCH_GATE = tuple(_gate_col(n) for n in ('a_gate', 'w_gate', 'g_gate', 'm_gate'))
GCOL_MO = _gate_col('m_o')
GCOL_MERGE = _gate_col('merge')
N_HEAD = _piece_cols(_HEAD_ORDER)['end']
N_TAIL = _piece_cols(_TAIL_ORDER)['end']
assert all(c % LANES == 0 for _, c in GCOL_BRANCH_GATE + (GCOL_MO, GCOL_MERGE))


def _params(*sem):
    return pltpu.CompilerParams(dimension_semantics=sem, vmem_limit_bytes=VMEM_LIMIT)


def _sigmoid(x):
    return 1.0 / (1.0 + jnp.exp2(x * (-LOG2E)))


def _silu(x):
    return x * _sigmoid(x)


def _gelu_tanh(x):
    c = 0.7978845608028654
    return 0.5 * x * (1.0 + jnp.tanh(c * (x + 0.044715 * (x * x * x))))


def _log_sigmoid(x):
    return -(jnp.maximum(-x, 0.0) + jnp.log1p(jnp.exp(-jnp.abs(x))))


def _mod_kernel(c_ref, w_ref, b_ref, o_ref):
    a = _silu(c_ref[...]).astype(BF16)
    o_ref[...] = jnp.dot(a, w_ref[0].astype(BF16), preferred_element_type=F32) + b_ref[...]


def _modulation(cc, w_mod, b_mod, layer):
    rows = cc.shape[0]
    n = w_mod.shape[2]
    tn = 1024
    return pl.pallas_call(
        _mod_kernel,
        grid=(n // tn,),
        in_specs=[pl.BlockSpec((rows, D_MODEL), lambda j: (0, 0)),
                  pl.BlockSpec((1, D_MODEL, tn), lambda j: (layer, 0, j)),
                  pl.BlockSpec((1, tn), lambda j: (0, j))],
        out_specs=pl.BlockSpec((rows, tn), lambda j: (0, j)),
        out_shape=jax.ShapeDtypeStruct((rows, n), F32),
        compiler_params=_params("parallel"),
        name="modulation",
    )(cc, w_mod, b_mod)


MOD_ROWS = 8


def _norm_mod(x, mult, shift):
    ms = jnp.mean(x * x, axis=-1, keepdims=True)
    return (x * lax.rsqrt(ms + EPS) * mult + shift).astype(BF16)


def _conv_silu_slab(x, w_ref, col, no_prev, no_next):
    T = x.shape[0]
    prev = jnp.where(no_prev, 0.0, pltpu.roll(x, 1, axis=0))
    nxt = jnp.where(no_next, 0.0, pltpu.roll(x, T - 1, axis=0))
    return _silu(prev * w_ref[0:1, col:col + LANES] + x * w_ref[1:2, col:col + LANES]
                 + nxt * w_ref[2:3, col:col + LANES])


W_TILE = 2 * LANES


def _main_tiles():
    sl = _ref_slices()
    tiles = []
    for name in _MAIN_ORDER:
        start, size = sl[name]
        if size < W_TILE:
            if start % W_TILE == 0:
                tiles.append(start // W_TILE)
            continue
        assert start % W_TILE == 0 and size % W_TILE == 0
        tiles.extend(range(start // W_TILE, (start + size) // W_TILE))
    return tiles


def _inproj_kernel(tbl_ref, hc_ref, hx_ref, mod_ref, w0_ref, w1_ref, wl_ref, cw_ref,
                   o_ref, og_ref, zq_ref, zkt_ref, zv_ref, xn_ref, *, n_lat, n_plain):
    del tbl_ref
    w_refs = (w0_ref, w1_ref)
    j = pl.program_id(1)
    rb = CTX_LEN
    T = rb + n_lat

    @pl.when(j == 0)
    def _():
        xn_ref[0:rb, :] = _norm_mod(hc_ref[0], mod_ref[0, 0:1, :], mod_ref[0, 1:2, :])

        def norm_rows(r, carry):
            off = pl.multiple_of(r * rb, rb)
            x = hx_ref[0, pl.ds(off, rb), :]
            xn_ref[pl.ds(rb + off, rb), :] = _norm_mod(x, mod_ref[0, 3:4, :], mod_ref[0, 4:5, :])
            return carry

        lax.fori_loop(0, n_lat // rb, norm_rows, 0)
        og_ref[0] = jnp.dot(xn_ref[...], wl_ref[...], preferred_element_type=F32)

    @pl.when(j < n_plain)
    def _():
        w = jnp.concatenate([r[...] for r in w_refs], axis=1)
        o_ref[0] = jnp.dot(xn_ref[...], w, preferred_element_type=F32).astype(o_ref.dtype)

    for part, z_ref in enumerate((zq_ref, zkt_ref, zv_ref)):
        @pl.when(j == n_plain + part)
        def _(part=part, z_ref=z_ref):
            row = lax.broadcasted_iota(jnp.int32, (T, 1), 0)
            no_prev = jnp.logical_or(row == 0, row == CTX_LEN)
            no_next = jnp.logical_or(row == CTX_LEN - 1, row == T - 1)
            ys = [jnp.dot(xn_ref[...], r[...], preferred_element_type=F32) for r in w_refs]
            for s in range(BRANCH_W // LANES):
                lo = s * LANES
                y = ys[lo // W_TILE][:, lo % W_TILE:lo % W_TILE + LANES]
                z = _conv_silu_slab(y, cw_ref, part * BRANCH_W + lo, no_prev, no_next)
                if part == 0:
                    z_ref[0, :, lo:lo + LANES] = (z * (M_HD ** -0.5)).astype(z_ref.dtype)
                elif part == 1:
                    z_ref[0, lo:lo + LANES, :] = z.T.astype(z_ref.dtype)
                else:
                    z_ref[0, :, lo:lo + LANES] = z.astype(z_ref.dtype)


def _in_projection(hc, hx, mod8, w2_bf, wl_bf, conv_w):
    B, n_lat, _ = hx.shape
    T = CTX_LEN + n_lat
    tn = BRANCH_W
    per_step = tn // W_TILE
    assert per_step == 2
    tiles = _main_tiles()
    n_plain = N_PLAIN // tn
    n_steps = len(tiles) // per_step
    assert len(tiles) * W_TILE == N_MAIN and n_steps == n_plain + 3
    table = jnp.asarray(tiles, jnp.int32)

    def resident(shape):
        return pl.BlockSpec(shape, lambda b, j, tbl: (b, 0, 0), pipeline_mode=pl.Buffered(1))

    h_specs = [pl.BlockSpec((1, a.shape[1], D_MODEL), lambda b, j, tbl: (b, 0, 0)) for a in (hc, hx)]
    w_specs = [pl.BlockSpec((D_MODEL, W_TILE), lambda b, j, tbl, i=i: (0, tbl[per_step * j + i]))
               for i in range(per_step)]
    row_major = jax.ShapeDtypeStruct((B, T, BRANCH_W), BF16)
    grid_spec = pltpu.PrefetchScalarGridSpec(
        num_scalar_prefetch=1,
        grid=(B, n_steps),
        in_specs=h_specs + [pl.BlockSpec((1, MOD_ROWS, D_MODEL), lambda b, j, tbl: (b, 0, 0))] + w_specs
        + [pl.BlockSpec((D_MODEL, LANES), lambda b, j, tbl: (0, 0)),
           pl.BlockSpec((3, 3 * BRANCH_W), lambda b, j, tbl: (0, 0))],
        out_specs=[pl.BlockSpec((1, T, tn), lambda b, j, tbl: (b, 0, jnp.minimum(j, n_plain - 1))),
                   resident((1, T, LANES)),
                   resident((1, T, BRANCH_W)), resident((1, BRANCH_W, T)), resident((1, T, BRANCH_W))],
        scratch_shapes=[pltpu.VMEM((T, D_MODEL), BF16)])
    return pl.pallas_call(
        functools.partial(_inproj_kernel, n_lat=n_lat, n_plain=n_plain),
        grid_spec=grid_spec,
        out_shape=[jax.ShapeDtypeStruct((B, T, N_PLAIN), BF16),
                   jax.ShapeDtypeStruct((B, T, LANES), F32),
                   row_major, jax.ShapeDtypeStruct((B, BRANCH_W, T), BF16), row_major],
        compiler_params=pltpu.CompilerParams(dimension_semantics=("parallel", "arbitrary"),
                                             vmem_limit_bytes=BIG_VMEM_LIMIT),
        name="in_projection",
    )(table, hc, hx, mod8, w2_bf, w2_bf, wl_bf, conv_w)


def _rope_slab(x, cos, sin):
    lane = lax.broadcasted_iota(jnp.int32, x.shape, 1)
    first_half = (lane & (HEAD_DIM - 1)) < (HEAD_DIM // 2)
    partner = jnp.where(first_half,
                        pltpu.roll(x, LANES - HEAD_DIM // 2, axis=1),
                        pltpu.roll(x, HEAD_DIM // 2, axis=1))
    return x * cos + partner * sin


def _head_norm_slab(x, gain, gmat):
    ss = jnp.dot((x * x).astype(BF16), gmat, preferred_element_type=F32)
    return x * lax.rsqrt(ss * (1.0 / HEAD_DIM) + EPS) * gain


def _prep_kv(k_ref, v_ref, cosk_ref, sink_ref, kn_ref, gmat_ref, kp_ref, vt_ref, nk, use_norm):
    k = k_ref[0, 0:nk, :].astype(F32)
    if use_norm:
        k = _head_norm_slab(k, kn_ref[...], gmat_ref[...])
    k = _rope_slab(k, cosk_ref[0:nk, :], sink_ref[0:nk, :]).astype(BF16)
    for g in range(N_KV):
        kp_ref[g, 0:nk, :] = k[:, g * HEAD_DIM:(g + 1) * HEAD_DIM]
    for blk in range(nk // CHUNK):
        v = v_ref[0, blk * CHUNK:(blk + 1) * CHUNK, :].astype(F32)
        vt_ref[blk] = v.T.astype(BF16)


def _prep_qt(q_ref, cosq_ref, sinq_ref, qn_ref, gmat_ref, use_norm, r0, tq):
    cos = cosq_ref[r0:r0 + tq, :]
    sin = sinq_ref[r0:r0 + tq, :]
    xts = []
    for s in range(N_HEADS // 2):
        x = q_ref[0, r0:r0 + tq, s * LANES:(s + 1) * LANES].astype(F32)
        if use_norm:
            x = _head_norm_slab(x, qn_ref[...], gmat_ref[...])
        x = _rope_slab(x, cos, sin) * (HEAD_DIM ** -0.5 * LOG2E)
        xts.append(x.T.astype(BF16))
    per = GROUP // 2
    out = []
    for g in range(N_KV):
        parts = []
        for s in range(g * per, (g + 1) * per):
            parts.append(xts[s][:HEAD_DIM, :])
            parts.append(xts[s][HEAD_DIM:, :])
        out.append(jnp.concatenate(parts, axis=1))
    return out


def _heads_from_t(ot, tq):
    slabs = []
    for j in range(GROUP // 2):
        two = jnp.concatenate([ot[:, (2 * j) * tq:(2 * j + 1) * tq],
                               ot[:, (2 * j + 1) * tq:(2 * j + 2) * tq]], axis=0)
        slabs.append(two.T)
    return jnp.concatenate(slabs, axis=1)


def _window_pieces(wsink_ref, q_ref, cosq_ref, sinq_ref, o_ref, kp_ref, vt_ref, step, *,
                   tq, nsub, banded, n_lat_blocks):
    cols = GROUP * tq
    gw = GROUP * HEAD_DIM
    n_ctx_chunks = CTX_LEN // CHUNK
    lane = lax.broadcasted_iota(jnp.int32, (1, cols), 1)
    sink_rows = []
    for g in range(N_KV):
        sink_row = jnp.zeros((1, cols), F32)
        for h in range(GROUP):
            in_head = jnp.logical_and(lane >= h * tq, lane < (h + 1) * tq)
            sink_row = jnp.where(in_head, wsink_ref[g * GROUP + h] * LOG2E, sink_row)
        sink_rows.append(sink_row)
    if banded:
        diff = (lax.broadcasted_iota(jnp.int32, (CHUNK, cols), 0)
                - (lax.broadcasted_iota(jnp.int32, (CHUNK, cols), 1) & (tq - 1)))

    for sub in range(nsub):
        i = step * nsub + sub
        qts = _prep_qt(q_ref, cosq_ref, sinq_ref, None, None, False, sub * tq, tq)
        yield
        if banded:
            prev_blk = i + n_ctx_chunks - 1
            own_blk = i + n_ctx_chunks
            next_blk = jnp.minimum(i + n_ctx_chunks + 1, n_ctx_chunks + n_lat_blocks - 1)
            ok_prev = diff >= jnp.where(i > 0, 0, CHUNK)
            ok_next = diff <= jnp.where(i < n_lat_blocks - 1, 0, -CHUNK)
            plan = ([(cb, None) for cb in range(n_ctx_chunks)]
                    + [(prev_blk, ok_prev), (own_blk, None), (next_blk, ok_next)])
        else:
            plan = [(cb, None) for cb in range(n_ctx_chunks)]

        for g in range(N_KV):
            qt = qts[g]
            sink_row = sink_rows[g]
            blocks = []
            for blk, ok in plan:
                off = blk * CHUNK
                if not isinstance(off, int):
                    off = pl.multiple_of(off, CHUNK)
                st = jnp.dot(kp_ref[g, pl.ds(off, CHUNK), :], qt, preferred_element_type=F32)
                blocks.append((st if ok is None else jnp.where(ok, st, -jnp.inf), blk))
                yield
            m = sink_row
            for st, _ in blocks:
                m = jnp.maximum(m, jnp.max(st, axis=0, keepdims=True))
            l = jnp.exp2(sink_row - m)
            acc = None
            for st, blk in blocks:
                p = jnp.exp2(st - m)
                l = l + jnp.sum(p, axis=0, keepdims=True)
                pv = jnp.dot(vt_ref[blk, g * HEAD_DIM:(g + 1) * HEAD_DIM, :], p.astype(BF16),
                             preferred_element_type=F32)
                acc = pv if acc is None else acc + pv
                yield
            o_ref[0, sub * tq:(sub + 1) * tq, g * gw:(g + 1) * gw] = (
                _heads_from_t(acc / l, tq).astype(o_ref.dtype))
            yield


def _attn_kernel(wsink_ref, qa_ref, ka_ref, va_ref, qw_ref, kw_ref, vw_ref, cosq_ref, sinq_ref,
                 cosk_ref, sink_ref, qn_ref, kn_ref, gmat_ref, oa_ref, ow_ref,
                 kpa_ref, vta_ref, s_ref, kpw_ref, vtw_ref, *, tq, nsub, nk, ck, banded, n_lat_blocks):
    step = pl.program_id(1)

    @pl.when(step == 0)
    def _():
        _prep_kv(ka_ref, va_ref, cosk_ref, sink_ref, kn_ref, gmat_ref, kpa_ref, vta_ref, nk, True)
        _prep_kv(kw_ref, vw_ref, cosk_ref, sink_ref, None, None, kpw_ref, vtw_ref, nk, False)

    window = _window_pieces(wsink_ref, qw_ref, cosq_ref, sinq_ref, ow_ref, kpw_ref, vtw_ref, step,
                            tq=tq, nsub=nsub, banded=banded, n_lat_blocks=n_lat_blocks)

    qts = [_prep_qt(qa_ref, cosq_ref, sinq_ref, qn_ref, gmat_ref, True, sub * tq, tq) for sub in range(nsub)]
    items = [(sub, g) for sub in range(nsub) for g in range(N_KV)]
    n_chunks = nk // ck
    blocks_per_chunk = ck // CHUNK
    gw = GROUP * HEAD_DIM

    def score_chunk(k, c, m):
        sub, g = items[k]
        st = jnp.dot(kpa_ref[g, c * ck:(c + 1) * ck, :], qts[sub][g], preferred_element_type=F32)
        s_ref[k, c * ck:(c + 1) * ck, :] = st
        cm = jnp.max(st, axis=0, keepdims=True)
        return cm if m is None else jnp.maximum(m, cm)

    def value_chunk(k, c, m, l, acc):
        _, g = items[k]
        p = jnp.exp2(s_ref[k, c * ck:(c + 1) * ck, :] - m)
        cl = jnp.sum(p, axis=0, keepdims=True)
        vt = jnp.concatenate(
            [vta_ref[c * blocks_per_chunk + j, g * HEAD_DIM:(g + 1) * HEAD_DIM, :]
             for j in range(blocks_per_chunk)], axis=1)
        pv = jnp.dot(vt, p.astype(BF16), preferred_element_type=F32)
        return (cl, pv) if l is None else (l + cl, acc + pv)

    m_next = None
    for c in range(n_chunks):
        m_next = score_chunk(0, c, m_next)
        next(window, None)
    for k, (sub, g) in enumerate(items):
        m, m_next = m_next, None
        l = acc = None
        for c in range(n_chunks):
            l, acc = value_chunk(k, c, m, l, acc)
            if k + 1 < len(items):
                m_next = score_chunk(k + 1, c, m_next)
            next(window, None)
        oa_ref[0, sub * tq:(sub + 1) * tq, g * gw:(g + 1) * gw] = _heads_from_t(acc / l, tq).astype(oa_ref.dtype)
    for _ in window:
        pass


def _attention(p_main, rope, qn, kn, gmat, w_sink, *, q_tile0, n_q_tiles, nk, banded):
    B, T, _ = p_main.shape
    cos, sin = rope
    tq = CHUNK
    nsub = ATTN_TILES_PER_STEP
    tb = nsub * tq
    blk0 = q_tile0 // nsub
    ck = 256
    n_lat_blocks = (T - CTX_LEN) // CHUNK
    kernel = functools.partial(_attn_kernel, tq=tq, nsub=nsub, nk=nk, ck=ck, banded=banded,
                               n_lat_blocks=n_lat_blocks)

    def q_spec(col):
        return pl.BlockSpec((1, tb, BRANCH_W), lambda b, i: (b, i + blk0, col // BRANCH_W))

    def kv_spec(col):
        return pl.BlockSpec((1, T, LANES), lambda b, i: (b, 0, col // LANES))

    def const(shape):
        return pl.BlockSpec(shape, lambda b, i: (0, 0))

    out = jax.ShapeDtypeStruct((B, n_q_tiles * tq, BRANCH_W), BF16)
    kv_scratch = [pltpu.VMEM((N_KV, nk, HEAD_DIM), BF16), pltpu.VMEM((nk // CHUNK, LANES, CHUNK), BF16)]
    return pl.pallas_call(
        kernel,
        grid=(B, n_q_tiles // nsub),
        in_specs=[pl.BlockSpec(memory_space=pltpu.SMEM),
                  q_spec(COL_AQ), kv_spec(COL_AK), kv_spec(COL_AV),
                  q_spec(COL_WQ), kv_spec(COL_WK), kv_spec(COL_WV),
                  pl.BlockSpec((tb, LANES), lambda b, i: (i + blk0, 0)),
                  pl.BlockSpec((tb, LANES), lambda b, i: (i + blk0, 0)),
                  const((T, LANES)), const((T, LANES)),
                  const((1, LANES)), const((1, LANES)), const((LANES, LANES))],
        out_specs=[pl.BlockSpec((1, tb, BRANCH_W), lambda b, i: (b, i, 0)),
                   pl.BlockSpec((1, tb, BRANCH_W), lambda b, i: (b, i, 0))],
        out_shape=[out, out],
        scratch_shapes=kv_scratch + [pltpu.VMEM((nsub * N_KV, nk, GROUP * tq), F32)] + kv_scratch,
        compiler_params=_params("parallel", "arbitrary"),
        name="attention",
    )(w_sink, p_main, p_main, p_main, p_main, p_main, p_main, cos, sin, cos, sin, qn, kn, gmat)


def _spatial_gate_tile(gu, gv, lng_ref, lnb_ref, ws_ref, bs_ref):
    u = _gelu_tanh(gu.astype(F32))
    v = _gelu_tanh(gv.astype(F32))
    mu = jnp.mean(v, axis=-1, keepdims=True)
    vc = v - mu
    var = jnp.mean(vc * vc, axis=-1, keepdims=True)
    vn = (vc * lax.rsqrt(var + EPS) * lng_ref[...] + lnb_ref[...]).astype(BF16)
    gw = BRANCH_W // B_GROUPS
    chunks = []
    for c in range(gu.shape[0] // CHUNK):
        cols = []
        for g in range(B_GROUPS):
            mixed = jnp.dot(ws_ref[g], vn[c * CHUNK:(c + 1) * CHUNK, g * gw:(g + 1) * gw],
                            preferred_element_type=F32)
            cols.append(mixed + bs_ref[:, g:g + 1])
        chunks.append(jnp.concatenate(cols, axis=1))
    return u * jnp.concatenate(chunks, axis=0)


def _split3(x):
    x1 = x.astype(BF16)
    r = x - x1.astype(F32)
    x2 = r.astype(BF16)
    x3 = (r - x2.astype(F32)).astype(BF16)
    return x1, x2, x3


def _mgates_kernel(g_ref, bias_ref, ct_ref, ur_ref, ar_ref, *, T):
    L = CHUNK
    n_chunks = T // L
    n_ctx = CTX_LEN // L
    gall = g_ref[0] + bias_ref[...]
    i_all = gall[0:N_CHAINS]
    f_all = _log_sigmoid(gall[N_CHAINS:2 * N_CHAINS])
    fwd_rows = lax.broadcasted_iota(jnp.int32, (N_CHAINS, L), 0) < M_HEADS
    fwd_col = lax.broadcasted_iota(jnp.int32, (N_CHAINS, 1), 0) < M_HEADS
    lane = lax.broadcasted_iota(jnp.int32, (N_CHAINS, L), 1)
    s_idx = lax.broadcasted_iota(jnp.int32, (L, L), 0)
    t_idx = lax.broadcasted_iota(jnp.int32, (L, L), 1)
    sum_upto = jnp.where(s_idx <= t_idx, 1.0, 0.0).astype(BF16)
    sum_from = jnp.where(s_idx >= t_idx, 1.0, 0.0).astype(BF16)

    def cumsum_dir(f):
        bf = None
        bb = None
        for part in _split3(f):
            a = jnp.dot(part, sum_upto, preferred_element_type=F32)
            b = jnp.dot(part, sum_from, preferred_element_type=F32)
            bf = a if bf is None else bf + a
            bb = b if bb is None else bb + b
        return jnp.where(fwd_rows, bf, bb)

    def cummax_dir(u):
        xf = u
        xb = u
        sh = 1
        while sh < L:
            xf = jnp.maximum(xf, jnp.where(lane >= sh, pltpu.roll(xf, sh, axis=1), -jnp.inf))
            xb = jnp.maximum(xb, jnp.where(lane < L - sh, pltpu.roll(xb, L - sh, axis=1), -jnp.inf))
            sh *= 2
        return jnp.where(fwd_rows, xf, xb)

    b_c, u_c, mloc_c, bl_c, g_c, gmax_c = [], [], [], [], [], []
    for c in range(n_chunks):
        f = f_all[:, c * L:(c + 1) * L]
        i = i_all[:, c * L:(c + 1) * L]
        b = cumsum_dir(f)
        u = i - b
        bl = jnp.sum(f, axis=1, keepdims=True)
        g = bl + u
        b_c.append(b)
        u_c.append(u)
        mloc_c.append(b + cummax_dir(u))
        bl_c.append(bl)
        g_c.append(g)
        gmax_c.append(jnp.max(g, axis=1, keepdims=True))

    def scan(order):
        m = jnp.zeros((N_CHAINS, 1), F32)
        prev, new = {}, {}
        for c in order:
            prev[c] = m
            m = jnp.maximum(bl_c[c] + m, gmax_c[c])
            new[c] = m
        return prev, new

    order_f = list(range(n_chunks))
    order_b = [n_ctx - 1 - j for j in range(n_ctx)] + [n_chunks + n_ctx - 1 - j for j in range(n_ctx, n_chunks)]
    prev_f, new_f = scan(order_f)
    prev_b, new_b = scan(order_b)

    for c in range(n_chunks):
        m_prev = jnp.where(fwd_col, prev_f[c], prev_b[c])
        m_new = jnp.where(fwd_col, new_f[c], new_b[c])
        b = b_c[c]
        m_inter = b + m_prev
        m_t = jnp.maximum(m_inter, mloc_c[c])
        pack = jnp.concatenate([b - m_t, jnp.exp(m_inter - m_t), jnp.exp(-m_t),
                                jnp.zeros((LANES - 3 * N_CHAINS, L), F32)], axis=0)
        ct_ref[0, c * L:(c + 1) * L, :] = pack.T
        ur_ref[0, :, c * L:(c + 1) * L] = jnp.concatenate([u_c[c], jnp.exp(g_c[c] - m_new)], axis=0)
        ar_ref[0, c] = jnp.broadcast_to(jnp.exp(bl_c[c] + m_prev - m_new), (N_CHAINS, LANES))


def _mlstm_gates(gates_row, bias_row):
    B, _, T = gates_row.shape
    n_chunks = T // CHUNK
    return pl.pallas_call(
        functools.partial(_mgates_kernel, T=T),
        grid=(B,),
        in_specs=[pl.BlockSpec((1, 2 * N_CHAINS, T), lambda b: (b, 0, 0)),
                  pl.BlockSpec((2 * N_CHAINS, T), lambda b: (0, 0))],
        out_specs=[pl.BlockSpec((1, T, LANES), lambda b: (b, 0, 0)),
                   pl.BlockSpec((1, 2 * N_CHAINS, T), lambda b: (b, 0, 0)),
                   pl.BlockSpec((1, n_chunks, N_CHAINS, LANES), lambda b: (b, 0, 0, 0))],
        out_shape=[jax.ShapeDtypeStruct((B, T, LANES), F32),
                   jax.ShapeDtypeStruct((B, 2 * N_CHAINS, T), F32),
                   jax.ShapeDtypeStruct((B, n_chunks, N_CHAINS, LANES), F32)],
        compiler_params=_params("parallel"),
        name="mlstm_gates",
    )(gates_row, bias_row)


SCAN_BATCH = 8


def _mlstm_kernel(qf_ref, ktf_ref, vf_ref, ctf_ref, urf_ref, arf_ref,
                  qb_ref, ktb_ref, vb_ref, ctb_ref, urb_ref, arb_ref, of_ref, ob_ref, st_ref):
    L = CHUNK

    @pl.when(pl.program_id(1) == 0)
    def _():
        st_ref[...] = jnp.zeros_like(st_ref)

    t_idx = lax.broadcasted_iota(jnp.int32, (L, L), 0)
    s_idx = lax.broadcasted_iota(jnp.int32, (L, L), 1)
    ones = jnp.ones((L, M_HD), BF16)
    dirs = ((qf_ref, ktf_ref, vf_ref, ctf_ref, urf_ref, arf_ref, of_ref),
            (qb_ref, ktb_ref, vb_ref, ctb_ref, urb_ref, arb_ref, ob_ref))
    for bi in range(SCAN_BATCH):
        for d, (q_ref, kt_ref, v_ref, ct_ref, ur_ref, ar_ref, o_ref) in enumerate(dirs):
            mask = (s_idx <= t_idx) if d == 0 else (s_idx >= t_idx)
            ct = ct_ref[bi]
            ur = ur_ref[bi]
            ar = ar_ref[bi, 0]
            outs = []
            for h in range(M_HEADS):
                c = d * M_HEADS + h
                q = q_ref[bi, :, h * M_HD:(h + 1) * M_HD]
                kt = kt_ref[bi, h * M_HD:(h + 1) * M_HD, :]
                v = v_ref[bi, :, h * M_HD:(h + 1) * M_HD]
                v_ext = jnp.concatenate([v, ones], axis=1)
                cq = ct[:, c:c + 1]
                w_inter = ct[:, N_CHAINS + c:N_CHAINS + c + 1]
                e_negm = ct[:, 2 * N_CHAINS + c:2 * N_CHAINS + c + 1]
                wk_row = ur[N_CHAINS + c:N_CHAINS + c + 1, :]
                w_intra = jnp.exp(jnp.where(mask, cq + ur[c:c + 1, :], -jnp.inf))
                s = jnp.dot(q, kt, preferred_element_type=F32) * w_intra
                st_prev = st_ref[bi, d, h]
                lhs = jnp.concatenate([s.astype(BF16), (q.astype(F32) * w_inter).astype(BF16)], axis=1)
                rhs = jnp.concatenate([v_ext, st_prev.astype(BF16)], axis=0)
                tot = jnp.dot(lhs, rhs, preferred_element_type=F32)
                den = jnp.maximum(jnp.abs(tot[:, M_HD:]), e_negm)
                outs.append(tot[:, :M_HD] / den)
                kt_w = (kt.astype(F32) * wk_row).astype(BF16)
                a = jnp.concatenate([ar[c:c + 1, :], ar[c:c + 1, :]], axis=1)
                st_ref[bi, d, h] = a * st_prev + jnp.dot(kt_w, v_ext, preferred_element_type=F32)
            o_ref[0, bi] = jnp.concatenate(outs, axis=1).astype(o_ref.dtype)


def _mlstm_scan(zq, zkt, zv, ct, ur, ar):
    B, T, _ = zq.shape
    n_chunks = T // CHUNK
    n_ctx = CTX_LEN // CHUNK

    def fwd(j):
        return j

    def bwd(j):
        return jnp.where(j < n_ctx, n_ctx - 1 - j, n_chunks + n_ctx - 1 - j)

    nb = SCAN_BATCH

    def specs(order):
        return [pl.BlockSpec((nb, CHUNK, BRANCH_W), lambda b, j: (b, order(j), 0)),
                pl.BlockSpec((nb, BRANCH_W, CHUNK), lambda b, j: (b, 0, order(j))),
                pl.BlockSpec((nb, CHUNK, BRANCH_W), lambda b, j: (b, order(j), 0)),
                pl.BlockSpec((nb, CHUNK, LANES), lambda b, j: (b, order(j), 0)),
                pl.BlockSpec((nb, 2 * N_CHAINS, CHUNK), lambda b, j: (b, 0, order(j))),
                pl.BlockSpec((nb, 1, N_CHAINS, LANES), lambda b, j: (b, order(j), 0, 0))]

    out = pl.pallas_call(
        _mlstm_kernel,
        grid=(B // nb, n_chunks),
        in_specs=specs(fwd) + specs(bwd),
        out_specs=[pl.BlockSpec((1, nb, CHUNK, BRANCH_W), lambda b, j: (0, b, fwd(j), 0)),
                   pl.BlockSpec((1, nb, CHUNK, BRANCH_W), lambda b, j: (0, b, bwd(j), 0))],
        out_shape=[jax.ShapeDtypeStruct((1, B, T, BRANCH_W), BF16),
                   jax.ShapeDtypeStruct((1, B, T, BRANCH_W), BF16)],
        scratch_shapes=[pltpu.VMEM((nb, 2, M_HEADS, M_HD, 2 * M_HD), F32)],
        compiler_params=_params("parallel", "arbitrary"),
        name="mlstm_scan",
    )(zq, zkt, zv, ct, ur, ar, zq, zkt, zv, ct, ur, ar)
    return out[0][0], out[1][0]


def _merge_kernel(*refs, nt, mod_row):
    groups = [refs[i * nt:(i + 1) * nt] for i in range(7)]
    ya_refs, yw_refs, gu_refs, gv_refs, hf_refs, hb_refs, h_refs = groups
    (mod_ref, wh_ref, wt_ref, wbr_ref, wo_ref, gpost_ref, mnorm_ref,
     lng_ref, lnb_ref, ws_ref, bs_ref, o_ref) = refs[7 * nt:]
    tm = CTX_LEN
    mult = mod_ref[0, mod_row:mod_row + 1, :]
    shift = mod_ref[0, mod_row + 1:mod_row + 2, :]
    gate = mod_ref[0, mod_row + 2:mod_row + 3, :]

    def rows(tile_refs):
        return jnp.concatenate([r[0] for r in tile_refs], axis=0)

    h = rows(h_refs)
    ya, yw = rows(ya_refs), rows(yw_refs)
    yg = jnp.concatenate([_spatial_gate_tile(gu_refs[i][0], gv_refs[i][0], lng_ref, lnb_ref, ws_ref, bs_ref)
                          for i in range(nt)], axis=0)
    xn = _norm_mod(h, mult, shift)

    def proj(where, width, offset=0):
        piece, col = where
        w_ref = (wh_ref, wt_ref)[piece]
        return jnp.dot(xn, w_ref[:, col + offset:col + offset + width], preferred_element_type=F32)

    hm = rows(hf_refs).astype(F32) + rows(hb_refs).astype(F32)
    parts = []
    for hd in range(M_HEADS):
        x = hm[:, hd * M_HD:(hd + 1) * M_HD]
        ms = jnp.mean(x * x, axis=-1, keepdims=True)
        parts.append(x * lax.rsqrt(ms + EPS) * mnorm_ref[:, hd * M_HD:(hd + 1) * M_HD])
    ym = _sigmoid(proj(GCOL_MO, BRANCH_W)) * jnp.concatenate(parts, axis=1)
    ys = (ya.astype(F32), yw.astype(F32), yg, ym)
    acc = None
    for k in range(N_BRANCH):
        yk = (ys[k] * _silu(proj(GCOL_BRANCH_GATE[k], BRANCH_W))).astype(BF16)
        bp = jnp.dot(yk, wbr_ref[k], preferred_element_type=F32)
        g = _sigmoid(proj(GCOL_MERGE, D_MODEL, k * D_MODEL))
        acc = g * bp if acc is None else acc + g * bp
    y = jnp.dot(acc.astype(BF16), wo_ref[...], preferred_element_type=F32)
    ms = jnp.mean(y * y, axis=-1, keepdims=True)
    yn = y * lax.rsqrt(ms + EPS) * gpost_ref[...]
    o_ref[0] = h + gate * yn


def _merge(ya, yw, p_main, hf, hb, h, mod8, wh_bf, wt_bf, wbr_bf, wo_bf, g_post, m_norm, ln_g, ln_b, ws_bf,
           bs_t, *, context):
    tm = CTX_LEN
    B, rows, _ = h.shape
    nt = 1 if context else 2
    n_steps = rows // (nt * tm)
    stream_off = 0 if context else CTX_LEN // tm
    bw = BRANCH_W

    def tiles(w, off, col=0):
        return [pl.BlockSpec((1, tm, w), lambda b, t, i=i: (b, nt * t + i + off, col)) for i in range(nt)]

    def const(shape):
        nd = len(shape)
        return pl.BlockSpec(shape, lambda b, t: (0,) * nd)

    token_specs = (tiles(bw, 0) + tiles(bw, 0)
                   + tiles(bw, stream_off, COL_GU // bw) + tiles(bw, stream_off, COL_GV // bw)
                   + tiles(bw, stream_off) + tiles(bw, stream_off) + tiles(D_MODEL, 0))
    token_args = [ya] * nt + [yw] * nt + [p_main] * (2 * nt) + [hf] * nt + [hb] * nt + [h] * nt
    return pl.pallas_call(
        functools.partial(_merge_kernel, nt=nt, mod_row=0 if context else 3),
        grid=(B, n_steps),
        in_specs=token_specs + [
            pl.BlockSpec((1, MOD_ROWS, D_MODEL), lambda b, t: (b, 0, 0)),
            const((D_MODEL, N_HEAD)), const((D_MODEL, N_TAIL)),
            const((N_BRANCH, bw, D_MODEL)), const((D_MODEL, D_MODEL)),
            const((1, D_MODEL)), const((1, bw)),
            const((1, bw)), const((1, bw)), const((B_GROUPS, CHUNK, CHUNK)), const((CHUNK, B_GROUPS))],
        out_specs=pl.BlockSpec((1, nt * tm, D_MODEL), lambda b, t: (b, t, 0)),
        out_shape=jax.ShapeDtypeStruct((B, rows, D_MODEL), F32),
        compiler_params=pltpu.CompilerParams(dimension_semantics=("parallel", "parallel"),
                                             vmem_limit_bytes=BIG_VMEM_LIMIT),
        name="merge_ctx" if context else "merge",
    )(*token_args, mod8, wh_bf, wt_bf, wbr_bf, wo_bf, g_post, m_norm, ln_g, ln_b, ws_bf, bs_t)


def _rope_tables(n_lat):
    rows = n_lat // GRID_W
    row = jnp.repeat(jnp.arange(rows), GRID_W).astype(F32)
    col = jnp.tile(jnp.arange(GRID_W), rows).astype(F32)
    n_freq = HEAD_DIM // 4
    inv = ROPE_THETA ** (-jnp.arange(n_freq, dtype=F32) / n_freq)
    ang = jnp.concatenate([row[:, None] * inv, col[:, None] * inv], axis=-1)
    cos, sin = jnp.cos(ang), jnp.sin(ang)
    cos_h = jnp.concatenate([cos, cos], axis=-1)
    sin_h = jnp.concatenate([-sin, sin], axis=-1)
    cos_t = jnp.concatenate([jnp.ones((CTX_LEN, HEAD_DIM), F32), cos_h], axis=0)
    sin_t = jnp.concatenate([jnp.zeros((CTX_LEN, HEAD_DIM), F32), sin_h], axis=0)
    return jnp.tile(cos_t, (1, LANES // HEAD_DIM)), jnp.tile(sin_t, (1, LANES // HEAD_DIM))


def _gather_cols(w, names):
    sl = _ref_slices()
    runs = []
    for n in names:
        start, size = sl[n]
        if runs and runs[-1][1] == start:
            runs[-1][1] = start + size
        else:
            runs.append([start, start + size])
    return jnp.concatenate([w[:, a:b].astype(BF16) for a, b in runs], axis=1)


def kernel(x, c, ctx, c_ctx, w_mod, b_mod, g_pre, g_post, w_in, a_q_norm, a_k_norm, w_sink, sg_ln_g, sg_ln_b,
           sg_w, sg_b, m_conv, m_b_i, m_b_f, m_norm, w_branch, w_out):
    B, n_lat, D = x.shape
    depth = w_mod.shape[0]
    assert depth == 2, "the context stream is only carried from the first layer to the last"
    T = CTX_LEN + n_lat
    n_ctx_chunks = CTX_LEN // CHUNK
    n_lat_chunks = n_lat // CHUNK
    rope = _rope_tables(n_lat)
    gmat = (jnp.arange(LANES)[:, None] // HEAD_DIM == jnp.arange(LANES)[None, :] // HEAD_DIM).astype(BF16)

    hc, hx = ctx, x
    mod_rows = 16
    cc = jnp.zeros((mod_rows, D), F32).at[:B].set(c).at[B].set(c_ctx)
    tm_tok = CTX_LEN
    n_ctx_tiles = CTX_LEN // tm_tok
    n_lat_tiles = n_lat // tm_tok

    for l in range(depth):
        need_ctx = l < depth - 1
        mod = _modulation(cc, w_mod, b_mod[l][None, :], l)
        shift, scale, gate = mod[:, :D], mod[:, D:2 * D], mod[:, 2 * D:]
        mult = g_pre[l][None, :] * (1.0 + scale)
        zero = jnp.zeros((B, D), F32)
        mod8 = jnp.stack([jnp.broadcast_to(mult[B], (B, D)), jnp.broadcast_to(shift[B], (B, D)),
                          jnp.broadcast_to(gate[B], (B, D)), mult[:B], shift[:B], gate[:B], zero, zero],
                         axis=1)

        w_head = _gather_cols(w_in[l], _HEAD_ORDER)
        w_tail = _gather_cols(w_in[l], _TAIL_ORDER)
        w_lstm = jnp.pad(_gather_cols(w_in[l], ('m_i', 'm_f')), ((0, 0), (0, LANES - 2 * N_CHAINS)))
        p_main, p_gates, zq, zkt, zv = _in_projection(hc, hx, mod8, w_head, w_lstm, m_conv[l])

        qn = jnp.tile(a_q_norm[l], LANES // HEAD_DIM)[None, :]
        kn = jnp.tile(a_k_norm[l], LANES // HEAD_DIM)[None, :]
        ya, yw = _attention(p_main, rope, qn, kn, gmat, w_sink[l],
                            q_tile0=n_ctx_chunks, n_q_tiles=n_lat_chunks, nk=T, banded=True)
        if need_ctx:
            ya_c, yw_c = _attention(p_main, rope, qn, kn, gmat, w_sink[l],
                                    q_tile0=0, n_q_tiles=n_ctx_chunks, nk=CTX_LEN, banded=False)

        gates_row = jnp.swapaxes(p_gates[:, :, :2 * N_CHAINS], 1, 2)
        bias_row = jnp.broadcast_to(
            jnp.concatenate([m_b_i[l].reshape(-1), m_b_f[l].reshape(-1)])[:, None], (2 * N_CHAINS, T))
        ct, ur, ar = _mlstm_gates(gates_row, bias_row)
        hf, hb = _mlstm_scan(zq, zkt, zv, ct, ur, ar)

        weights = (mod8, w_head, w_tail, w_branch[l].astype(BF16), w_out[l].astype(BF16),
                   g_post[l][None, :], m_norm[l][None, :],
                   sg_ln_g[l][None, :], sg_ln_b[l][None, :], sg_w[l].astype(BF16), sg_b[l].T)
        hx_new = _merge(ya, yw, p_main, hf, hb, hx, *weights, context=False)
        if need_ctx:
            hc = _merge(ya_c, yw_c, p_main, hf, hb, hc, *weights, context=True)
        hx = hx_new
    return hx
```
